```python
import jax
import jax.numpy as jnp
from jax import lax
import numpy as np

D_MODEL = 2048
BATCH = 4
SEQ = 2048
DEPTH = 4

GRID_W = 64
CTX_LEN = 256
N_BRANCH = 3
MIX_W = 1024
A_HEADS = 8
A_DK = 128
A_DV = MIX_W // A_HEADS
B_HEADS = 8
B_DK = 64
B_DV = MIX_W // B_HEADS
GK_RANK = 16
GATE_LOGIT_NORMALIZER = 16.0
C_HEADS = 8
C_DH = MIX_W // C_HEADS
KR_MAX = 8
KC = 16
D_FF = 4 * D_MODEL
CHUNK = 16
ROPE_THETA = 10000.0
RMS_EPS = 1e-6
LN_EPS = 1e-5
MASK_VALUE = -1e30
DEEPNORM_ALPHA = (2 * DEPTH) ** 0.25
DEEPNORM_BETA = (8 * DEPTH) ** -0.25
IN_SPLITS = (A_HEADS * A_DK, MIX_W, MIX_W, A_HEADS * A_DK, A_HEADS * A_DK,
             B_HEADS * B_DK, B_HEADS * B_DK, MIX_W, MIX_W, 2 * GK_RANK,
             MIX_W, MIX_W, MIX_W,
             N_BRANCH * D_MODEL)
N_IN = int(sum(IN_SPLITS))
SPLIT_POINTS = tuple(int(s) for s in np.cumsum(IN_SPLITS)[:-1])

kernel_name = 'hybrid_dit_hgrn2_gla_natten'


def to_heads(a, n_heads):
    b, t, w = a.shape
    return a.reshape(b, t, n_heads, w // n_heads).transpose(0, 2, 1, 3)


def from_heads(a):
    b, h, t, d = a.shape
    return a.transpose(0, 2, 1, 3).reshape(b, t, h * d)


def layer_norm(x, g, b):
    xf = x.astype(jnp.float32)
    mu = xf.mean(-1, keepdims=True)
    var = jnp.square(xf - mu).mean(-1, keepdims=True)
    return ((xf - mu) * lax.rsqrt(var + LN_EPS) * g.astype(jnp.float32) + b.astype(jnp.float32)).astype(x.dtype)


def rms_norm_swish_gate(o, gain, gate):
    of = o.astype(jnp.float32)
    of = of * lax.rsqrt(jnp.mean(jnp.square(of), -1, keepdims=True) + RMS_EPS) * gain.astype(jnp.float32)
    return (from_heads(of) * jax.nn.silu(gate.astype(jnp.float32))).astype(gate.dtype)


def axial_rope(t, dim):
    half = dim // 2
    freqs = ROPE_THETA ** (-jnp.arange(0, half, 2, dtype=jnp.float32) / half)
    pos = jnp.arange(t)
    ang_r = (pos // GRID_W).astype(jnp.float32)[:, None] * freqs
    ang_c = (pos % GRID_W).astype(jnp.float32)[:, None] * freqs
    ang = jnp.concatenate([ang_r, ang_r, ang_c, ang_c], axis=-1)
    return jnp.cos(ang), jnp.sin(ang)


def rotate_half(u):
    h = u.shape[-1] // 2
    return jnp.concatenate([-u[..., h:], u[..., :h]], axis=-1)


def apply_axial_rope(x, cos, sin):
    x_r, x_c = jnp.split(x, 2, axis=-1)
    return x * cos + jnp.concatenate([rotate_half(x_r), rotate_half(x_c)], axis=-1) * sin


def chunk_gated_scan(q, k, v, log_a, s0):
    bsz, nh, t, _ = q.shape
    dv = v.shape[-1]
    n = t // CHUNK
    blk = lambda a: a.astype(jnp.float32).reshape(bsz, nh, n, CHUNK, a.shape[-1])
    q, k, v, log_a = blk(q), blk(k), blk(v), blk(log_a)
    b = jnp.cumsum(log_a, axis=3)
    lower = jnp.tril(jnp.ones((CHUNK, CHUNK), dtype=bool))[:, :, None]
    rel = b[:, :, :, :, None, :] - b[:, :, :, None, :, :]
    decay = jnp.where(lower, jnp.exp(jnp.where(lower, rel, 0.0)), 0.0)
    scores = jnp.einsum('bhntd,bhntsd,bhnsd->bhnts', q, decay, k)
    o = jnp.einsum('bhnts,bhnsv->bhntv', scores, v)
    b_end = b[:, :, :, -1, :]
    kv = jnp.einsum('bhnsd,bhnsv->bhndv', k * jnp.exp(b_end[:, :, :, None, :] - b), v)

    def step(s, inp):
        a_end, kv_c = inp
        return a_end[..., None] * s + kv_c, s

    s_final, s_prev = lax.scan(step, s0.astype(jnp.float32),
                               (jnp.moveaxis(jnp.exp(b_end), 2, 0), jnp.moveaxis(kv, 2, 0)))
    o = o + jnp.einsum('bhntd,nbhdv->bhntv', q * jnp.exp(b), s_prev)
    return o.reshape(bsz, nh, t, dv), s_final


def context_final_state(k, v, log_a):
    k, v, log_a = (a.astype(jnp.float32) for a in (k, v, log_a))
    b = jnp.cumsum(log_a, axis=2)
    return jnp.einsum('bhtd,bhtv->bhdv', k * jnp.exp(b[:, :, -1:, :] - b), v)


def one_direction(q_l, k_l, v_l, la_l, q_c, k_c, v_c, la_c, with_ctx_out):
    if with_ctx_out:
        bsz, nh, _, dk = k_c.shape
        o_c, s_c = chunk_gated_scan(q_c, k_c, v_c, la_c, jnp.zeros((bsz, nh, dk, v_c.shape[-1]), jnp.float32))
    else:
        o_c, s_c = None, context_final_state(k_c, v_c, la_c)
    o_l, _ = chunk_gated_scan(q_l, k_l, v_l, la_l, s_c)
    return o_l, o_c


def bidirectional_scan(q_l, k_l, v_l, la_l, q_c, k_c, v_c, la_c, with_ctx_out):
    rev = lambda a: jnp.flip(a, axis=2)
    f_l, f_c = one_direction(q_l, k_l[0], v_l, la_l[0], q_c, k_c[0], v_c, la_c[0], with_ctx_out)
    r_l, r_c = one_direction(rev(q_l), rev(k_l[1]), rev(v_l), rev(la_l[1]),
                             rev(q_c), rev(k_c[1]), rev(v_c), rev(la_c[1]), with_ctx_out)
    o_l = f_l + rev(r_l)
    o_c = f_c + rev(r_c) if with_ctx_out else None
    return o_l, o_c


def hgrn2_forget(f_raw, lb):
    f = f_raw.astype(jnp.float32)
    k = (1.0 - lb) * jax.nn.sigmoid(-f)
    log_a = jnp.log(lb + (1.0 - lb) * jax.nn.sigmoid(f))
    return to_heads(k, A_HEADS), to_heads(log_a, A_HEADS)


def hgrn2_branch(p_l, p_c, lb, norm_g, with_ctx_out):
    def prep(p):
        q, i, g, f_fwd, f_bwd = p
        q = to_heads(jax.nn.silu(q.astype(jnp.float32)), A_HEADS) * (A_DK ** -0.5)
        v = to_heads(i.astype(jnp.float32), A_HEADS)
        k_f, la_f = hgrn2_forget(f_fwd, lb[0])
        k_b, la_b = hgrn2_forget(f_bwd, lb[1])
        return q, (k_f, k_b), v, (la_f, la_b), g

    q_l, k_l, v_l, la_l, g_l = prep(p_l)
    q_c, k_c, v_c, la_c, g_c = prep(p_c)
    o_l, o_c = bidirectional_scan(q_l, k_l, v_l, la_l, q_c, k_c, v_c, la_c, with_ctx_out)
    y_l = rms_norm_swish_gate(o_l, norm_g, g_l)
    y_c = rms_norm_swish_gate(o_c, norm_g, g_c) if with_ctx_out else None
    return y_l, y_c


def gla_branch(p_l, p_c, w_gk2, b_gk2, norm_g, cos, sin, with_ctx_out):
    def decays(gk_low):
        lows = jnp.split(gk_low, 2, axis=-1)
        return tuple(to_heads(jax.nn.log_sigmoid((lo @ w_gk2[d] + b_gk2[d]).astype(jnp.float32))
                              / GATE_LOGIT_NORMALIZER, B_HEADS) for d, lo in enumerate(lows))

    q_l, k_l, v_l, g_l, gk_l = p_l
    q_c, k_c, v_c, g_c, gk_c = p_c
    scale = B_DK ** -0.5
    ql = apply_axial_rope(to_heads(q_l.astype(jnp.float32), B_HEADS), cos, sin) * scale
    kl = apply_axial_rope(to_heads(k_l.astype(jnp.float32), B_HEADS), cos, sin)
    vl = to_heads(v_l, B_HEADS)
    qc = to_heads(q_c.astype(jnp.float32), B_HEADS) * scale
    kc = to_heads(k_c, B_HEADS)
    vc = to_heads(v_c, B_HEADS)
    o_l, o_c = bidirectional_scan(ql, (kl, kl), vl, decays(gk_l), qc, (kc, kc), vc, decays(gk_c), with_ctx_out)
    y_l = rms_norm_swish_gate(o_l, norm_g, g_l)
    y_c = rms_norm_swish_gate(o_c, norm_g, g_c) if with_ctx_out else None
    return y_l, y_c


def neighbourhood_branch(p_l, p_c, rpb, with_ctx_out):
    q, k, v = (to_heads(a, C_HEADS) for a in p_l)
    qc, kc, vc = (to_heads(a, C_HEADS) for a in p_c)
    bsz, nh, t, dh = q.shape
    rows = t // GRID_W
    kr = min(KR_MAX, rows)
    scale = dh ** -0.5
    r = jnp.arange(rows)
    col = jnp.arange(GRID_W)
    row_idx = jnp.clip(r - kr // 2, 0, rows - kr)[:, None] + jnp.arange(kr)[None, :]
    col_start = jnp.clip(col - KC // 2, 0, GRID_W - KC)
    col_ok = (col[None, :] >= col_start[:, None]) & (col[None, :] < col_start[:, None] + KC)
    q_grid = q.reshape(bsz, nh, rows, GRID_W, dh)
    k_band = k.reshape(bsz, nh, rows, GRID_W, dh)[:, :, row_idx]
    v_band = v.reshape(bsz, nh, rows, GRID_W, dh)[:, :, row_idx]
    dr = row_idx - r[:, None] + (KR_MAX - 1)
    dc = jnp.clip(col[None, :] - col[:, None] + (KC - 1), 0, 2 * KC - 2)
    bias = rpb[:, dr[:, None, :, None], dc[None, :, None, :]].astype(jnp.float32)
    s_band = jnp.einsum('bhrqd,bhrkjd->bhrqkj', q_grid, k_band).astype(jnp.float32) * scale + bias
    s_band = jnp.where(col_ok[:, None, :], s_band, MASK_VALUE)
    s_ctx = jnp.einsum('bhtd,bhcd->bhtc', q, kc).astype(jnp.float32) * scale
    n_band = kr * GRID_W
    s = jnp.concatenate([s_band.reshape(bsz, nh, rows, GRID_W, n_band),
                         s_ctx.reshape(bsz, nh, rows, GRID_W, -1)], axis=-1)
    p = jax.nn.softmax(s, axis=-1).astype(v.dtype)
    o = (jnp.einsum('bhrqkj,bhrkjd->bhrqd', p[..., :n_band].reshape(bsz, nh, rows, GRID_W, kr, GRID_W), v_band)
         + jnp.einsum('bhrqc,bhcd->bhrqd', p[..., n_band:], vc))
    y_l = from_heads(o.reshape(bsz, nh, t, dh))
    if with_ctx_out:
        pc = jax.nn.softmax(jnp.einsum('bhsd,bhcd->bhsc', qc, kc).astype(jnp.float32) * scale, axis=-1)
        y_c = from_heads(jnp.einsum('bhsc,bhcd->bhsd', pc.astype(vc.dtype), vc))
    else:
        y_c = None
    return y_l, y_c


def merge_branches(ys, gate_logits, w_branch, w_out):
    gates = jax.nn.sigmoid(gate_logits).reshape(*gate_logits.shape[:-1], N_BRANCH, D_MODEL)
    z = jnp.einsum('btnm,nmd->btnd', jnp.stack(ys, axis=-2), w_branch)
    return jnp.sum(gates * z, axis=-2) @ w_out


def hybrid_mixer(h, h_c, w_in, lb, hgrn_norm_g, gla_w_gk2, gla_b_gk2, gla_norm_g, rpb,
                 w_branch, w_out, cos, sin, with_ctx_out):
    p_l = jnp.split(h @ w_in, SPLIT_POINTS, axis=-1)
    p_c = jnp.split(h_c @ w_in, SPLIT_POINTS, axis=-1)
    a_l, a_c = hgrn2_branch(p_l[0:5], p_c[0:5], lb, hgrn_norm_g, with_ctx_out)
    b_l, b_c = gla_branch(p_l[5:10], p_c[5:10], gla_w_gk2, gla_b_gk2, gla_norm_g, cos, sin, with_ctx_out)
    n_l, n_c = neighbourhood_branch(p_l[10:13], p_c[10:13], rpb, with_ctx_out)
    out_l = merge_branches((a_l, b_l, n_l), p_l[13], w_branch, w_out)
    out_c = merge_branches((a_c, b_c, n_c), p_c[13], w_branch, w_out) if with_ctx_out else None
    return out_l, out_c


def sqrelu_mlp(h, w1, w2):
    return jnp.square(jax.nn.relu(h @ w1)) @ w2


def setup_inputs(seed: int = 0) -> dict:
    key = jax.random.key(seed)
    ks = jax.random.split(key, 24)
    nrm = lambda k, shape, s: jax.random.normal(k, shape, jnp.float32) * s
    return {
        'x': nrm(ks[0], (BATCH, SEQ, D_MODEL), 1.0),
        'c': nrm(ks[1], (BATCH, D_MODEL), 1.0),
        'ctx': nrm(ks[2], (BATCH, CTX_LEN, D_MODEL), 1.0),
        'c_ctx': nrm(ks[3], (D_MODEL,), 1.0),
        'w_ada': nrm(ks[4], (DEPTH, D_MODEL, 6 * D_MODEL), 0.5 * D_MODEL ** -0.5),
        'b_ada': nrm(ks[5], (DEPTH, 6 * D_MODEL), 0.02),
        'w_in': nrm(ks[6], (DEPTH, D_MODEL, N_IN), D_MODEL ** -0.5),
        'hgrn_lb_logits': nrm(ks[7], (DEPTH, 2, A_HEADS * A_DK), 0.5),
        'hgrn_norm_g': 1.0 + nrm(ks[8], (DEPTH, A_DV), 0.02),
        'gla_w_gk2': nrm(ks[9], (DEPTH, 2, GK_RANK, B_HEADS * B_DK), GK_RANK ** -0.5),
        'gla_b_gk2': nrm(ks[10], (DEPTH, 2, B_HEADS * B_DK), 0.02),
        'gla_norm_g': 1.0 + nrm(ks[11], (DEPTH, B_DV), 0.02),
        'natten_rpb': nrm(ks[12], (DEPTH, C_HEADS, 2 * KR_MAX - 1, 2 * KC - 1), 0.02),
        'w_branch': nrm(ks[13], (DEPTH, N_BRANCH, MIX_W, D_MODEL), DEEPNORM_BETA * MIX_W ** -0.5),
        'w_out': nrm(ks[14], (DEPTH, D_MODEL, D_MODEL), DEEPNORM_BETA * D_MODEL ** -0.5),
        'ln1_g': 1.0 + nrm(ks[15], (DEPTH, D_MODEL), 0.02),
        'ln1_b': nrm(ks[16], (DEPTH, D_MODEL), 0.02),
        'ln2_g': 1.0 + nrm(ks[17], (DEPTH, D_MODEL), 0.02),
        'ln2_b': nrm(ks[18], (DEPTH, D_MODEL), 0.02),
        'w_mlp1': nrm(ks[19], (DEPTH, D_MODEL, D_FF), D_MODEL ** -0.5),
        'w_mlp2': nrm(ks[20], (DEPTH, D_FF, D_MODEL), DEEPNORM_BETA * D_FF ** -0.5),
    }


def reference(x, c, ctx, c_ctx, w_ada, b_ada, w_in, hgrn_lb_logits, hgrn_norm_g, gla_w_gk2, gla_b_gk2,
              gla_norm_g, natten_rpb, w_branch, w_out, ln1_g, ln1_b, ln2_g, ln2_b, w_mlp1, w_mlp2):
    cos, sin = axial_rope(x.shape[1], B_DK)
    lb_p = jax.nn.softmax(hgrn_lb_logits.astype(jnp.float32), axis=0)
    lb_cum = jnp.cumsum(lb_p, axis=0)
    lower_bounds = jnp.concatenate([jnp.zeros_like(lb_cum[:1]), lb_cum[:-1]], axis=0)
    for l in range(DEPTH):
        with_ctx_out = l < DEPTH - 1
        mod = jax.nn.silu(c) @ w_ada[l] + b_ada[l]
        mod_c = jax.nn.silu(c_ctx) @ w_ada[l] + b_ada[l]
        sh1, sc1, g1, sh2, sc2, g2 = (m[:, None, :] for m in jnp.split(mod, 6, axis=-1))
        sh1c, sc1c, g1c, sh2c, sc2c, g2c = jnp.split(mod_c, 6, axis=-1)
        h = x * (1.0 + sc1) + sh1
        h_c = ctx * (1.0 + sc1c) + sh1c
        mix, mix_c = hybrid_mixer(h, h_c, w_in[l], lower_bounds[l], hgrn_norm_g[l], gla_w_gk2[l], gla_b_gk2[l],
                                  gla_norm_g[l], natten_rpb[l], w_branch[l], w_out[l], cos, sin, with_ctx_out)
        x = layer_norm(DEEPNORM_ALPHA * x + g1 * mix, ln1_g[l], ln1_b[l])
        x = layer_norm(DEEPNORM_ALPHA * x + g2 * sqrelu_mlp(x * (1.0 + sc2) + sh2, w_mlp1[l], w_mlp2[l]),
                       ln2_g[l], ln2_b[l])
        if with_ctx_out:
            ctx = layer_norm(DEEPNORM_ALPHA * ctx + g1c * mix_c, ln1_g[l], ln1_b[l])
            ctx = layer_norm(DEEPNORM_ALPHA * ctx + g2c * sqrelu_mlp(ctx * (1.0 + sc2c) + sh2c, w_mlp1[l], w_mlp2[l]),
                             ln2_g[l], ln2_b[l])
    return x
```

```python
import functools

import jax
import jax.numpy as jnp
import numpy as np
from jax import lax
from jax.experimental import pallas as pl
from jax.experimental.pallas import tpu as pltpu

GRID_W = 64
N_BRANCH = 3
MIX_W = 1024
N_HEADS = 8
A_DK = 128
B_DK = 64
HEAD_DV = MIX_W // N_HEADS
GK_RANK = 16
GATE_LOGIT_NORMALIZER = 16.0
KR_MAX = 8
KC = 16
ROPE_THETA = 10000.0
RMS_EPS = 1e-6
LN_EPS = 1e-5
MASK_VALUE = -1e30

LANES = 128
SCAN_CHUNK = 128
VMEM_LIMIT_BYTES = 56 * 1024 * 1024

F32 = jnp.float32
BF16 = jnp.bfloat16


def _pick_tile(n, prefs):
    for t in prefs:
        if n % t == 0:
            return t
    return n


def _params(sem):
    return pltpu.CompilerParams(dimension_semantics=sem, vmem_limit_bytes=VMEM_LIMIT_BYTES)


def _dot(a, b):
    return jnp.dot(a, b, preferred_element_type=F32)


def _dot_nt(a, b):
    return lax.dot_general(a, b, (((1,), (1,)), ((), ())), preferred_element_type=F32)


def _ada_kernel(c_ref, w_ref, b_ref, o_ref):
    c = c_ref[...]
    s = (c * jax.nn.sigmoid(c)).astype(BF16)
    o_ref[...] = _dot(s, w_ref[...].astype(BF16)) + b_ref[...]


def _ada_all_layers(c_rows, w_ada, b_ada):
    depth, d, n = w_ada.shape
    tn = _pick_tile(n, (1024, 512, 256, 128))
    return pl.pallas_call(
        _ada_kernel,
        grid=(depth, n // tn),
        in_specs=[pl.BlockSpec((8, d), lambda l, j: (0, 0)),
                  pl.BlockSpec((None, d, tn), lambda l, j: (l, 0, j)),
                  pl.BlockSpec((None, 1, tn), lambda l, j: (l, 0, j))],
        out_specs=pl.BlockSpec((None, 8, tn), lambda l, j: (l, 0, j)),
        out_shape=jax.ShapeDtypeStruct((depth, 8, n), F32),
        compiler_params=_params(("arbitrary", "arbitrary")),
        name="ada_mod",
    )(c_rows, w_ada, b_ada.reshape(depth, 1, n))


def _modulate_kernel(ctx_len, x_ref, m_ref, h_ref):
    x = x_ref[...]
    rows = lax.broadcasted_iota(jnp.int32, x.shape, 0) + pl.program_id(1) * x.shape[0]
    is_ctx = rows < ctx_len
    sc = jnp.where(is_ctx, m_ref[1, 0:1, :], m_ref[0, 0:1, :])
    sh = jnp.where(is_ctx, m_ref[1, 1:2, :], m_ref[0, 1:2, :])
    h_ref[...] = (x * (1.0 + sc) + sh).astype(BF16)


def _modulate(x, mod_sel, ctx_len):
    b, tt, d = x.shape
    tm = _pick_tile(tt, (768, 384, 256, 128))
    return pl.pallas_call(
        functools.partial(_modulate_kernel, ctx_len),
        grid=(b, tt // tm),
        in_specs=[pl.BlockSpec((None, tm, d), lambda i, j: (i, j, 0)),
                  pl.BlockSpec((None, 2, 2, d), lambda i, j: (i, 0, 0, 0))],
        out_specs=pl.BlockSpec((None, tm, d), lambda i, j: (i, j, 0)),
        out_shape=jax.ShapeDtypeStruct((b, tt, d), BF16),
        compiler_params=_params(("arbitrary", "arbitrary")),
        name="modulate",
    )(x, mod_sel)


def _mm_kernel(sq_relu, a_ref, w_ref, o_ref):
    r = _dot(a_ref[...], w_ref[...])
    if sq_relu:
        r = jnp.square(jnp.maximum(r, 0.0))
    o_ref[...] = r.astype(o_ref.dtype)


def _matmul(a, w, sq_relu=False, name="matmul"):
    m, k = a.shape
    n = w.shape[1]
    tm = _pick_tile(m, (512, 256, 128))
    tn = _pick_tile(n, (1024, 512, 256, 128))
    return pl.pallas_call(
        functools.partial(_mm_kernel, sq_relu),
        grid=(n // tn, m // tm),
        in_specs=[pl.BlockSpec((tm, k), lambda j, i: (i, 0)),
                  pl.BlockSpec((k, tn), lambda j, i: (0, j))],
        out_specs=pl.BlockSpec((tm, tn), lambda j, i: (i, j)),
        out_shape=jax.ShapeDtypeStruct((m, n), BF16),
        compiler_params=_params(("arbitrary", "arbitrary")),
        name=name,
    )(a, w)


def _ref_rows(bc, m, rev):
    c, dk = bc.shape
    if 2 * m >= 16:
        pieces = []
        for p in range(c // (2 * m)):
            r = p * 2 * m + (m if rev else m - 1)
            pieces.append(jnp.broadcast_to(bc[r:r + 1, :], (2 * m, dk)))
        return pieces[0] if len(pieces) == 1 else jnp.concatenate(pieces, axis=0)
    bc3 = bc.reshape(c // 8, 8, dk)
    if m == 4:
        r = 4 if rev else 3
        return jnp.broadcast_to(bc3[:, r:r + 1, :], bc3.shape).reshape(c, dk)
    assert m == 2
    r0, r1 = (2, 6) if rev else (1, 5)
    sub = lax.broadcasted_iota(jnp.int32, bc3.shape, 1)
    lo = jnp.broadcast_to(bc3[:, r0:r0 + 1, :], bc3.shape)
    hi = jnp.broadcast_to(bc3[:, r1:r1 + 1, :], bc3.shape)
    return jnp.where(sub < 4, lo, hi).reshape(c, dk)


def _gla_chunk(q, k, v, la, st, tri, rev):
    c, dk = q.shape
    la_hi = la.astype(BF16)
    la_lo = (la - la_hi.astype(F32)).astype(BF16)
    bb = _dot(tri, jnp.concatenate([la_hi, la_lo], axis=1))
    bc = bb[:, :dk] + bb[:, dk:]

    row = lax.broadcasted_iota(jnp.int32, (c, dk), 0)
    txs = lax.broadcasted_iota(jnp.int32, (c, c), 0) ^ lax.broadcasted_iota(jnp.int32, (c, c), 1)
    scores = jnp.zeros((c, c), F32)
    m = c // 2
    while m >= 1:
        upper = (row & m) != 0
        q_role = jnp.logical_not(upper) if rev else upper
        if m == 1:
            e = jnp.where(q_role, la, 0.0)
        else:
            d = bc - _ref_rows(bc, m, rev)
            e = jnp.where(q_role, d, -d)
        qk = jnp.where(q_role, q, k) * jnp.exp(e)
        qt = jnp.where(q_role, qk, 0.0).astype(BF16)
        kt = jnp.where(q_role, 0.0, qk).astype(BF16)
        s_l = _dot_nt(qt, kt)
        scores = scores + (s_l if 2 * m == c else jnp.where(txs < 2 * m, s_l, 0.0))
        m //= 2

    o = _dot(scores.astype(BF16), v)
    o = o + jnp.sum(q * k, axis=1, keepdims=True) * v.astype(F32)
    o = o + _dot_nt((q * jnp.exp(bc)).astype(BF16), st.astype(BF16))
    tot = bc[0:1, :] if rev else bc[c - 1:c, :]
    kg = (k * jnp.exp(tot - bc)).astype(BF16)
    st_new = st * jnp.exp(tot) + _dot(v.T, kg)
    return o, st_new


def _scan_both_directions(q_s, kf_s, kb_s, laf_s, lab_s, v_ref, o_s, st_s, n_ctx_chunks):
    c = SCAN_CHUNK
    tt = q_s.shape[0]
    n_chunks = tt // c
    ti = lax.broadcasted_iota(jnp.int32, (c, c), 0)
    si = lax.broadcasted_iota(jnp.int32, (c, c), 1)
    tri_f = (si <= ti).astype(BF16)
    tri_b = (si >= ti).astype(BF16)

    def run(k_s, la_s, tri, rev):
        st_s[...] = jnp.zeros_like(st_s)

        def body(i, carry):
            if rev:
                ci = jnp.where(i < n_ctx_chunks, n_ctx_chunks - 1 - i, n_chunks - 1 + n_ctx_chunks - i)
            else:
                ci = i
            sl = pl.ds(pl.multiple_of(ci * c, c), c)
            o, st_new = _gla_chunk(q_s[sl, :], k_s[sl, :], v_ref[sl, :], la_s[sl, :], st_s[...], tri, rev)
            st_s[...] = st_new
            if rev:
                o_s[sl, :] = o_s[sl, :] + o
            else:
                o_s[sl, :] = o
            return carry

        lax.fori_loop(0, n_chunks, body, 0)

    run(kf_s, laf_s, tri_f, False)
    run(kb_s, lab_s, tri_b, True)


def _rms_gate(o, gain, gate):
    o = o * lax.rsqrt(jnp.mean(jnp.square(o), axis=-1, keepdims=True) + RMS_EPS) * gain
    return o * (gate * jax.nn.sigmoid(gate))


def _hgrn_kernel(n_ctx_chunks, q_ref, i_ref, g_ref, ff_ref, fb_ref, lb_ref, gain_ref, y_ref,
                 q_s, kf_s, kb_s, laf_s, lab_s, o_s, st_s):
    c = SCAN_CHUNK
    n_chunks = q_s.shape[0] // c
    lb_f = lb_ref[0:1, :]
    lb_b = lb_ref[1:2, :]

    def prep(i, carry):
        sl = pl.ds(pl.multiple_of(i * c, c), c)
        q = q_ref[sl, :].astype(F32)
        q_s[sl, :] = q * jax.nn.sigmoid(q) * (A_DK ** -0.5)
        for f_ref, lb, k_s, la_s in ((ff_ref, lb_f, kf_s, laf_s), (fb_ref, lb_b, kb_s, lab_s)):
            f = f_ref[sl, :].astype(F32)
            k_s[sl, :] = (1.0 - lb) * jax.nn.sigmoid(-f)
            la_s[sl, :] = jnp.log(lb + (1.0 - lb) * jax.nn.sigmoid(f))
        return carry

    lax.fori_loop(0, n_chunks, prep, 0)
    _scan_both_directions(q_s, kf_s, kb_s, laf_s, lab_s, i_ref, o_s, st_s, n_ctx_chunks)

    def fin(i, carry):
        sl = pl.ds(pl.multiple_of(i * c, c), c)
        y_ref[sl, :] = _rms_gate(o_s[sl, :], gain_ref[...], g_ref[sl, :].astype(F32)).astype(y_ref.dtype)
        return carry

    lax.fori_loop(0, n_chunks, fin, 0)


def _hgrn_mixer(p, lb, gain, ctx_len, col0):
    b, tt, _ = p.shape
    blk = lambda seg: pl.BlockSpec((None, tt, LANES), lambda i, h, s=seg: (i, 0, col0 + s * N_HEADS + h))
    seq = lambda: pltpu.VMEM((tt, LANES), F32)
    return pl.pallas_call(
        functools.partial(_hgrn_kernel, ctx_len // SCAN_CHUNK),
        grid=(b, N_HEADS),
        in_specs=[blk(0), blk(1), blk(2), blk(3), blk(4),
                  pl.BlockSpec((2, LANES), lambda i, h: (0, h)),
                  pl.BlockSpec((1, LANES), lambda i, h: (0, 0))],
        out_specs=pl.BlockSpec((None, tt, LANES), lambda i, h: (i, 0, h)),
        out_shape=jax.ShapeDtypeStruct((b, tt, MIX_W), BF16),
        scratch_shapes=[seq(), seq(), seq(), seq(), seq(), seq(), pltpu.VMEM((HEAD_DV, A_DK), F32)],
        compiler_params=_params(("arbitrary", "arbitrary")),
        name="hgrn2_mixer",
    )(p, p, p, p, p, lb, gain.reshape(1, HEAD_DV))


def _rope_rotate(x):
    n = x.shape[-1]
    lane = lax.broadcasted_iota(jnp.int32, x.shape, x.ndim - 1)
    first = (lane % 32) < 16
    return jnp.where(first, -pltpu.roll(x, n - 16, x.ndim - 1), pltpu.roll(x, 16, x.ndim - 1))


def _gla_kernel(n_ctx_chunks, q_ref, k_ref, v_ref, g_ref, gk_ref, wg_ref, bg_ref, cos_ref, sin_ref,
                gain_ref, y_ref, q_s, k_s, laf_s, lab_s, o_s, st_s):
    c = SCAN_CHUNK
    n_chunks = q_s.shape[0] // c
    odd = (pl.program_id(1) % 2) == 1

    def half(x):
        return jnp.where(odd, x[:, B_DK:], x[:, :B_DK])

    def prep(i, carry):
        sl = pl.ds(pl.multiple_of(i * c, c), c)
        cos = cos_ref[sl, :]
        sin = sin_ref[sl, :]
        q = q_ref[sl, :].astype(F32)
        k = k_ref[sl, :].astype(F32)
        q_s[sl, :] = half(q * cos + _rope_rotate(q) * sin) * (B_DK ** -0.5)
        k_s[sl, :] = half(k * cos + _rope_rotate(k) * sin)
        gk = gk_ref[sl, :]
        for d, la_s in ((0, laf_s), (1, lab_s)):
            z = _dot(gk, wg_ref[d]) + bg_ref[d]
            la_s[sl, :] = jax.nn.log_sigmoid(z) / GATE_LOGIT_NORMALIZER
        return carry

    lax.fori_loop(0, n_chunks, prep, 0)
    _scan_both_directions(q_s, k_s, k_s, laf_s, lab_s, v_ref, o_s, st_s, n_ctx_chunks)

    def fin(i, carry):
        sl = pl.ds(pl.multiple_of(i * c, c), c)
        y_ref[sl, :] = _rms_gate(o_s[sl, :], gain_ref[...], g_ref[sl, :].astype(F32)).astype(y_ref.dtype)
        return carry

    lax.fori_loop(0, n_chunks, fin, 0)


def _gla_mixer(p, p_gk, wg, bg, cos, sin, gain, ctx_len, col_q, col_k, col_v, col_g):
    b, tt, _ = p.shape
    pair = lambda c0: pl.BlockSpec((None, tt, LANES), lambda i, h: (i, 0, c0 + h // 2))
    head = lambda c0: pl.BlockSpec((None, tt, LANES), lambda i, h: (i, 0, c0 + h))
    seq = lambda w: pltpu.VMEM((tt, w), F32)
    return pl.pallas_call(
        functools.partial(_gla_kernel, ctx_len // SCAN_CHUNK),
        grid=(b, N_HEADS),
        in_specs=[pair(col_q), pair(col_k), head(col_v), head(col_g),
                  pl.BlockSpec((None, tt, LANES), lambda i, h: (i, 0, 0)),
                  pl.BlockSpec((2, None, LANES, B_DK), lambda i, h: (0, h, 0, 0)),
                  pl.BlockSpec((2, None, 1, B_DK), lambda i, h: (0, h, 0, 0)),
                  pl.BlockSpec((tt, LANES), lambda i, h: (0, 0)),
                  pl.BlockSpec((tt, LANES), lambda i, h: (0, 0)),
                  pl.BlockSpec((1, LANES), lambda i, h: (0, 0))],
        out_specs=pl.BlockSpec((None, tt, LANES), lambda i, h: (i, 0, h)),
        out_shape=jax.ShapeDtypeStruct((b, tt, MIX_W), BF16),
        scratch_shapes=[seq(B_DK), seq(B_DK), seq(B_DK), seq(B_DK), seq(HEAD_DV),
                        pltpu.VMEM((HEAD_DV, B_DK), F32)],
        compiler_params=_params(("arbitrary", "arbitrary")),
        name="gla_mixer",
    )(p, p, p, p, p_gk, wg, bg, cos, sin, gain.reshape(1, HEAD_DV))


def _natten_kernel(ctx_len, rows, kr, q_ref, k_ref, v_ref, bias_ref, y_ref):
    scale = HEAD_DV ** -0.5
    kc = k_ref[0:ctx_len, :]
    vc = v_ref[0:ctx_len, :]

    s = _dot_nt(q_ref[0:ctx_len, :], kc) * scale
    p = jnp.exp(s - jnp.max(s, axis=-1, keepdims=True))
    o = _dot(p.astype(BF16), vc) / jnp.sum(p, axis=-1, keepdims=True)
    y_ref[0:ctx_len, :] = o.astype(y_ref.dtype)

    def body(r, carry):
        start = jnp.clip(r - kr // 2, 0, rows - kr)
        variant = start - r + (KR_MAX - 1) - (KR_MAX - kr)
        q_sl = pl.ds(pl.multiple_of(ctx_len + r * GRID_W, GRID_W), GRID_W)
        b_sl = pl.ds(pl.multiple_of(ctx_len + start * GRID_W, GRID_W), kr * GRID_W)
        q = q_ref[q_sl, :]
        s_b = _dot_nt(q, k_ref[b_sl, :]) * scale + bias_ref[variant]
        s_c = _dot_nt(q, kc) * scale
        mx = jnp.maximum(jnp.max(s_b, axis=-1, keepdims=True), jnp.max(s_c, axis=-1, keepdims=True))
        p_b = jnp.exp(s_b - mx)
        p_c = jnp.exp(s_c - mx)
        den = jnp.sum(p_b, axis=-1, keepdims=True) + jnp.sum(p_c, axis=-1, keepdims=True)
        o = (_dot(p_b.astype(BF16), v_ref[b_sl, :]) + _dot(p_c.astype(BF16), vc)) / den
        y_ref[q_sl, :] = o.astype(y_ref.dtype)
        return carry

    lax.fori_loop(0, rows, body, 0)


def _natten_mixer(p, bias, ctx_len, col_q, col_k, col_v):
    b, tt, _ = p.shape
    rows = (tt - ctx_len) // GRID_W
    kr = min(KR_MAX, rows)
    nvar = bias.shape[1]
    head = lambda c0: pl.BlockSpec((None, tt, LANES), lambda i, h: (i, 0, c0 + h))
    return pl.pallas_call(
        functools.partial(_natten_kernel, ctx_len, rows, kr),
        grid=(b, N_HEADS),
        in_specs=[head(col_q), head(col_k), head(col_v),
                  pl.BlockSpec((None, nvar, GRID_W, kr * GRID_W), lambda i, h: (h, 0, 0, 0))],
        out_specs=pl.BlockSpec((None, tt, LANES), lambda i, h: (i, 0, h)),
        out_shape=jax.ShapeDtypeStruct((b, tt, MIX_W), BF16),
        compiler_params=_params(("arbitrary", "arbitrary")),
        name="natten_mixer",
    )(p, p, p, bias)


def _natten_bias_table(rpb, rows):
    kr = min(KR_MAX, rows)
    col = np.arange(GRID_W)
    col_start = np.clip(col - KC // 2, 0, GRID_W - KC)
    col_ok = (col[None, :] >= col_start[:, None]) & (col[None, :] < col_start[:, None] + KC)
    dc = np.clip(col[None, :] - col[:, None] + (KC - 1), 0, 2 * KC - 2)
    nvar = kr
    dr = np.arange(nvar)[:, None] + np.arange(kr)[None, :] + (KR_MAX - kr)
    tab = rpb[:, dr[:, None, :, None], dc[None, :, None, :]].astype(F32)
    tab = jnp.where(col_ok[None, None, :, None, :], tab, MASK_VALUE)
    return tab.reshape(rpb.shape[0], nvar, GRID_W, kr * GRID_W)


def _merge_kernel(ya_ref, yb_ref, yc_ref, w_ref, ga_ref, gb_ref, gc_ref, o_ref):
    acc = None
    for n, (y_ref, g_ref) in enumerate(((ya_ref, ga_ref), (yb_ref, gb_ref), (yc_ref, gc_ref))):
        z = jax.nn.sigmoid(g_ref[...].astype(F32)) * _dot(y_ref[...], w_ref[n])
        acc = z if acc is None else acc + z
    o_ref[...] = acc.astype(o_ref.dtype)


def _merge(ya, yb, yc, w_branch, p, col_gate, d_model):
    m = ya.shape[0]
    tm = _pick_tile(m, (512, 256, 128))
    tn = _pick_tile(d_model, (1024, 512, 256, 128))
    nj = d_model // tn
    yspec = pl.BlockSpec((tm, MIX_W), lambda j, i: (i, 0))
    gspec = lambda n: pl.BlockSpec((tm, tn), lambda j, i, n=n: (i, col_gate * LANES // tn + n * nj + j))
    return pl.pallas_call(
        _merge_kernel,
        grid=(nj, m // tm),
        in_specs=[yspec, yspec, yspec,
                  pl.BlockSpec((N_BRANCH, MIX_W, tn), lambda j, i: (0, 0, j)),
                  gspec(0), gspec(1), gspec(2)],
        out_specs=pl.BlockSpec((tm, tn), lambda j, i: (i, j)),
        out_shape=jax.ShapeDtypeStruct((m, d_model), BF16),
        compiler_params=_params(("arbitrary", "arbitrary")),
        name="branch_merge",
    )(ya, yb, yc, w_branch, p, p, p)


def _proj_ln_kernel(alpha, ctx_len, tiles_per_batch, emit_h, a_ref, w_ref, x_ref, mod_ref, ln_ref, *rest):
    if emit_h:
        xo_ref, ho_ref, acc_ref = rest
    else:
        xo_ref, acc_ref = rest
    kk = pl.program_id(1)

    @pl.when(kk == 0)
    def _():
        acc_ref[...] = jnp.zeros_like(acc_ref)

    acc_ref[...] += _dot(a_ref[...], w_ref[...])

    @pl.when(kk == pl.num_programs(1) - 1)
    def _():
        tm = acc_ref.shape[0]
        rows = lax.broadcasted_iota(jnp.int32, acc_ref.shape, 0) + (pl.program_id(0) % tiles_per_batch) * tm
        is_ctx = rows < ctx_len
        sel = lambda j: jnp.where(is_ctx, mod_ref[1, j:j + 1, :], mod_ref[0, j:j + 1, :])
        y = alpha * x_ref[...] + sel(0) * acc_ref[...]
        mu = jnp.mean(y, axis=-1, keepdims=True)
        yc = y - mu
        var = jnp.mean(jnp.square(yc), axis=-1, keepdims=True)
        xn = yc * lax.rsqrt(var + LN_EPS) * ln_ref[0:1, :] + ln_ref[1:2, :]
        xo_ref[...] = xn
        if emit_h:
            ho_ref[...] = (xn * (1.0 + sel(1)) + sel(2)).astype(BF16)


def _proj_ln(a, w, x, mod_sel, ln_gb, alpha, ctx_len, tt, emit_h, name):
    m, k = a.shape
    d = w.shape[1]
    tm = _pick_tile(tt, (384, 256, 128))
    tk = _pick_tile(k, (1024, 512, 256, 128))
    tpb = tt // tm
    out_shape = [jax.ShapeDtypeStruct((m, d), F32)]
    out_specs = [pl.BlockSpec((tm, d), lambda i, j: (i, 0))]
    if emit_h:
        out_shape.append(jax.ShapeDtypeStruct((m, d), BF16))
        out_specs.append(pl.BlockSpec((tm, d), lambda i, j: (i, 0)))
    res = pl.pallas_call(
        functools.partial(_proj_ln_kernel, alpha, ctx_len, tpb, emit_h),
        grid=(m // tm, k // tk),
        in_specs=[pl.BlockSpec((tm, tk), lambda i, j: (i, j)),
                  pl.BlockSpec((tk, d), lambda i, j: (j, 0)),
                  pl.BlockSpec((tm, d), lambda i, j: (i, 0)),
                  pl.BlockSpec((None, 2, 3, d), lambda i, j: (i // tpb, 0, 0, 0)),
                  pl.BlockSpec((2, d), lambda i, j: (0, 0))],
        out_specs=out_specs,
        out_shape=out_shape,
        scratch_shapes=[pltpu.VMEM((tm, d), F32)],
        compiler_params=_params(("arbitrary", "arbitrary")),
        name=name,
    )(a, w, x, mod_sel, ln_gb)
    return (res[0], res[1]) if emit_h else (res[0], None)


def _rope_tables(seq, ctx_len):
    half = B_DK // 2
    freqs = ROPE_THETA ** (-jnp.arange(0, half, 2, dtype=F32) / half)
    pos = jnp.arange(seq)
    ang_r = (pos // GRID_W).astype(F32)[:, None] * freqs
    ang_c = (pos % GRID_W).astype(F32)[:, None] * freqs
    ang = jnp.concatenate([ang_r, ang_r, ang_c, ang_c], axis=-1)
    ang = jnp.concatenate([jnp.zeros((ctx_len, B_DK), F32), ang], axis=0)
    ang = jnp.concatenate([ang, ang], axis=-1)
    return jnp.cos(ang), jnp.sin(ang)


def kernel(x, c, ctx, c_ctx, w_ada, b_ada, w_in, hgrn_lb_logits, hgrn_norm_g, gla_w_gk2, gla_b_gk2, gla_norm_g,
           natten_rpb, w_branch, w_out, ln1_g, ln1_b, ln2_g, ln2_b, w_mlp1, w_mlp2):
    bsz, seq, d = x.shape
    ctx_len = ctx.shape[1]
    depth = w_ada.shape[0]
    tt = ctx_len + seq
    m_tok = bsz * tt
    alpha = (2 * depth) ** 0.25
    assert ctx_len % SCAN_CHUNK == 0 and seq % SCAN_CHUNK == 0 and seq % GRID_W == 0 and bsz + 1 <= 8

    seg = N_HEADS * A_DK
    gk_lo = 5 * seg + 2 * N_HEADS * B_DK + 2 * MIX_W
    gk_hi = gk_lo + 2 * GK_RANK
    col_a = 0
    col_bq = 5 * seg // LANES
    col_bk = col_bq + N_HEADS * B_DK // LANES
    col_bv = col_bk + N_HEADS * B_DK // LANES
    col_bg = col_bv + MIX_W // LANES
    col_cq = col_bg + MIX_W // LANES
    col_ck = col_cq + MIX_W // LANES
    col_cv = col_ck + MIX_W // LANES
    col_gate = col_cv + MIX_W // LANES

    c_rows = jnp.zeros((8, d), F32).at[:bsz].set(c).at[bsz].set(c_ctx)
    mod = _ada_all_layers(c_rows, w_ada, b_ada).reshape(depth, 8, 6, d)

    def mod_sel(l, idx):
        lat = mod[l, :bsz][:, idx, :]
        cx = jnp.broadcast_to(mod[l, bsz][idx, :], lat.shape)
        return jnp.stack([lat, cx], axis=1)

    lb_p = jax.nn.softmax(hgrn_lb_logits.astype(F32), axis=0)
    lb_cum = jnp.cumsum(lb_p, axis=0)
    lower_bounds = jnp.concatenate([jnp.zeros_like(lb_cum[:1]), lb_cum[:-1]], axis=0)

    cos, sin = _rope_tables(seq, ctx_len)
    rows = seq // GRID_W

    xs = jnp.concatenate([ctx, x], axis=1)
    h = _modulate(xs, mod_sel(0, np.array([1, 0])), ctx_len).reshape(m_tok, d)
    xs = xs.reshape(m_tok, d)

    for l in range(depth):
        w_l = w_in[l]
        w_main = jnp.concatenate([w_l[:, :gk_lo], w_l[:, gk_hi:]], axis=1).astype(BF16)
        w_gk = jnp.pad(w_l[:, gk_lo:gk_hi], ((0, 0), (0, LANES - 2 * GK_RANK))).astype(BF16)
        p = _matmul(h, w_main, name="in_proj").reshape(bsz, tt, -1)
        p_gk = _matmul(h, w_gk, name="in_proj_gk").reshape(bsz, tt, LANES)

        wg = jnp.zeros((2, N_HEADS, LANES, B_DK), F32)
        w2 = gla_w_gk2[l].reshape(2, GK_RANK, N_HEADS, B_DK).transpose(0, 2, 1, 3)
        wg = wg.at[0, :, :GK_RANK].set(w2[0]).at[1, :, GK_RANK:2 * GK_RANK].set(w2[1]).astype(BF16)
        bg = gla_b_gk2[l].reshape(2, N_HEADS, 1, B_DK)

        y_a = _hgrn_mixer(p, lower_bounds[l], hgrn_norm_g[l], ctx_len, col_a)
        y_b = _gla_mixer(p, p_gk, wg, bg, cos, sin, gla_norm_g[l], ctx_len, col_bq, col_bk, col_bv, col_bg)
        y_c = _natten_mixer(p, _natten_bias_table(natten_rpb[l], rows), ctx_len, col_cq, col_ck, col_cv)

        merged = _merge(y_a.reshape(m_tok, MIX_W), y_b.reshape(m_tok, MIX_W), y_c.reshape(m_tok, MIX_W),
                        w_branch[l].astype(BF16), p.reshape(m_tok, -1), col_gate, d)
        xs, h2 = _proj_ln(merged, w_out[l].astype(BF16), xs, mod_sel(l, np.array([2, 4, 3])),
                          jnp.stack([ln1_g[l], ln1_b[l]]), alpha, ctx_len, tt, True, "out_proj_ln")
        u = _matmul(h2, w_mlp1[l].astype(BF16), sq_relu=True, name="mlp_up")
        last = l == depth - 1
        if last:
            nxt = jnp.zeros((bsz, 2, 3, d), F32).at[:, :, 0].set(mod_sel(l, np.array([5]))[:, :, 0])
        else:
            nxt = jnp.concatenate([mod_sel(l, np.array([5])), mod_sel(l + 1, np.array([1, 0]))], axis=2)
        xs, h = _proj_ln(u, w_mlp2[l].astype(BF16), xs, nxt, jnp.stack([ln2_g[l], ln2_b[l]]),
                         alpha, ctx_len, tt, not last, "mlp_down_ln")

    return xs.reshape(bsz, tt, d)[:, ctx_len:, :]
```

```python
import functools

import jax
import jax.numpy as jnp
import numpy as np
from jax import lax
from jax.experimental import pallas as pl
from jax.experimental.pallas import tpu as pltpu

GRID_W = 64
N_BRANCH = 3
MIX_W = 1024
N_HEADS = 8
A_DK = 128
B_DK = 64
HEAD_DV = MIX_W // N_HEADS
GK_RANK = 16
GATE_LOGIT_NORMALIZER = 16.0
KR_MAX = 8
KC = 16
ROPE_THETA = 10000.0
RMS_EPS = 1e-6
LN_EPS = 1e-5
MASK_VALUE = -1e30
LOG2E = 1.4426950408889634

LANES = 128
SCAN_CHUNK = 128
VMEM_LIMIT_BYTES = 56 * 1024 * 1024

F32 = jnp.float32
BF16 = jnp.bfloat16


def _pick_tile(n, prefs):
    for t in prefs:
        if n % t == 0:
            return t
    return n


def _params(sem):
    return pltpu.CompilerParams(dimension_semantics=sem, vmem_limit_bytes=VMEM_LIMIT_BYTES)


def _dot(a, b):
    return jnp.dot(a, b, preferred_element_type=F32)


def _dot_nt(a, b):
    return lax.dot_general(a, b, (((1,), (1,)), ((), ())), preferred_element_type=F32)


def _ada_kernel(c_ref, w_ref, b_ref, o_ref):
    c = c_ref[...]
    s = (c * jax.nn.sigmoid(c)).astype(BF16)
    o_ref[...] = _dot(s, w_ref[...].astype(BF16)) + b_ref[...]


def _ada_all_layers(c_rows, w_ada, b_ada):
    depth, d, n = w_ada.shape
    tn = _pick_tile(n, (1024, 512, 256, 128))
    return pl.pallas_call(
        _ada_kernel,
        grid=(depth, n // tn),
        in_specs=[pl.BlockSpec((8, d), lambda l, j: (0, 0)),
                  pl.BlockSpec((None, d, tn), lambda l, j: (l, 0, j)),
                  pl.BlockSpec((None, 1, tn), lambda l, j: (l, 0, j))],
        out_specs=pl.BlockSpec((None, 8, tn), lambda l, j: (l, 0, j)),
        out_shape=jax.ShapeDtypeStruct((depth, 8, n), F32),
        compiler_params=_params(("arbitrary", "arbitrary")),
        name="ada_mod",
    )(c_rows, w_ada, b_ada.reshape(depth, 1, n))


def _modulate_kernel(ctx_len, x_ref, m_ref, h_ref):
    x = x_ref[...]
    rows = lax.broadcasted_iota(jnp.int32, x.shape, 0) + pl.program_id(1) * x.shape[0]
    is_ctx = rows < ctx_len
    sc = jnp.where(is_ctx, m_ref[1, 0:1, :], m_ref[0, 0:1, :])
    sh = jnp.where(is_ctx, m_ref[1, 1:2, :], m_ref[0, 1:2, :])
    h_ref[...] = (x * (1.0 + sc) + sh).astype(BF16)


def _modulate(x, mod_sel, ctx_len):
    b, tt, d = x.shape
    tm = _pick_tile(tt, (768, 384, 256, 128))
    return pl.pallas_call(
        functools.partial(_modulate_kernel, ctx_len),
        grid=(b, tt // tm),
        in_specs=[pl.BlockSpec((None, tm, d), lambda i, j: (i, j, 0)),
                  pl.BlockSpec((None, 2, 2, d), lambda i, j: (i, 0, 0, 0))],
        out_specs=pl.BlockSpec((None, tm, d), lambda i, j: (i, j, 0)),
        out_shape=jax.ShapeDtypeStruct((b, tt, d), BF16),
        compiler_params=_params(("arbitrary", "arbitrary")),
        name="modulate",
    )(x, mod_sel)


def _mm_kernel(sq_relu, a_ref, w_ref, o_ref):
    r = _dot(a_ref[...], w_ref[...])
    if sq_relu:
        r = jnp.square(jnp.maximum(r, 0.0))
    o_ref[...] = r.astype(o_ref.dtype)


def _matmul(a, w, sq_relu=False, name="matmul"):
    m, k = a.shape
    n = w.shape[1]
    tm = _pick_tile(m, (512, 256, 128))
    tn = _pick_tile(n, (1024, 512, 256, 128))
    return pl.pallas_call(
        functools.partial(_mm_kernel, sq_relu),
        grid=(n // tn, m // tm),
        in_specs=[pl.BlockSpec((tm, k), lambda j, i: (i, 0)),
                  pl.BlockSpec((k, tn), lambda j, i: (0, j))],
        out_specs=pl.BlockSpec((tm, tn), lambda j, i: (i, j)),
        out_shape=jax.ShapeDtypeStruct((m, n), BF16),
        compiler_params=_params(("arbitrary", "arbitrary")),
        name=name,
    )(a, w)


def _level_operand(q, k, la, bc, m, rev):
    c, dk = q.shape
    if m >= 8:
        pieces = []
        for jb in range(c // m):
            rows = slice(jb * m, (jb + 1) * m)
            r = (jb // 2) * 2 * m + (m if rev else m - 1)
            cvec = bc[r:r + 1, :]
            if (jb % 2 == 1) != rev:
                pieces.append(q[rows, :] * jnp.exp2((bc[rows, :] - cvec) * LOG2E))
            else:
                pieces.append(k[rows, :] * jnp.exp2((cvec - bc[rows, :]) * LOG2E))
        return jnp.concatenate(pieces, axis=0)
    shape3 = (c // 8, 8, dk)
    sub = lax.broadcasted_iota(jnp.int32, (1, 8, dk), 1)
    upper = (sub & m) != 0
    q_role = jnp.logical_not(upper) if rev else upper
    q3, k3, bc3 = q.reshape(shape3), k.reshape(shape3), bc.reshape(shape3)
    if m == 1:
        e2 = la.reshape(shape3) * jnp.where(q_role, LOG2E, 0.0)
    else:
        if m == 4:
            r = 4 if rev else 3
            c3 = bc3[:, r:r + 1, :]
        else:
            r0, r1 = (2, 6) if rev else (1, 5)
            c3 = jnp.where(sub < 4, bc3[:, r0:r0 + 1, :], bc3[:, r1:r1 + 1, :])
        e2 = (bc3 - c3) * jnp.where(q_role, LOG2E, -LOG2E)
    return (jnp.where(q_role, q3, k3) * jnp.exp2(e2)).reshape(c, dk)


def _level_matrix(c, rev):
    t = lax.broadcasted_iota(jnp.int32, (c, c), 0)
    s = lax.broadcasted_iota(jnp.int32, (c, c), 1)
    x = t ^ s
    lvl = jnp.full((c, c), -1, jnp.int32)
    b = 1
    while b < c:
        lvl = lvl + (x >= b).astype(jnp.int32)
        b *= 2
    return jnp.where((t < s) if rev else (t > s), lvl, -1)


def _gla_chunk(q, k, la, vs, sts, tri, lvl, head_lanes, rev):
    c, dk = q.shape
    nh = len(vs)
    la_hi = la.astype(BF16)
    la_lo = (la - la_hi.astype(F32)).astype(BF16)
    bb = _dot(tri, jnp.concatenate([la_hi, la_lo], axis=1))
    bc = bb[:, :dk] + bb[:, dk:]

    scores = [jnp.zeros((c, c), F32) for _ in range(nh)]
    m, level = c // 2, (c // 2).bit_length() - 1
    while m >= 1:
        x = _level_operand(q, k, la, bc, m, rev).astype(BF16)
        for h in range(nh):
            xh = x if nh == 1 else x * head_lanes[h][1]
            scores[h] = jnp.where(lvl == level, _dot_nt(xh, x), scores[h])
        m, level = m // 2, level - 1

    qk = q * k
    qg = (q * jnp.exp(bc)).astype(BF16)
    tot = bc[0:1, :] if rev else bc[c - 1:c, :]
    kg = (k * jnp.exp(tot - bc)).astype(BF16)
    dec = jnp.exp(tot)
    outs, new_sts = [], []
    for h in range(nh):
        v = vs[h]
        qk_h = qk if nh == 1 else jnp.where(head_lanes[h][0], qk, 0.0)
        qg_h = qg if nh == 1 else qg * head_lanes[h][1]
        o = _dot(scores[h].astype(BF16), v)
        o = o + jnp.sum(qk_h, axis=1, keepdims=True) * v.astype(F32)
        o = o + _dot_nt(qg_h, sts[h].astype(BF16))
        outs.append(o)
        new_sts.append(sts[h] * dec + _dot(v.T, kg))
    return outs, new_sts


def _scan_both_directions(q_s, kf_s, kb_s, laf_s, lab_s, v_ref, of_s, ob_s, stf_s, stb_s, n_ctx_chunks):
    c = SCAN_CHUNK
    n_chunks = q_s.shape[0] // c
    nh = stf_s.shape[0]
    dv = v_ref.shape[1] // nh
    ti = lax.broadcasted_iota(jnp.int32, (c, c), 0)
    si = lax.broadcasted_iota(jnp.int32, (c, c), 1)
    tri_f = (si <= ti).astype(BF16)
    tri_b = (si >= ti).astype(BF16)
    lvl_f = _level_matrix(c, False)
    lvl_b = _level_matrix(c, True)
    lane = lax.broadcasted_iota(jnp.int32, (c, LANES), 1) // (LANES // nh)
    head_lanes = [(lane == h, (lane == h).astype(BF16)) for h in range(nh)]
    stf_s[...] = jnp.zeros_like(stf_s)
    stb_s[...] = jnp.zeros_like(stb_s)

    def one(sl, k_s, la_s, st_s, o_s, tri, lvl, rev):
        vs = [v_ref[sl, h * dv:(h + 1) * dv] for h in range(nh)]
        outs, sts = _gla_chunk(q_s[sl, :], k_s[sl, :], la_s[sl, :], vs, [st_s[h] for h in range(nh)],
                               tri, lvl, head_lanes, rev)
        for h in range(nh):
            st_s[h] = sts[h]
            o_s[sl, h * dv:(h + 1) * dv] = outs[h]

    def body(i, carry):
        one(pl.ds(pl.multiple_of(i * c, c), c), kf_s, laf_s, stf_s, of_s, tri_f, lvl_f, False)
        cb = jnp.where(i < n_ctx_chunks, n_ctx_chunks - 1 - i, n_chunks - 1 + n_ctx_chunks - i)
        one(pl.ds(pl.multiple_of(cb * c, c), c), kb_s, lab_s, stb_s, ob_s, tri_b, lvl_b, True)
        return carry

    lax.fori_loop(0, n_chunks, body, 0)


def _rms_gate(o, gain, gate):
    o = o * lax.rsqrt(jnp.mean(jnp.square(o), axis=-1, keepdims=True) + RMS_EPS) * gain
    return o * (gate * jax.nn.sigmoid(gate))


def _hgrn_kernel(n_ctx_chunks, q_ref, i_ref, g_ref, ff_ref, fb_ref, lb_ref, gain_ref, y_ref,
                 q_s, kf_s, kb_s, laf_s, lab_s, of_s, ob_s, stf_s, stb_s):
    c = SCAN_CHUNK
    n_chunks = q_s.shape[0] // c
    lb_f = lb_ref[0:1, :]
    lb_b = lb_ref[1:2, :]

    def prep(i, carry):
        sl = pl.ds(pl.multiple_of(i * c, c), c)
        q = q_ref[sl, :].astype(F32)
        q_s[sl, :] = q * jax.nn.sigmoid(q) * (A_DK ** -0.5)
        for f_ref, lb, k_s, la_s in ((ff_ref, lb_f, kf_s, laf_s), (fb_ref, lb_b, kb_s, lab_s)):
            f = f_ref[sl, :].astype(F32)
            k_s[sl, :] = (1.0 - lb) * jax.nn.sigmoid(-f)
            la_s[sl, :] = jnp.log(lb + (1.0 - lb) * jax.nn.sigmoid(f))
        return carry

    lax.fori_loop(0, n_chunks, prep, 0)
    _scan_both_directions(q_s, kf_s, kb_s, laf_s, lab_s, i_ref, of_s, ob_s, stf_s, stb_s, n_ctx_chunks)

    def fin(i, carry):
        sl = pl.ds(pl.multiple_of(i * c, c), c)
        o = of_s[sl, :] + ob_s[sl, :]
        y_ref[sl, :] = _rms_gate(o, gain_ref[...], g_ref[sl, :].astype(F32)).astype(y_ref.dtype)
        return carry

    lax.fori_loop(0, n_chunks, fin, 0)


def _hgrn_mixer(p, lb, gain, ctx_len, col0):
    b, tt, _ = p.shape
    blk = lambda seg: pl.BlockSpec((None, tt, LANES), lambda i, h, s=seg: (i, 0, col0 + s * N_HEADS + h))
    seq = lambda: pltpu.VMEM((tt, LANES), F32)
    return pl.pallas_call(
        functools.partial(_hgrn_kernel, ctx_len // SCAN_CHUNK),
        grid=(b, N_HEADS),
        in_specs=[blk(0), blk(1), blk(2), blk(3), blk(4),
                  pl.BlockSpec((2, LANES), lambda i, h: (0, h)),
                  pl.BlockSpec((1, LANES), lambda i, h: (0, 0))],
        out_specs=pl.BlockSpec((None, tt, LANES), lambda i, h: (i, 0, h)),
        out_shape=jax.ShapeDtypeStruct((b, tt, MIX_W), BF16),
        scratch_shapes=[seq(), seq(), seq(), seq(), seq(), seq(), seq(),
                        pltpu.VMEM((1, HEAD_DV, LANES), F32), pltpu.VMEM((1, HEAD_DV, LANES), F32)],
        compiler_params=_params(("arbitrary", "arbitrary")),
        name="hgrn2_mixer",
    )(p, p, p, p, p, lb, gain.reshape(1, HEAD_DV))


def _rope_rotate(x):
    n = x.shape[-1]
    lane = lax.broadcasted_iota(jnp.int32, x.shape, x.ndim - 1)
    first = (lane % 32) < 16
    return jnp.where(first, -pltpu.roll(x, n - 16, x.ndim - 1), pltpu.roll(x, 16, x.ndim - 1))


def _gla_kernel(n_ctx_chunks, q_ref, k_ref, v_ref, g_ref, gk_ref, wg_ref, bg_ref, cos_ref, sin_ref,
                gain_ref, y_ref, q_s, k_s, laf_s, lab_s, of_s, ob_s, stf_s, stb_s):
    c = SCAN_CHUNK
    n_chunks = q_s.shape[0] // c
    nh = LANES // B_DK

    def prep(i, carry):
        sl = pl.ds(pl.multiple_of(i * c, c), c)
        cos = cos_ref[sl, :]
        sin = sin_ref[sl, :]
        q = q_ref[sl, :].astype(F32)
        k = k_ref[sl, :].astype(F32)
        q_s[sl, :] = (q * cos + _rope_rotate(q) * sin) * (B_DK ** -0.5)
        k_s[sl, :] = k * cos + _rope_rotate(k) * sin
        gk = gk_ref[sl, :]
        for d, la_s in ((0, laf_s), (1, lab_s)):
            z = _dot(gk, wg_ref[d]) + bg_ref[d]
            la_s[sl, :] = jax.nn.log_sigmoid(z) / GATE_LOGIT_NORMALIZER
        return carry

    lax.fori_loop(0, n_chunks, prep, 0)
    _scan_both_directions(q_s, k_s, k_s, laf_s, lab_s, v_ref, of_s, ob_s, stf_s, stb_s, n_ctx_chunks)

    def fin(i, carry):
        sl = pl.ds(pl.multiple_of(i * c, c), c)
        for h in range(nh):
            cols = slice(h * HEAD_DV, (h + 1) * HEAD_DV)
            o = of_s[sl, cols] + ob_s[sl, cols]
            y_ref[sl, cols] = _rms_gate(o, gain_ref[...], g_ref[sl, cols].astype(F32)).astype(y_ref.dtype)
        return carry

    lax.fori_loop(0, n_chunks, fin, 0)


def _gla_mixer(p, p_gk, wg, bg, cos, sin, gain, ctx_len, col_q, col_k, col_v, col_g):
    b, tt, _ = p.shape
    nh = LANES // B_DK
    wide = nh * HEAD_DV
    pair = lambda c0: pl.BlockSpec((None, tt, LANES), lambda i, h: (i, 0, c0 + h))
    pair_v = lambda c0: pl.BlockSpec((None, tt, wide), lambda i, h: (i, 0, c0 * LANES // wide + h))
    seq = lambda w: pltpu.VMEM((tt, w), F32)
    state = lambda: pltpu.VMEM((nh, HEAD_DV, LANES), F32)
    assert (col_v * LANES) % wide == 0 and (col_g * LANES) % wide == 0
    return pl.pallas_call(
        functools.partial(_gla_kernel, ctx_len // SCAN_CHUNK),
        grid=(b, N_HEADS // nh),
        in_specs=[pair(col_q), pair(col_k), pair_v(col_v), pair_v(col_g),
                  pl.BlockSpec((None, tt, LANES), lambda i, h: (i, 0, 0)),
                  pl.BlockSpec((2, None, LANES, LANES), lambda i, h: (0, h, 0, 0)),
                  pl.BlockSpec((2, None, 1, LANES), lambda i, h: (0, h, 0, 0)),
                  pl.BlockSpec((tt, LANES), lambda i, h: (0, 0)),
                  pl.BlockSpec((tt, LANES), lambda i, h: (0, 0)),
                  pl.BlockSpec((1, LANES), lambda i, h: (0, 0))],
        out_specs=pl.BlockSpec((None, tt, wide), lambda i, h: (i, 0, h)),
        out_shape=jax.ShapeDtypeStruct((b, tt, MIX_W), BF16),
        scratch_shapes=[seq(LANES), seq(LANES), seq(LANES), seq(LANES), seq(wide), seq(wide), state(), state()],
        compiler_params=_params(("arbitrary", "arbitrary")),
        name="gla_mixer",
    )(p, p, p, p, p_gk, wg, bg, cos, sin, gain.reshape(1, HEAD_DV))


def _natten_kernel(ctx_len, rows, kr, q_ref, k_ref, v_ref, rpb_ref, y_ref, bias_ref):
    scale = HEAD_DV ** -0.5
    kc = k_ref[0:ctx_len, :]
    vc = v_ref[0:ctx_len, :]

    qi = lax.broadcasted_iota(jnp.int32, (GRID_W, LANES), 0)
    kj = lax.broadcasted_iota(jnp.int32, (GRID_W, LANES), 1) % GRID_W
    col_start = jnp.clip(qi - KC // 2, 0, GRID_W - KC)
    col_ok = (kj >= col_start) & (kj < col_start + KC)
    for v in range(kr):
        for blk in range(kr * GRID_W // LANES):
            d = v + blk * (LANES // GRID_W)
            src = jnp.broadcast_to(rpb_ref[d:d + 1, :], (GRID_W, LANES))
            rot = pltpu.roll(src, LANES - (KC - 1), 1, stride=1, stride_axis=0)
            bias_ref[v, :, blk * LANES:(blk + 1) * LANES] = jnp.where(col_ok, rot, MASK_VALUE)

    s = _dot_nt(q_ref[0:ctx_len, :], kc) * scale
    p = jnp.exp(s - jnp.max(s, axis=-1, keepdims=True))
    o = _dot(p.astype(BF16), vc) / jnp.sum(p, axis=-1, keepdims=True)
    y_ref[0:ctx_len, :] = o.astype(y_ref.dtype)

    def body(r, carry):
        start = jnp.clip(r - kr // 2, 0, rows - kr)
        variant = start - r + (KR_MAX - 1)
        q_sl = pl.ds(pl.multiple_of(ctx_len + r * GRID_W, GRID_W), GRID_W)
        b_sl = pl.ds(pl.multiple_of(ctx_len + start * GRID_W, GRID_W), kr * GRID_W)
        q = q_ref[q_sl, :]
        s_b = _dot_nt(q, k_ref[b_sl, :]) * scale + bias_ref[variant]
        s_c = _dot_nt(q, kc) * scale
        mx = jnp.maximum(jnp.max(s_b, axis=-1, keepdims=True), jnp.max(s_c, axis=-1, keepdims=True))
        p_b = jnp.exp(s_b - mx)
        p_c = jnp.exp(s_c - mx)
        den = jnp.sum(p_b, axis=-1, keepdims=True) + jnp.sum(p_c, axis=-1, keepdims=True)
        o = (_dot(p_b.astype(BF16), v_ref[b_sl, :]) + _dot(p_c.astype(BF16), vc)) / den
        y_ref[q_sl, :] = o.astype(y_ref.dtype)
        return carry

    lax.fori_loop(0, rows, body, 0, unroll=2)


def _natten_mixer(p, rpb, ctx_len, col_q, col_k, col_v):
    b, tt, _ = p.shape
    rows = (tt - ctx_len) // GRID_W
    kr = KR_MAX
    assert rows >= KR_MAX and rows % 2 == 0 and 2 * GRID_W == LANES and 2 * KC - 1 <= GRID_W
    padded = jnp.pad(rpb.astype(F32), ((0, 0), (0, 0), (0, GRID_W - (2 * KC - 1))))
    pairs = jnp.concatenate([padded[:, :-1], padded[:, 1:]], axis=-1)
    head = lambda c0: pl.BlockSpec((None, tt, LANES), lambda i, h: (i, 0, c0 + h))
    return pl.pallas_call(
        functools.partial(_natten_kernel, ctx_len, rows, kr),
        grid=(b, N_HEADS),
        in_specs=[head(col_q), head(col_k), head(col_v),
                  pl.BlockSpec((None, 2 * KR_MAX - 2, LANES), lambda i, h: (h, 0, 0))],
        out_specs=pl.BlockSpec((None, tt, LANES), lambda i, h: (i, 0, h)),
        out_shape=jax.ShapeDtypeStruct((b, tt, MIX_W), BF16),
        scratch_shapes=[pltpu.VMEM((kr, GRID_W, kr * GRID_W), F32)],
        compiler_params=_params(("arbitrary", "arbitrary")),
        name="natten_mixer",
    )(p, p, p, pairs)


def _merge_kernel(ya_ref, yb_ref, yc_ref, w_ref, ga_ref, gb_ref, gc_ref, o_ref):
    acc = None
    for n, (y_ref, g_ref) in enumerate(((ya_ref, ga_ref), (yb_ref, gb_ref), (yc_ref, gc_ref))):
        z = jax.nn.sigmoid(g_ref[...].astype(F32)) * _dot(y_ref[...], w_ref[n])
        acc = z if acc is None else acc + z
    o_ref[...] = acc.astype(o_ref.dtype)


def _merge(ya, yb, yc, w_branch, p, col_gate, d_model):
    m = ya.shape[0]
    tm = _pick_tile(m, (512, 256, 128))
    tn = _pick_tile(d_model, (1024, 512, 256, 128))
    nj = d_model // tn
    yspec = pl.BlockSpec((tm, MIX_W), lambda j, i: (i, 0))
    gspec = lambda n: pl.BlockSpec((tm, tn), lambda j, i, n=n: (i, col_gate * LANES // tn + n * nj + j))
    return pl.pallas_call(
        _merge_kernel,
        grid=(nj, m // tm),
        in_specs=[yspec, yspec, yspec,
                  pl.BlockSpec((N_BRANCH, MIX_W, tn), lambda j, i: (0, 0, j)),
                  gspec(0), gspec(1), gspec(2)],
        out_specs=pl.BlockSpec((tm, tn), lambda j, i: (i, j)),
        out_shape=jax.ShapeDtypeStruct((m, d_model), BF16),
        compiler_params=_params(("arbitrary", "arbitrary")),
        name="branch_merge",
    )(ya, yb, yc, w_branch, p, p, p)


def _proj_ln_kernel(alpha, ctx_len, tiles_per_batch, emit_h, a_ref, w_ref, x_ref, mod_ref, ln_ref, *rest):
    if emit_h:
        xo_ref, ho_ref, acc_ref = rest
    else:
        xo_ref, acc_ref = rest
    kk = pl.program_id(1)

    @pl.when(kk == 0)
    def _():
        acc_ref[...] = jnp.zeros_like(acc_ref)

    acc_ref[...] += _dot(a_ref[...], w_ref[...])

    @pl.when(kk == pl.num_programs(1) - 1)
    def _():
        tm = acc_ref.shape[0]
        rows = lax.broadcasted_iota(jnp.int32, acc_ref.shape, 0) + (pl.program_id(0) % tiles_per_batch) * tm
        is_ctx = rows < ctx_len
        sel = lambda j: jnp.where(is_ctx, mod_ref[1, j:j + 1, :], mod_ref[0, j:j + 1, :])
        y = alpha * x_ref[...] + sel(0) * acc_ref[...]
        mu = jnp.mean(y, axis=-1, keepdims=True)
        yc = y - mu
        var = jnp.mean(jnp.square(yc), axis=-1, keepdims=True)
        xn = yc * lax.rsqrt(var + LN_EPS) * ln_ref[0:1, :] + ln_ref[1:2, :]
        xo_ref[...] = xn
        if emit_h:
            ho_ref[...] = (xn * (1.0 + sel(1)) + sel(2)).astype(BF16)


def _proj_ln(a, w, x, mod_sel, ln_gb, alpha, ctx_len, tt, emit_h, name):
    m, k = a.shape
    d = w.shape[1]
    tm = _pick_tile(tt, (384, 256, 128))
    tk = _pick_tile(k, (1024, 512, 256, 128))
    tpb = tt // tm
    out_shape = [jax.ShapeDtypeStruct((m, d), F32)]
    out_specs = [pl.BlockSpec((tm, d), lambda i, j: (i, 0))]
    if emit_h:
        out_shape.append(jax.ShapeDtypeStruct((m, d), BF16))
        out_specs.append(pl.BlockSpec((tm, d), lambda i, j: (i, 0)))
    res = pl.pallas_call(
        functools.partial(_proj_ln_kernel, alpha, ctx_len, tpb, emit_h),
        grid=(m // tm, k // tk),
        in_specs=[pl.BlockSpec((tm, tk), lambda i, j: (i, j)),
                  pl.BlockSpec((tk, d), lambda i, j: (j, 0)),
                  pl.BlockSpec((tm, d), lambda i, j: (i, 0)),
                  pl.BlockSpec((None, 2, 3, d), lambda i, j: (i // tpb, 0, 0, 0)),
                  pl.BlockSpec((2, d), lambda i, j: (0, 0))],
        out_specs=out_specs,
        out_shape=out_shape,
        scratch_shapes=[pltpu.VMEM((tm, d), F32)],
        compiler_params=_params(("arbitrary", "arbitrary")),
        name=name,
    )(a, w, x, mod_sel, ln_gb)
    return (res[0], res[1]) if emit_h else (res[0], None)


def _rope_tables(seq, ctx_len):
    half = B_DK // 2
    freqs = ROPE_THETA ** (-jnp.arange(0, half, 2, dtype=F32) / half)
    pos = jnp.arange(seq)
    ang_r = (pos // GRID_W).astype(F32)[:, None] * freqs
    ang_c = (pos % GRID_W).astype(F32)[:, None] * freqs
    ang = jnp.concatenate([ang_r, ang_r, ang_c, ang_c], axis=-1)
    ang = jnp.concatenate([jnp.zeros((ctx_len, B_DK), F32), ang], axis=0)
    ang = jnp.concatenate([ang, ang], axis=-1)
    return jnp.cos(ang), jnp.sin(ang)


def kernel(x, c, ctx, c_ctx, w_ada, b_ada, w_in, hgrn_lb_logits, hgrn_norm_g, gla_w_gk2, gla_b_gk2, gla_norm_g,
           natten_rpb, w_branch, w_out, ln1_g, ln1_b, ln2_g, ln2_b, w_mlp1, w_mlp2):
    bsz, seq, d = x.shape
    ctx_len = ctx.shape[1]
    depth = w_ada.shape[0]
    tt = ctx_len + seq
    m_tok = bsz * tt
    alpha = (2 * depth) ** 0.25
    assert ctx_len % SCAN_CHUNK == 0 and seq % SCAN_CHUNK == 0 and seq % GRID_W == 0 and bsz + 1 <= 8

    seg = N_HEADS * A_DK
    gk_lo = 5 * seg + 2 * N_HEADS * B_DK + 2 * MIX_W
    gk_hi = gk_lo + 2 * GK_RANK
    col_a = 0
    col_bq = 5 * seg // LANES
    col_bk = col_bq + N_HEADS * B_DK // LANES
    col_bv = col_bk + N_HEADS * B_DK // LANES
    col_bg = col_bv + MIX_W // LANES
    col_cq = col_bg + MIX_W // LANES
    col_ck = col_cq + MIX_W // LANES
    col_cv = col_ck + MIX_W // LANES
    col_gate = col_cv + MIX_W // LANES

    c_rows = jnp.zeros((8, d), F32).at[:bsz].set(c).at[bsz].set(c_ctx)
    mod = _ada_all_layers(c_rows, w_ada, b_ada).reshape(depth, 8, 6, d)

    def mod_sel(l, idx):
        lat = mod[l, :bsz][:, idx, :]
        cx = jnp.broadcast_to(mod[l, bsz][idx, :], lat.shape)
        return jnp.stack([lat, cx], axis=1)

    lb_p = jax.nn.softmax(hgrn_lb_logits.astype(F32), axis=0)
    lb_cum = jnp.cumsum(lb_p, axis=0)
    lower_bounds = jnp.concatenate([jnp.zeros_like(lb_cum[:1]), lb_cum[:-1]], axis=0)

    cos, sin = _rope_tables(seq, ctx_len)
    rows = seq // GRID_W

    xs = jnp.concatenate([ctx, x], axis=1)
    h = _modulate(xs, mod_sel(0, np.array([1, 0])), ctx_len).reshape(m_tok, d)
    xs = xs.reshape(m_tok, d)

    for l in range(depth):
        w_l = w_in[l]
        w_main = jnp.concatenate([w_l[:, :gk_lo], w_l[:, gk_hi:]], axis=1).astype(BF16)
        w_gk = jnp.pad(w_l[:, gk_lo:gk_hi], ((0, 0), (0, LANES - 2 * GK_RANK))).astype(BF16)
        p = _matmul(h, w_main, name="in_proj").reshape(bsz, tt, -1)
        p_gk = _matmul(h, w_gk, name="in_proj_gk").reshape(bsz, tt, LANES)

        n_pairs = N_HEADS * B_DK // LANES
        wg = jnp.zeros((2, n_pairs, LANES, LANES), F32)
        w2 = gla_w_gk2[l].reshape(2, GK_RANK, n_pairs, LANES).transpose(0, 2, 1, 3)
        wg = wg.at[0, :, :GK_RANK].set(w2[0]).at[1, :, GK_RANK:2 * GK_RANK].set(w2[1]).astype(BF16)
        bg = gla_b_gk2[l].reshape(2, n_pairs, 1, LANES)

        y_a = _hgrn_mixer(p, lower_bounds[l], hgrn_norm_g[l], ctx_len, col_a)
        y_b = _gla_mixer(p, p_gk, wg, bg, cos, sin, gla_norm_g[l], ctx_len, col_bq, col_bk, col_bv, col_bg)
        y_c = _natten_mixer(p, natten_rpb[l], ctx_len, col_cq, col_ck, col_cv)

        merged = _merge(y_a.reshape(m_tok, MIX_W), y_b.reshape(m_tok, MIX_W), y_c.reshape(m_tok, MIX_W),
                        w_branch[l].astype(BF16), p.reshape(m_tok, -1), col_gate, d)
        xs, h2 = _proj_ln(merged, w_out[l].astype(BF16), xs, mod_sel(l, np.array([2, 4, 3])),
                          jnp.stack([ln1_g[l], ln1_b[l]]), alpha, ctx_len, tt, True, "out_proj_ln")
        u = _matmul(h2, w_mlp1[l].astype(BF16), sq_relu=True, name="mlp_up")
        last = l == depth - 1
        if last:
            nxt = jnp.zeros((bsz, 2, 3, d), F32).at[:, :, 0].set(mod_sel(l, np.array([5]))[:, :, 0])
        else:
            nxt = jnp.concatenate([mod_sel(l, np.array([5])), mod_sel(l + 1, np.array([1, 0]))], axis=2)
        xs, h = _proj_ln(u, w_mlp2[l].astype(BF16), xs, nxt, jnp.stack([ln2_g[l], ln2_b[l]]),
                         alpha, ctx_len, tt, not last, "mlp_down_ln")

    return xs.reshape(bsz, tt, d)[:, ctx_len:, :]
```

```python
import functools

import jax
import jax.numpy as jnp
import numpy as np
from jax import lax
from jax.experimental import pallas as pl
from jax.experimental.pallas import tpu as pltpu

GRID_W = 64
N_BRANCH = 3
MIX_W = 1024
N_HEADS = 8
A_DK = 128
B_DK = 64
HEAD_DV = MIX_W // N_HEADS
GK_RANK = 16
GATE_LOGIT_NORMALIZER = 16.0
KR_MAX = 8
KC = 16
ROPE_THETA = 10000.0
RMS_EPS = 1e-6
LN_EPS = 1e-5
MASK_VALUE = -1e30
LOG2E = 1.4426950408889634

LANES = 128
SCAN_CHUNK = 128
LN_ROW_BLOCK = 128
VMEM_LIMIT_BYTES = 56 * 1024 * 1024

F32 = jnp.float32
BF16 = jnp.bfloat16


def _pick_tile(n, prefs):
    for t in prefs:
        if n % t == 0:
            return t
    return n


def _params(sem):
    return pltpu.CompilerParams(dimension_semantics=sem, vmem_limit_bytes=VMEM_LIMIT_BYTES)


def _dot(a, b):
    return jnp.dot(a, b, preferred_element_type=F32)


def _dot_nt(a, b):
    return lax.dot_general(a, b, (((1,), (1,)), ((), ())), preferred_element_type=F32)


def _ada_kernel(c_ref, w_ref, b_ref, o_ref):
    c = c_ref[...]
    s = (c * jax.nn.sigmoid(c)).astype(BF16)
    o_ref[...] = _dot(s, w_ref[...].astype(BF16)) + b_ref[...]


def _ada_all_layers(c_rows, w_ada, b_ada):
    depth, d, n = w_ada.shape
    tn = _pick_tile(n, (1024, 512, 256, 128))
    return pl.pallas_call(
        _ada_kernel,
        grid=(depth, n // tn),
        in_specs=[pl.BlockSpec((8, d), lambda l, j: (0, 0)),
                  pl.BlockSpec((None, d, tn), lambda l, j: (l, 0, j)),
                  pl.BlockSpec((None, 1, tn), lambda l, j: (l, 0, j))],
        out_specs=pl.BlockSpec((None, 8, tn), lambda l, j: (l, 0, j)),
        out_shape=jax.ShapeDtypeStruct((depth, 8, n), F32),
        compiler_params=_params(("arbitrary", "arbitrary")),
        name="ada_mod",
    )(c_rows, w_ada, b_ada.reshape(depth, 1, n))


def _modulate_kernel(ctx_len, x_ref, m_ref, h_ref):
    x = x_ref[...]
    rows = lax.broadcasted_iota(jnp.int32, x.shape, 0) + pl.program_id(1) * x.shape[0]
    is_ctx = rows < ctx_len
    sc = jnp.where(is_ctx, m_ref[1, 0:1, :], m_ref[0, 0:1, :])
    sh = jnp.where(is_ctx, m_ref[1, 1:2, :], m_ref[0, 1:2, :])
    h_ref[...] = (x * (1.0 + sc) + sh).astype(BF16)


def _modulate(x, mod_sel, ctx_len):
    b, tt, d = x.shape
    tm = _pick_tile(tt, (768, 384, 256, 128))
    return pl.pallas_call(
        functools.partial(_modulate_kernel, ctx_len),
        grid=(b, tt // tm),
        in_specs=[pl.BlockSpec((None, tm, d), lambda i, j: (i, j, 0)),
                  pl.BlockSpec((None, 2, 2, d), lambda i, j: (i, 0, 0, 0))],
        out_specs=pl.BlockSpec((None, tm, d), lambda i, j: (i, j, 0)),
        out_shape=jax.ShapeDtypeStruct((b, tt, d), BF16),
        compiler_params=_params(("arbitrary", "arbitrary")),
        name="modulate",
    )(x, mod_sel)


def _mm_kernel(sq_relu, a_ref, w_ref, o_ref, wb_ref):
    @pl.when(pl.program_id(1) == 0)
    def _():
        wb_ref[...] = w_ref[...].astype(BF16)

    r = _dot(a_ref[...], wb_ref[...])
    if sq_relu:
        r = jnp.square(jnp.maximum(r, 0.0))
    o_ref[...] = r.astype(o_ref.dtype)


def _matmul(a, w, layer, n_cols, sq_relu=False, name="matmul"):
    m, k = a.shape
    tm = _pick_tile(m, (512, 256, 128))
    tn = _pick_tile(n_cols, (1024, 512, 256, 128))
    return pl.pallas_call(
        functools.partial(_mm_kernel, sq_relu),
        grid=(n_cols // tn, m // tm),
        in_specs=[pl.BlockSpec((tm, k), lambda j, i: (i, 0)),
                  pl.BlockSpec((None, k, tn), lambda j, i: (layer, 0, j))],
        out_specs=pl.BlockSpec((tm, tn), lambda j, i: (i, j)),
        out_shape=jax.ShapeDtypeStruct((m, n_cols), BF16),
        scratch_shapes=[pltpu.VMEM((k, tn), BF16)],
        compiler_params=_params(("arbitrary", "arbitrary")),
        name=name,
    )(a, w)


def _level_operand(q, k, la, bc, m, rev):
    c, dk = q.shape
    if m >= 8:
        pieces = []
        for jb in range(c // m):
            rows = slice(jb * m, (jb + 1) * m)
            r = (jb // 2) * 2 * m + (m if rev else m - 1)
            cvec = bc[r:r + 1, :]
            if (jb % 2 == 1) != rev:
                pieces.append(q[rows, :] * jnp.exp2((bc[rows, :] - cvec) * LOG2E))
            else:
                pieces.append(k[rows, :] * jnp.exp2((cvec - bc[rows, :]) * LOG2E))
        return jnp.concatenate(pieces, axis=0)
    shape3 = (c // 8, 8, dk)
    sub = lax.broadcasted_iota(jnp.int32, (1, 8, dk), 1)
    upper = (sub & m) != 0
    q_role = jnp.logical_not(upper) if rev else upper
    q3, k3, bc3 = q.reshape(shape3), k.reshape(shape3), bc.reshape(shape3)
    if m == 1:
        e2 = la.reshape(shape3) * jnp.where(q_role, LOG2E, 0.0)
    else:
        if m == 4:
            r = 4 if rev else 3
            c3 = bc3[:, r:r + 1, :]
        else:
            r0, r1 = (2, 6) if rev else (1, 5)
            c3 = jnp.where(sub < 4, bc3[:, r0:r0 + 1, :], bc3[:, r1:r1 + 1, :])
        e2 = (bc3 - c3) * jnp.where(q_role, LOG2E, -LOG2E)
    return (jnp.where(q_role, q3, k3) * jnp.exp2(e2)).reshape(c, dk)


def _level_matrix(c, rev):
    t = lax.broadcasted_iota(jnp.int32, (c, c), 0)
    s = lax.broadcasted_iota(jnp.int32, (c, c), 1)
    x = t ^ s
    lvl = jnp.full((c, c), -1, jnp.int32)
    b = 1
    while b < c:
        lvl = lvl + (x >= b).astype(jnp.int32)
        b *= 2
    return jnp.where((t < s) if rev else (t > s), lvl, -1)


def _gla_chunk(q, k, la, vs, sts, tri, lvl, head_lanes, rev):
    c, dk = q.shape
    nh = len(vs)
    la_hi = la.astype(BF16)
    la_lo = (la - la_hi.astype(F32)).astype(BF16)
    bb = _dot(tri, jnp.concatenate([la_hi, la_lo], axis=1))
    bc = bb[:, :dk] + bb[:, dk:]

    scores = [jnp.zeros((c, c), F32) for _ in range(nh)]
    m, level = c // 2, (c // 2).bit_length() - 1
    while m >= 1:
        x = _level_operand(q, k, la, bc, m, rev).astype(BF16)
        for h in range(nh):
            xh = x if nh == 1 else x * head_lanes[h][1]
            scores[h] = jnp.where(lvl == level, _dot_nt(xh, x), scores[h])
        m, level = m // 2, level - 1

    qk = q * k
    qg = (q * jnp.exp(bc)).astype(BF16)
    tot = bc[0:1, :] if rev else bc[c - 1:c, :]
    kg = (k * jnp.exp(tot - bc)).astype(BF16)
    dec = jnp.exp(tot)
    outs, new_sts = [], []
    for h in range(nh):
        v = vs[h]
        qk_h = qk if nh == 1 else jnp.where(head_lanes[h][0], qk, 0.0)
        qg_h = qg if nh == 1 else qg * head_lanes[h][1]
        o = _dot(scores[h].astype(BF16), v)
        o = o + jnp.sum(qk_h, axis=1, keepdims=True) * v.astype(F32)
        o = o + _dot_nt(qg_h, sts[h].astype(BF16))
        outs.append(o)
        new_sts.append(sts[h] * dec + _dot(v.T, kg))
    return outs, new_sts


def _scan_both_directions(q_s, kf_s, kb_s, laf_s, lab_s, v_ref, of_s, ob_s, stf_s, stb_s, n_ctx_chunks):
    c = SCAN_CHUNK
    n_chunks = q_s.shape[0] // c
    nh = stf_s.shape[0]
    dv = v_ref.shape[1] // nh
    ti = lax.broadcasted_iota(jnp.int32, (c, c), 0)
    si = lax.broadcasted_iota(jnp.int32, (c, c), 1)
    tri_f = (si <= ti).astype(BF16)
    tri_b = (si >= ti).astype(BF16)
    lvl_f = _level_matrix(c, False)
    lvl_b = _level_matrix(c, True)
    lane = lax.broadcasted_iota(jnp.int32, (c, LANES), 1) // (LANES // nh)
    head_lanes = [(lane == h, (lane == h).astype(BF16)) for h in range(nh)]
    stf_s[...] = jnp.zeros_like(stf_s)
    stb_s[...] = jnp.zeros_like(stb_s)

    def one(sl, k_s, la_s, st_s, o_s, tri, lvl, rev):
        vs = [v_ref[sl, h * dv:(h + 1) * dv] for h in range(nh)]
        outs, sts = _gla_chunk(q_s[sl, :], k_s[sl, :], la_s[sl, :], vs, [st_s[h] for h in range(nh)],
                               tri, lvl, head_lanes, rev)
        for h in range(nh):
            st_s[h] = sts[h]
            o_s[sl, h * dv:(h + 1) * dv] = outs[h]

    def body(i, carry):
        one(pl.ds(pl.multiple_of(i * c, c), c), kf_s, laf_s, stf_s, of_s, tri_f, lvl_f, False)
        cb = jnp.where(i < n_ctx_chunks, n_ctx_chunks - 1 - i, n_chunks - 1 + n_ctx_chunks - i)
        one(pl.ds(pl.multiple_of(cb * c, c), c), kb_s, lab_s, stb_s, ob_s, tri_b, lvl_b, True)
        return carry

    lax.fori_loop(0, n_chunks, body, 0)


def _rms_gate(o, gain, gate):
    o = o * lax.rsqrt(jnp.mean(jnp.square(o), axis=-1, keepdims=True) + RMS_EPS) * gain
    return o * (gate * jax.nn.sigmoid(gate))


def _hgrn_kernel(n_ctx_chunks, q_ref, i_ref, g_ref, ff_ref, fb_ref, lb_ref, gain_ref, y_ref,
                 q_s, kf_s, kb_s, laf_s, lab_s, of_s, ob_s, stf_s, stb_s):
    c = SCAN_CHUNK
    n_chunks = q_s.shape[0] // c
    lb_f = lb_ref[0:1, :]
    lb_b = lb_ref[1:2, :]

    def prep(i, carry):
        sl = pl.ds(pl.multiple_of(i * c, c), c)
        q = q_ref[sl, :].astype(F32)
        q_s[sl, :] = q * jax.nn.sigmoid(q) * (A_DK ** -0.5)
        for f_ref, lb, k_s, la_s in ((ff_ref, lb_f, kf_s, laf_s), (fb_ref, lb_b, kb_s, lab_s)):
            f = f_ref[sl, :].astype(F32)
            k_s[sl, :] = (1.0 - lb) * jax.nn.sigmoid(-f)
            la_s[sl, :] = jnp.log(lb + (1.0 - lb) * jax.nn.sigmoid(f))
        return carry

    lax.fori_loop(0, n_chunks, prep, 0)
    _scan_both_directions(q_s, kf_s, kb_s, laf_s, lab_s, i_ref, of_s, ob_s, stf_s, stb_s, n_ctx_chunks)

    def fin(i, carry):
        sl = pl.ds(pl.multiple_of(i * c, c), c)
        o = of_s[sl, :] + ob_s[sl, :]
        y_ref[sl, :] = _rms_gate(o, gain_ref[...], g_ref[sl, :].astype(F32)).astype(y_ref.dtype)
        return carry

    lax.fori_loop(0, n_chunks, fin, 0)


def _hgrn_mixer(p, lb, gain, ctx_len, col0):
    b, tt, _ = p.shape
    blk = lambda seg: pl.BlockSpec((None, tt, LANES), lambda i, h, s=seg: (i, 0, col0 + s * N_HEADS + h))
    seq = lambda: pltpu.VMEM((tt, LANES), F32)
    return pl.pallas_call(
        functools.partial(_hgrn_kernel, ctx_len // SCAN_CHUNK),
        grid=(b, N_HEADS),
        in_specs=[blk(0), blk(1), blk(2), blk(3), blk(4),
                  pl.BlockSpec((2, LANES), lambda i, h: (0, h)),
                  pl.BlockSpec((1, LANES), lambda i, h: (0, 0))],
        out_specs=pl.BlockSpec((None, tt, LANES), lambda i, h: (i, 0, h)),
        out_shape=jax.ShapeDtypeStruct((b, tt, MIX_W), BF16),
        scratch_shapes=[seq(), seq(), seq(), seq(), seq(), seq(), seq(),
                        pltpu.VMEM((1, HEAD_DV, LANES), F32), pltpu.VMEM((1, HEAD_DV, LANES), F32)],
        compiler_params=_params(("arbitrary", "arbitrary")),
        name="hgrn2_mixer",
    )(p, p, p, p, p, lb, gain.reshape(1, HEAD_DV))


def _rope_rotate(x):
    n = x.shape[-1]
    lane = lax.broadcasted_iota(jnp.int32, x.shape, x.ndim - 1)
    first = (lane % 32) < 16
    return jnp.where(first, -pltpu.roll(x, n - 16, x.ndim - 1), pltpu.roll(x, 16, x.ndim - 1))


def _gla_kernel(n_ctx_chunks, q_ref, k_ref, v_ref, g_ref, gk_ref, wg_ref, bg_ref, cos_ref, sin_ref,
                gain_ref, y_ref, q_s, k_s, laf_s, lab_s, of_s, ob_s, stf_s, stb_s):
    c = SCAN_CHUNK
    n_chunks = q_s.shape[0] // c
    nh = LANES // B_DK

    def prep(i, carry):
        sl = pl.ds(pl.multiple_of(i * c, c), c)
        cos = cos_ref[sl, :]
        sin = sin_ref[sl, :]
        q = q_ref[sl, :].astype(F32)
        k = k_ref[sl, :].astype(F32)
        q_s[sl, :] = (q * cos + _rope_rotate(q) * sin) * (B_DK ** -0.5)
        k_s[sl, :] = k * cos + _rope_rotate(k) * sin
        gk = gk_ref[sl, :]
        for d, la_s in ((0, laf_s), (1, lab_s)):
            z = _dot(gk, wg_ref[d]) + bg_ref[d]
            la_s[sl, :] = jax.nn.log_sigmoid(z) / GATE_LOGIT_NORMALIZER
        return carry

    lax.fori_loop(0, n_chunks, prep, 0)
    _scan_both_directions(q_s, k_s, k_s, laf_s, lab_s, v_ref, of_s, ob_s, stf_s, stb_s, n_ctx_chunks)

    def fin(i, carry):
        sl = pl.ds(pl.multiple_of(i * c, c), c)
        for h in range(nh):
            cols = slice(h * HEAD_DV, (h + 1) * HEAD_DV)
            o = of_s[sl, cols] + ob_s[sl, cols]
            y_ref[sl, cols] = _rms_gate(o, gain_ref[...], g_ref[sl, cols].astype(F32)).astype(y_ref.dtype)
        return carry

    lax.fori_loop(0, n_chunks, fin, 0)


def _gla_mixer(p, p_gk, wg, bg, cos, sin, gain, ctx_len, col_q, col_k, col_v, col_g):
    b, tt, _ = p.shape
    nh = LANES // B_DK
    wide = nh * HEAD_DV
    pair = lambda c0: pl.BlockSpec((None, tt, LANES), lambda i, h: (i, 0, c0 + h))
    pair_v = lambda c0: pl.BlockSpec((None, tt, wide), lambda i, h: (i, 0, c0 * LANES // wide + h))
    seq = lambda w: pltpu.VMEM((tt, w), F32)
    state = lambda: pltpu.VMEM((nh, HEAD_DV, LANES), F32)
    assert (col_v * LANES) % wide == 0 and (col_g * LANES) % wide == 0
    return pl.pallas_call(
        functools.partial(_gla_kernel, ctx_len // SCAN_CHUNK),
        grid=(b, N_HEADS // nh),
        in_specs=[pair(col_q), pair(col_k), pair_v(col_v), pair_v(col_g),
                  pl.BlockSpec((None, tt, LANES), lambda i, h: (i, 0, 0)),
                  pl.BlockSpec((2, None, LANES, LANES), lambda i, h: (0, h, 0, 0)),
                  pl.BlockSpec((2, None, 1, LANES), lambda i, h: (0, h, 0, 0)),
                  pl.BlockSpec((tt, LANES), lambda i, h: (0, 0)),
                  pl.BlockSpec((tt, LANES), lambda i, h: (0, 0)),
                  pl.BlockSpec((1, LANES), lambda i, h: (0, 0))],
        out_specs=pl.BlockSpec((None, tt, wide), lambda i, h: (i, 0, h)),
        out_shape=jax.ShapeDtypeStruct((b, tt, MIX_W), BF16),
        scratch_shapes=[seq(LANES), seq(LANES), seq(LANES), seq(LANES), seq(wide), seq(wide), state(), state()],
        compiler_params=_params(("arbitrary", "arbitrary")),
        name="gla_mixer",
    )(p, p, p, p, p_gk, wg, bg, cos, sin, gain.reshape(1, HEAD_DV))


def _natten_kernel(ctx_len, rows, kr, q_ref, k_ref, v_ref, rpb_ref, y_ref, bias_ref):
    scale = HEAD_DV ** -0.5
    kc = k_ref[0:ctx_len, :]
    vc = v_ref[0:ctx_len, :]

    qi = lax.broadcasted_iota(jnp.int32, (GRID_W, LANES), 0)
    kj = lax.broadcasted_iota(jnp.int32, (GRID_W, LANES), 1) % GRID_W
    col_start = jnp.clip(qi - KC // 2, 0, GRID_W - KC)
    col_ok = (kj >= col_start) & (kj < col_start + KC)
    for v in range(kr):
        for blk in range(kr * GRID_W // LANES):
            d = v + blk * (LANES // GRID_W)
            src = jnp.broadcast_to(rpb_ref[d:d + 1, :], (GRID_W, LANES))
            rot = pltpu.roll(src, LANES - (KC - 1), 1, stride=1, stride_axis=0)
            bias_ref[v, :, blk * LANES:(blk + 1) * LANES] = jnp.where(col_ok, rot, MASK_VALUE)

    s = _dot_nt(q_ref[0:ctx_len, :], kc) * scale
    p = jnp.exp(s - jnp.max(s, axis=-1, keepdims=True))
    o = _dot(p.astype(BF16), vc) / jnp.sum(p, axis=-1, keepdims=True)
    y_ref[0:ctx_len, :] = o.astype(y_ref.dtype)

    def body(r, carry):
        start = jnp.clip(r - kr // 2, 0, rows - kr)
        variant = start - r + (KR_MAX - 1)
        q_sl = pl.ds(pl.multiple_of(ctx_len + r * GRID_W, GRID_W), GRID_W)
        b_sl = pl.ds(pl.multiple_of(ctx_len + start * GRID_W, GRID_W), kr * GRID_W)
        q = q_ref[q_sl, :]
        s_b = _dot_nt(q, k_ref[b_sl, :]) * scale + bias_ref[variant]
        s_c = _dot_nt(q, kc) * scale
        mx = jnp.maximum(jnp.max(s_b, axis=-1, keepdims=True), jnp.max(s_c, axis=-1, keepdims=True))
        p_b = jnp.exp(s_b - mx)
        p_c = jnp.exp(s_c - mx)
        den = jnp.sum(p_b, axis=-1, keepdims=True) + jnp.sum(p_c, axis=-1, keepdims=True)
        o = (_dot(p_b.astype(BF16), v_ref[b_sl, :]) + _dot(p_c.astype(BF16), vc)) / den
        y_ref[q_sl, :] = o.astype(y_ref.dtype)
        return carry

    lax.fori_loop(0, rows, body, 0, unroll=2)


def _natten_mixer(p, rpb, ctx_len, col_q, col_k, col_v):
    b, tt, _ = p.shape
    rows = (tt - ctx_len) // GRID_W
    kr = KR_MAX
    assert rows >= KR_MAX and rows % 2 == 0 and 2 * GRID_W == LANES and 2 * KC - 1 <= GRID_W
    padded = jnp.pad(rpb.astype(F32), ((0, 0), (0, 0), (0, GRID_W - (2 * KC - 1))))
    pairs = jnp.concatenate([padded[:, :-1], padded[:, 1:]], axis=-1)
    head = lambda c0: pl.BlockSpec((None, tt, LANES), lambda i, h: (i, 0, c0 + h))
    return pl.pallas_call(
        functools.partial(_natten_kernel, ctx_len, rows, kr),
        grid=(b, N_HEADS),
        in_specs=[head(col_q), head(col_k), head(col_v),
                  pl.BlockSpec((None, 2 * KR_MAX - 2, LANES), lambda i, h: (h, 0, 0))],
        out_specs=pl.BlockSpec((None, tt, LANES), lambda i, h: (i, 0, h)),
        out_shape=jax.ShapeDtypeStruct((b, tt, MIX_W), BF16),
        scratch_shapes=[pltpu.VMEM((kr, GRID_W, kr * GRID_W), F32)],
        compiler_params=_params(("arbitrary", "arbitrary")),
        name="natten_mixer",
    )(p, p, p, pairs)


def _merge_kernel(ya_ref, yb_ref, yc_ref, w_ref, ga_ref, gb_ref, gc_ref, o_ref, wb_ref):
    @pl.when(pl.program_id(1) == 0)
    def _():
        wb_ref[...] = w_ref[...].astype(BF16)

    acc = None
    for n, (y_ref, g_ref) in enumerate(((ya_ref, ga_ref), (yb_ref, gb_ref), (yc_ref, gc_ref))):
        z = jax.nn.sigmoid(g_ref[...].astype(F32)) * _dot(y_ref[...], wb_ref[n])
        acc = z if acc is None else acc + z
    o_ref[...] = acc.astype(o_ref.dtype)


def _merge(ya, yb, yc, w_branch, layer, p, col_gate, d_model):
    m = ya.shape[0]
    tm = _pick_tile(m, (512, 256, 128))
    tn = _pick_tile(d_model, (512, 256, 128))
    nj = d_model // tn
    assert (col_gate * LANES) % tn == 0
    yspec = pl.BlockSpec((tm, MIX_W), lambda j, i: (i, 0))
    gspec = lambda n: pl.BlockSpec((tm, tn), lambda j, i, n=n: (i, col_gate * LANES // tn + n * nj + j))
    return pl.pallas_call(
        _merge_kernel,
        grid=(nj, m // tm),
        in_specs=[yspec, yspec, yspec,
                  pl.BlockSpec((None, N_BRANCH, MIX_W, tn), lambda j, i: (layer, 0, 0, j)),
                  gspec(0), gspec(1), gspec(2)],
        out_specs=pl.BlockSpec((tm, tn), lambda j, i: (i, j)),
        out_shape=jax.ShapeDtypeStruct((m, d_model), BF16),
        scratch_shapes=[pltpu.VMEM((N_BRANCH, MIX_W, tn), BF16)],
        compiler_params=_params(("arbitrary", "arbitrary")),
        name="branch_merge",
    )(ya, yb, yc, w_branch, p, p, p)


def _proj_ln_kernel(alpha, ctx_len, tiles_per_batch, emit_h, a_ref, w_ref, x_ref, mod_ref, ln_ref, *rest):
    if emit_h:
        xo_ref, ho_ref, acc_ref = rest
    else:
        xo_ref, acc_ref = rest
    kk = pl.program_id(1)

    @pl.when(kk == 0)
    def _():
        acc_ref[...] = jnp.zeros_like(acc_ref)

    acc_ref[...] += _dot(a_ref[...], w_ref[...].astype(BF16))

    @pl.when(kk == pl.num_programs(1) - 1)
    def _():
        tm, d = acc_ref.shape
        rb = LN_ROW_BLOCK
        row0 = (pl.program_id(0) % tiles_per_batch) * tm

        def block(i, carry):
            sl = pl.ds(pl.multiple_of(i * rb, rb), rb)
            rows = lax.broadcasted_iota(jnp.int32, (rb, d), 0) + (row0 + i * rb)
            is_ctx = rows < ctx_len
            sel = lambda j: jnp.where(is_ctx, mod_ref[1, j:j + 1, :], mod_ref[0, j:j + 1, :])
            y = alpha * x_ref[sl, :] + sel(0) * acc_ref[sl, :]
            mu = jnp.mean(y, axis=-1, keepdims=True)
            yc = y - mu
            var = jnp.mean(jnp.square(yc), axis=-1, keepdims=True)
            xn = yc * lax.rsqrt(var + LN_EPS) * ln_ref[0:1, :] + ln_ref[1:2, :]
            xo_ref[sl, :] = xn
            if emit_h:
                ho_ref[sl, :] = (xn * (1.0 + sel(1)) + sel(2)).astype(BF16)
            return carry

        lax.fori_loop(0, tm // rb, block, 0)


def _proj_ln(a, w, layer, x, mod_sel, ln_gb, alpha, ctx_len, tt, emit_h, name):
    m, k = a.shape
    d = w.shape[2]
    if w.dtype == BF16:
        tm, tk = _pick_tile(tt, (384, 256, 128)), k
    else:
        tm, tk = _pick_tile(tt, (768, 384, 256, 128)), _pick_tile(k, (512, 256, 128))
    assert tm % LN_ROW_BLOCK == 0 and ctx_len <= tm
    tpb = tt // tm
    out_shape = [jax.ShapeDtypeStruct((m, d), F32)]
    out_specs = [pl.BlockSpec((tm, d), lambda i, j: (i, 0))]
    if emit_h:
        out_shape.append(jax.ShapeDtypeStruct((m, d), BF16))
        out_specs.append(pl.BlockSpec((tm, d), lambda i, j: (i, 0)))
    res = pl.pallas_call(
        functools.partial(_proj_ln_kernel, alpha, ctx_len, tpb, emit_h),
        grid=(m // tm, k // tk),
        in_specs=[pl.BlockSpec((tm, tk), lambda i, j: (i, j)),
                  pl.BlockSpec((None, tk, d), lambda i, j: (layer, j, 0)),
                  pl.BlockSpec((tm, d), lambda i, j: (i, 0)),
                  pl.BlockSpec((None, 2, 3, d), lambda i, j: (i // tpb, 0, 0, 0)),
                  pl.BlockSpec((2, d), lambda i, j: (0, 0))],
        out_specs=out_specs,
        out_shape=out_shape,
        scratch_shapes=[pltpu.VMEM((tm, d), F32)],
        compiler_params=_params(("arbitrary", "arbitrary")),
        name=name,
    )(a, w, x, mod_sel, ln_gb)
    return (res[0], res[1]) if emit_h else (res[0], None)


def _rope_tables(seq, ctx_len):
    half = B_DK // 2
    freqs = ROPE_THETA ** (-jnp.arange(0, half, 2, dtype=F32) / half)
    pos = jnp.arange(seq)
    ang_r = (pos // GRID_W).astype(F32)[:, None] * freqs
    ang_c = (pos % GRID_W).astype(F32)[:, None] * freqs
    ang = jnp.concatenate([ang_r, ang_r, ang_c, ang_c], axis=-1)
    ang = jnp.concatenate([jnp.zeros((ctx_len, B_DK), F32), ang], axis=0)
    ang = jnp.concatenate([ang, ang], axis=-1)
    return jnp.cos(ang), jnp.sin(ang)


def kernel(x, c, ctx, c_ctx, w_ada, b_ada, w_in, hgrn_lb_logits, hgrn_norm_g, gla_w_gk2, gla_b_gk2, gla_norm_g,
           natten_rpb, w_branch, w_out, ln1_g, ln1_b, ln2_g, ln2_b, w_mlp1, w_mlp2):
    bsz, seq, d = x.shape
    ctx_len = ctx.shape[1]
    depth = w_ada.shape[0]
    tt = ctx_len + seq
    m_tok = bsz * tt
    alpha = (2 * depth) ** 0.25
    assert ctx_len % SCAN_CHUNK == 0 and seq % SCAN_CHUNK == 0 and seq % GRID_W == 0 and bsz + 1 <= 8

    seg = N_HEADS * A_DK
    gk_lo = 5 * seg + 2 * N_HEADS * B_DK + 2 * MIX_W
    gk_hi = gk_lo + 2 * GK_RANK
    n_cg = 3 * MIX_W + N_BRANCH * d
    assert gk_lo % 1024 == 0 and w_in.shape[2] == gk_hi + n_cg
    col_a = 0
    col_bq = 5 * seg // LANES
    col_bk = col_bq + N_HEADS * B_DK // LANES
    col_bv = col_bk + N_HEADS * B_DK // LANES
    col_bg = col_bv + MIX_W // LANES
    col_cq = 0
    col_ck = col_cq + MIX_W // LANES
    col_cv = col_ck + MIX_W // LANES
    col_gate = col_cv + MIX_W // LANES

    c_rows = jnp.zeros((8, d), F32).at[:bsz].set(c).at[bsz].set(c_ctx)
    mod = _ada_all_layers(c_rows, w_ada, b_ada).reshape(depth, 8, 6, d)

    def mod_sel(l, idx):
        lat = mod[l, :bsz][:, idx, :]
        cx = jnp.broadcast_to(mod[l, bsz][idx, :], lat.shape)
        return jnp.stack([lat, cx], axis=1)

    lb_p = jax.nn.softmax(hgrn_lb_logits.astype(F32), axis=0)
    lb_cum = jnp.cumsum(lb_p, axis=0)
    lower_bounds = jnp.concatenate([jnp.zeros_like(lb_cum[:1]), lb_cum[:-1]], axis=0)

    cos, sin = _rope_tables(seq, ctx_len)
    w_out_bf = w_out.astype(BF16)

    xs = jnp.concatenate([ctx, x], axis=1)
    h = _modulate(xs, mod_sel(0, np.array([1, 0])), ctx_len).reshape(m_tok, d)
    xs = xs.reshape(m_tok, d)

    for l in range(depth):
        w_cg = w_in[l, :, gk_hi:][None]
        w_gk = jnp.pad(w_in[l, :, gk_lo:gk_hi], ((0, 0), (0, LANES - 2 * GK_RANK)))[None]
        p = _matmul(h, w_in, l, gk_lo, name="in_proj_ab").reshape(bsz, tt, gk_lo)
        p_cg = _matmul(h, w_cg, 0, n_cg, name="in_proj_cg").reshape(bsz, tt, n_cg)
        p_gk = _matmul(h, w_gk, 0, LANES, name="in_proj_gk").reshape(bsz, tt, LANES)

        n_pairs = N_HEADS * B_DK // LANES
        wg = jnp.zeros((2, n_pairs, LANES, LANES), F32)
        w2 = gla_w_gk2[l].reshape(2, GK_RANK, n_pairs, LANES).transpose(0, 2, 1, 3)
        wg = wg.at[0, :, :GK_RANK].set(w2[0]).at[1, :, GK_RANK:2 * GK_RANK].set(w2[1]).astype(BF16)
        bg = gla_b_gk2[l].reshape(2, n_pairs, 1, LANES)

        y_a = _hgrn_mixer(p, lower_bounds[l], hgrn_norm_g[l], ctx_len, col_a)
        y_b = _gla_mixer(p, p_gk, wg, bg, cos, sin, gla_norm_g[l], ctx_len, col_bq, col_bk, col_bv, col_bg)
        y_c = _natten_mixer(p_cg, natten_rpb[l], ctx_len, col_cq, col_ck, col_cv)

        merged = _merge(y_a.reshape(m_tok, MIX_W), y_b.reshape(m_tok, MIX_W), y_c.reshape(m_tok, MIX_W),
                        w_branch, l, p_cg.reshape(m_tok, n_cg), col_gate, d)
        xs, h2 = _proj_ln(merged, w_out_bf, l, xs, mod_sel(l, np.array([2, 4, 3])),
                          jnp.stack([ln1_g[l], ln1_b[l]]), alpha, ctx_len, tt, True, "out_proj_ln")
        u = _matmul(h2, w_mlp1, l, w_mlp1.shape[2], sq_relu=True, name="mlp_up")
        last = l == depth - 1
        if last:
            nxt = jnp.zeros((bsz, 2, 3, d), F32).at[:, :, 0].set(mod_sel(l, np.array([5]))[:, :, 0])
        else:
            nxt = jnp.concatenate([mod_sel(l, np.array([5])), mod_sel(l + 1, np.array([1, 0]))], axis=2)
        xs, h = _proj_ln(u, w_mlp2, l, xs, nxt, jnp.stack([ln2_g[l], ln2_b[l]]),
                         alpha, ctx_len, tt, not last, "mlp_down_ln")

    return xs.reshape(bsz, tt, d)[:, ctx_len:, :]
```

```python
import functools

import jax
import jax.numpy as jnp
import numpy as np
from jax import lax
from jax.experimental import pallas as pl
from jax.experimental.pallas import tpu as pltpu

GRID_W = 64
N_BRANCH = 3
MIX_W = 1024
N_HEADS = 8
A_DK = 128
B_DK = 64
HEAD_DV = MIX_W // N_HEADS
GK_RANK = 16
GATE_LOGIT_NORMALIZER = 16.0
KR_MAX = 8
KC = 16
ROPE_THETA = 10000.0
RMS_EPS = 1e-6
LN_EPS = 1e-5
MASK_VALUE = -1e30
LOG2E = 1.4426950408889634

LANES = 128
SCAN_CHUNK = 128
SCAN_UNROLL = 2
NATTEN_ROWS_PER_TRIP = 4
LN_ROW_BLOCK = 128
VMEM_LIMIT_BYTES = 56 * 1024 * 1024

F32 = jnp.float32
BF16 = jnp.bfloat16


def _pick_tile(n, prefs):
    for t in prefs:
        if n % t == 0:
            return t
    return n


def _params(sem):
    return pltpu.CompilerParams(dimension_semantics=sem, vmem_limit_bytes=VMEM_LIMIT_BYTES)


def _dot(a, b):
    return jnp.dot(a, b, preferred_element_type=F32)


def _dot_nt(a, b):
    return lax.dot_general(a, b, (((1,), (1,)), ((), ())), preferred_element_type=F32)


def _ada_kernel(c_ref, w_ref, b_ref, o_ref):
    c = c_ref[...]
    s = (c * jax.nn.sigmoid(c)).astype(BF16)
    o_ref[...] = _dot(s, w_ref[...].astype(BF16)) + b_ref[...]


def _ada_all_layers(c_rows, w_ada, b_ada):
    depth, d, n = w_ada.shape
    tn = _pick_tile(n, (1024, 512, 256, 128))
    return pl.pallas_call(
        _ada_kernel,
        grid=(depth, n // tn),
        in_specs=[pl.BlockSpec((8, d), lambda l, j: (0, 0)),
                  pl.BlockSpec((None, d, tn), lambda l, j: (l, 0, j)),
                  pl.BlockSpec((None, 1, tn), lambda l, j: (l, 0, j))],
        out_specs=pl.BlockSpec((None, 8, tn), lambda l, j: (l, 0, j)),
        out_shape=jax.ShapeDtypeStruct((depth, 8, n), F32),
        compiler_params=_params(("arbitrary", "arbitrary")),
        name="ada_mod",
    )(c_rows, w_ada, b_ada.reshape(depth, 1, n))


def _modulate_kernel(ctx_len, x_ref, m_ref, h_ref):
    x = x_ref[...]
    rows = lax.broadcasted_iota(jnp.int32, x.shape, 0) + pl.program_id(1) * x.shape[0]
    is_ctx = rows < ctx_len
    sc = jnp.where(is_ctx, m_ref[1, 0:1, :], m_ref[0, 0:1, :])
    sh = jnp.where(is_ctx, m_ref[1, 1:2, :], m_ref[0, 1:2, :])
    h_ref[...] = (x * (1.0 + sc) + sh).astype(BF16)


def _modulate(x, mod_sel, ctx_len):
    b, tt, d = x.shape
    tm = _pick_tile(tt, (768, 384, 256, 128))
    return pl.pallas_call(
        functools.partial(_modulate_kernel, ctx_len),
        grid=(b, tt // tm),
        in_specs=[pl.BlockSpec((None, tm, d), lambda i, j: (i, j, 0)),
                  pl.BlockSpec((None, 2, 2, d), lambda i, j: (i, 0, 0, 0))],
        out_specs=pl.BlockSpec((None, tm, d), lambda i, j: (i, j, 0)),
        out_shape=jax.ShapeDtypeStruct((b, tt, d), BF16),
        compiler_params=_params(("arbitrary", "arbitrary")),
        name="modulate",
    )(x, mod_sel)


def _mm_kernel(sq_relu, a_ref, w_ref, o_ref, *scratch):
    if scratch:
        wb_ref, = scratch

        @pl.when(pl.program_id(1) == 0)
        def _():
            wb_ref[...] = w_ref[...].astype(BF16)
    else:
        wb_ref = w_ref

    r = _dot(a_ref[...], wb_ref[...])
    if sq_relu:
        r = jnp.square(jnp.maximum(r, 0.0))
    o_ref[...] = r.astype(o_ref.dtype)


def _matmul(a, w, layer, sq_relu=False, name="matmul"):
    m, k = a.shape
    n = w.shape[2]
    tm = _pick_tile(m, (512, 256, 128))
    tn = _pick_tile(n, (1024, 512, 256, 128))
    return pl.pallas_call(
        functools.partial(_mm_kernel, sq_relu),
        grid=(n // tn, m // tm),
        in_specs=[pl.BlockSpec((tm, k), lambda j, i: (i, 0)),
                  pl.BlockSpec((None, k, tn), lambda j, i: (layer, 0, j))],
        out_specs=pl.BlockSpec((tm, tn), lambda j, i: (i, j)),
        out_shape=jax.ShapeDtypeStruct((m, n), BF16),
        scratch_shapes=[] if w.dtype == BF16 else [pltpu.VMEM((k, tn), BF16)],
        compiler_params=_params(("arbitrary", "arbitrary")),
        name=name,
    )(a, w)


def _level_operand(q, k, la, bc, m, rev):
    c, dk = q.shape
    if m >= 8:
        pieces = []
        for jb in range(c // m):
            rows = slice(jb * m, (jb + 1) * m)
            r = (jb // 2) * 2 * m + (m if rev else m - 1)
            cvec = bc[r:r + 1, :]
            if (jb % 2 == 1) != rev:
                pieces.append(q[rows, :] * jnp.exp2((bc[rows, :] - cvec) * LOG2E))
            else:
                pieces.append(k[rows, :] * jnp.exp2((cvec - bc[rows, :]) * LOG2E))
        return jnp.concatenate(pieces, axis=0)
    shape3 = (c // 8, 8, dk)
    sub = lax.broadcasted_iota(jnp.int32, (1, 8, dk), 1)
    upper = (sub & m) != 0
    q_role = jnp.logical_not(upper) if rev else upper
    q3, k3, bc3 = q.reshape(shape3), k.reshape(shape3), bc.reshape(shape3)
    if m == 1:
        e2 = la.reshape(shape3) * jnp.where(q_role, LOG2E, 0.0)
    else:
        if m == 4:
            r = 4 if rev else 3
            c3 = bc3[:, r:r + 1, :]
        else:
            r0, r1 = (2, 6) if rev else (1, 5)
            c3 = jnp.where(sub < 4, bc3[:, r0:r0 + 1, :], bc3[:, r1:r1 + 1, :])
        e2 = (bc3 - c3) * jnp.where(q_role, LOG2E, -LOG2E)
    return (jnp.where(q_role, q3, k3) * jnp.exp2(e2)).reshape(c, dk)


def _level_matrix(c, rev):
    t = lax.broadcasted_iota(jnp.int32, (c, c), 0)
    s = lax.broadcasted_iota(jnp.int32, (c, c), 1)
    x = t ^ s
    lvl = jnp.full((c, c), -1, jnp.int32)
    b = 1
    while b < c:
        lvl = lvl + (x >= b).astype(jnp.int32)
        b *= 2
    return jnp.where((t < s) if rev else (t > s), lvl, -1)


def _scan_both_directions(prep_chunk, q_s, kf_s, kb_s, laf_s, lab_s, v_ref, o_s, st_s, bc_s, vt_s, n_ctx_chunks):
    c = SCAN_CHUNK
    n_chunks = q_s.shape[0] // c
    nh = st_s.shape[1]
    dv = v_ref.shape[1] // nh
    ti = lax.broadcasted_iota(jnp.int32, (c, c), 0)
    si = lax.broadcasted_iota(jnp.int32, (c, c), 1)
    lane = lax.broadcasted_iota(jnp.int32, (c, LANES), 1) // (LANES // nh)
    head_lanes = [(lane == h, (lane == h).astype(BF16)) for h in range(nh)]
    k_refs, la_refs = (kf_s, kb_s), (laf_s, lab_s)
    tris = ((si <= ti).astype(BF16), (si >= ti).astype(BF16))
    lvls = (_level_matrix(c, False), _level_matrix(c, True))

    def cumulate(i, carry):
        sl = pl.ds(pl.multiple_of(i * c, c), c)
        las = prep_chunk(sl)
        for h in range(nh):
            vt_s[i, h * dv:(h + 1) * dv, :] = v_ref[sl, h * dv:(h + 1) * dv].T
        for d in range(2):
            la_hi = las[d].astype(BF16)
            la_lo = (las[d] - la_hi.astype(F32)).astype(BF16)
            bb = _dot(tris[d], jnp.concatenate([la_hi, la_lo], axis=1))
            bc_s[d, sl, :] = bb[:, :LANES] + bb[:, LANES:]
        return carry

    lax.fori_loop(0, n_chunks, cumulate, 0, unroll=2)

    st_s[...] = jnp.zeros_like(st_s)
    nu = SCAN_UNROLL
    assert n_chunks % nu == 0

    def body(i, carry):
        seq = [i * nu + u for u in range(nu)]
        cis = [seq, [jnp.where(j < n_ctx_chunks, n_ctx_chunks - 1 - j, n_chunks - 1 + n_ctx_chunks - j) for j in seq]]
        streams = [(d, u) for u in range(nu) for d in range(2)]
        sl = {(d, u): pl.ds(pl.multiple_of(cis[d][u] * c, c), c) for d, u in streams}
        q = {su: q_s[sl[su], :] for su in streams}
        k = {(d, u): k_refs[d][sl[d, u], :] for d, u in streams}
        la = {(d, u): la_refs[d][sl[d, u], :] for d, u in streams}
        bc = {(d, u): bc_s[d, sl[d, u], :] for d, u in streams}
        vs = {su: [v_ref[sl[su], h * dv:(h + 1) * dv] for h in range(nh)] for su in streams}

        st = [[st_s[d, h] for h in range(nh)] for d in range(2)]
        inter = {}
        for d, u in streams:
            su = (d, u)
            qg = (q[su] * jnp.exp(bc[su])).astype(BF16)
            tot = bc[su][0:1, :] if d else bc[su][c - 1:c, :]
            kg = (k[su] * jnp.exp(tot - bc[su])).astype(BF16)
            dec = jnp.exp(tot)
            for h in range(nh):
                qg_h = qg if nh == 1 else qg * head_lanes[h][1]
                inter[su, h] = _dot_nt(qg_h, st[d][h].astype(BF16))
                st[d][h] = st[d][h] * dec + _dot(vt_s[cis[d][u], h * dv:(h + 1) * dv, :], kg)
        for d in range(2):
            for h in range(nh):
                st_s[d, h] = st[d][h]

        scores = {(su, h): jnp.zeros((c, c), F32) for su in streams for h in range(nh)}
        m, level = c // 2, (c // 2).bit_length() - 1
        while m >= 1:
            for d, u in streams:
                su = (d, u)
                x = _level_operand(q[su], k[su], la[su], bc[su], m, d == 1).astype(BF16)
                for h in range(nh):
                    xh = x if nh == 1 else x * head_lanes[h][1]
                    scores[su, h] = jnp.where(lvls[d] == level, _dot_nt(xh, x), scores[su, h])
            m, level = m // 2, level - 1

        for d, u in streams:
            su = (d, u)
            qk = q[su] * k[su]
            for h in range(nh):
                v = vs[su][h]
                qk_h = qk if nh == 1 else jnp.where(head_lanes[h][0], qk, 0.0)
                o = _dot(scores[su, h].astype(BF16), v) + jnp.sum(qk_h, axis=1, keepdims=True) * v.astype(F32)
                o_s[d, sl[su], h * dv:(h + 1) * dv] = o + inter[su, h]
        return carry

    lax.fori_loop(0, n_chunks // nu, body, 0)


def _scan_scratch(tt, nh):
    n_chunks = tt // SCAN_CHUNK
    return [pltpu.VMEM((2, tt, nh * HEAD_DV), F32), pltpu.VMEM((2, nh, HEAD_DV, LANES), F32),
            pltpu.VMEM((2, tt, LANES), F32), pltpu.VMEM((n_chunks, nh * HEAD_DV, SCAN_CHUNK), BF16)]


def _rms_gate(o, gain, gate):
    o = o * lax.rsqrt(jnp.mean(jnp.square(o), axis=-1, keepdims=True) + RMS_EPS) * gain
    return o * (gate * jax.nn.sigmoid(gate))


def _hgrn_kernel(n_ctx_chunks, q_ref, i_ref, g_ref, ff_ref, fb_ref, lb_ref, gain_ref, y_ref,
                 q_s, kf_s, kb_s, laf_s, lab_s, *scan_s):
    c = SCAN_CHUNK
    n_chunks = q_s.shape[0] // c
    o_s = scan_s[0]
    lb_f = lb_ref[0:1, :]
    lb_b = lb_ref[1:2, :]

    def prep_chunk(sl):
        q = q_ref[sl, :].astype(F32)
        q_s[sl, :] = q * jax.nn.sigmoid(q) * (A_DK ** -0.5)
        las = []
        for f_ref, lb, k_s, la_s in ((ff_ref, lb_f, kf_s, laf_s), (fb_ref, lb_b, kb_s, lab_s)):
            f = f_ref[sl, :].astype(F32)
            k_s[sl, :] = (1.0 - lb) * jax.nn.sigmoid(-f)
            las.append(jnp.log(lb + (1.0 - lb) * jax.nn.sigmoid(f)))
            la_s[sl, :] = las[-1]
        return las

    _scan_both_directions(prep_chunk, q_s, kf_s, kb_s, laf_s, lab_s, i_ref, *scan_s, n_ctx_chunks)

    def fin(i, carry):
        sl = pl.ds(pl.multiple_of(i * c, c), c)
        o = o_s[0, sl, :] + o_s[1, sl, :]
        y_ref[sl, :] = _rms_gate(o, gain_ref[...], g_ref[sl, :].astype(F32)).astype(y_ref.dtype)
        return carry

    lax.fori_loop(0, n_chunks, fin, 0, unroll=2)


def _hgrn_mixer(p, lb, gain, ctx_len, col0):
    b, tt, _ = p.shape
    blk = lambda seg: pl.BlockSpec((None, tt, LANES), lambda i, h, s=seg: (i, 0, col0 + s * N_HEADS + h))
    seq = lambda: pltpu.VMEM((tt, LANES), F32)
    return pl.pallas_call(
        functools.partial(_hgrn_kernel, ctx_len // SCAN_CHUNK),
        grid=(b, N_HEADS),
        in_specs=[blk(0), blk(1), blk(2), blk(3), blk(4),
                  pl.BlockSpec((2, LANES), lambda i, h: (0, h)),
                  pl.BlockSpec((1, LANES), lambda i, h: (0, 0))],
        out_specs=pl.BlockSpec((None, tt, LANES), lambda i, h: (i, 0, h)),
        out_shape=jax.ShapeDtypeStruct((b, tt, MIX_W), BF16),
        scratch_shapes=[seq(), seq(), seq(), seq(), seq()] + _scan_scratch(tt, 1),
        compiler_params=_params(("arbitrary", "arbitrary")),
        name="hgrn2_mixer",
    )(p, p, p, p, p, lb, gain.reshape(1, HEAD_DV))


def _rope_rotate(x):
    n = x.shape[-1]
    lane = lax.broadcasted_iota(jnp.int32, x.shape, x.ndim - 1)
    first = (lane % 32) < 16
    return jnp.where(first, -pltpu.roll(x, n - 16, x.ndim - 1), pltpu.roll(x, 16, x.ndim - 1))


def _gla_kernel(n_ctx_chunks, q_ref, k_ref, v_ref, g_ref, gk_ref, wg_ref, bg_ref, cos_ref, sin_ref,
                gain_ref, y_ref, q_s, k_s, laf_s, lab_s, *scan_s):
    c = SCAN_CHUNK
    n_chunks = q_s.shape[0] // c
    nh = LANES // B_DK
    o_s = scan_s[0]

    def prep_chunk(sl):
        cos = cos_ref[sl, :]
        sin = sin_ref[sl, :]
        q = q_ref[sl, :].astype(F32)
        k = k_ref[sl, :].astype(F32)
        q_s[sl, :] = (q * cos + _rope_rotate(q) * sin) * (B_DK ** -0.5)
        k_s[sl, :] = k * cos + _rope_rotate(k) * sin
        gk = gk_ref[sl, :]
        las = []
        for d, la_s in ((0, laf_s), (1, lab_s)):
            z = _dot(gk, wg_ref[d]) + bg_ref[d]
            las.append(jax.nn.log_sigmoid(z) / GATE_LOGIT_NORMALIZER)
            la_s[sl, :] = las[-1]
        return las

    _scan_both_directions(prep_chunk, q_s, k_s, k_s, laf_s, lab_s, v_ref, *scan_s, n_ctx_chunks)

    def fin(i, carry):
        sl = pl.ds(pl.multiple_of(i * c, c), c)
        for h in range(nh):
            cols = slice(h * HEAD_DV, (h + 1) * HEAD_DV)
            o = o_s[0, sl, cols] + o_s[1, sl, cols]
            y_ref[sl, cols] = _rms_gate(o, gain_ref[...], g_ref[sl, cols].astype(F32)).astype(y_ref.dtype)
        return carry

    lax.fori_loop(0, n_chunks, fin, 0, unroll=2)


def _gla_mixer(p, p_gk, wg, bg, cos, sin, gain, ctx_len, col_q, col_k, col_v, col_g):
    b, tt, _ = p.shape
    nh = LANES // B_DK
    wide = nh * HEAD_DV
    pair = lambda c0: pl.BlockSpec((None, tt, LANES), lambda i, h: (i, 0, c0 + h))
    pair_v = lambda c0: pl.BlockSpec((None, tt, wide), lambda i, h: (i, 0, c0 * LANES // wide + h))
    seq = lambda: pltpu.VMEM((tt, LANES), F32)
    assert (col_v * LANES) % wide == 0 and (col_g * LANES) % wide == 0
    return pl.pallas_call(
        functools.partial(_gla_kernel, ctx_len // SCAN_CHUNK),
        grid=(b, N_HEADS // nh),
        in_specs=[pair(col_q), pair(col_k), pair_v(col_v), pair_v(col_g),
                  pl.BlockSpec((None, tt, LANES), lambda i, h: (i, 0, 0)),
                  pl.BlockSpec((2, None, LANES, LANES), lambda i, h: (0, h, 0, 0)),
                  pl.BlockSpec((2, None, 1, LANES), lambda i, h: (0, h, 0, 0)),
                  pl.BlockSpec((tt, LANES), lambda i, h: (0, 0)),
                  pl.BlockSpec((tt, LANES), lambda i, h: (0, 0)),
                  pl.BlockSpec((1, LANES), lambda i, h: (0, 0))],
        out_specs=pl.BlockSpec((None, tt, wide), lambda i, h: (i, 0, h)),
        out_shape=jax.ShapeDtypeStruct((b, tt, MIX_W), BF16),
        scratch_shapes=[seq(), seq(), seq(), seq()] + _scan_scratch(tt, nh),
        compiler_params=_params(("arbitrary", "arbitrary")),
        name="gla_mixer",
    )(p, p, p, p, p_gk, wg, bg, cos, sin, gain.reshape(1, HEAD_DV))


def _natten_kernel(ctx_len, rows, kr, q_ref, k_ref, v_ref, rpb_ref, y_ref, bias_ref):
    scale = HEAD_DV ** -0.5
    kc = k_ref[0:ctx_len, :]
    vc = v_ref[0:ctx_len, :]

    qi = lax.broadcasted_iota(jnp.int32, (GRID_W, LANES), 0)
    kj = lax.broadcasted_iota(jnp.int32, (GRID_W, LANES), 1) % GRID_W
    col_start = jnp.clip(qi - KC // 2, 0, GRID_W - KC)
    col_ok = (kj >= col_start) & (kj < col_start + KC)
    for v in range(kr):
        for blk in range(kr * GRID_W // LANES):
            d = v + blk * (LANES // GRID_W)
            src = jnp.broadcast_to(rpb_ref[d:d + 1, :], (GRID_W, LANES))
            rot = pltpu.roll(src, LANES - (KC - 1), 1, stride=1, stride_axis=0)
            bias_ref[v, :, blk * LANES:(blk + 1) * LANES] = jnp.where(col_ok, rot, MASK_VALUE)

    s = _dot_nt(q_ref[0:ctx_len, :], kc) * scale
    p = jnp.exp(s - jnp.max(s, axis=-1, keepdims=True))
    o = _dot(p.astype(BF16), vc) / jnp.sum(p, axis=-1, keepdims=True)
    y_ref[0:ctx_len, :] = o.astype(y_ref.dtype)

    nr = NATTEN_ROWS_PER_TRIP
    assert rows % nr == 0

    def body(i, carry):
        q_sl, b_sl, s_b, s_c = [], [], [], []
        for u in range(nr):
            r = i * nr + u
            start = jnp.clip(r - kr // 2, 0, rows - kr)
            q_sl.append(pl.ds(pl.multiple_of(ctx_len + r * GRID_W, GRID_W), GRID_W))
            b_sl.append(pl.ds(pl.multiple_of(ctx_len + start * GRID_W, GRID_W), kr * GRID_W))
            q = q_ref[q_sl[u], :]
            s_b.append(_dot_nt(q, k_ref[b_sl[u], :]) * scale + bias_ref[start - r + (KR_MAX - 1)])
            s_c.append(_dot_nt(q, kc) * scale)
        for u in range(nr):
            mx = jnp.maximum(jnp.max(s_b[u], axis=-1, keepdims=True), jnp.max(s_c[u], axis=-1, keepdims=True))
            p_b = jnp.exp(s_b[u] - mx)
            p_c = jnp.exp(s_c[u] - mx)
            den = jnp.sum(p_b, axis=-1, keepdims=True) + jnp.sum(p_c, axis=-1, keepdims=True)
            o = (_dot(p_b.astype(BF16), v_ref[b_sl[u], :]) + _dot(p_c.astype(BF16), vc)) / den
            y_ref[q_sl[u], :] = o.astype(y_ref.dtype)
        return carry

    lax.fori_loop(0, rows // nr, body, 0)


def _natten_mixer(p, rpb, ctx_len, col_q, col_k, col_v):
    b, tt, _ = p.shape
    rows = (tt - ctx_len) // GRID_W
    kr = KR_MAX
    assert rows >= KR_MAX and rows % 2 == 0 and 2 * GRID_W == LANES and 2 * KC - 1 <= GRID_W
    padded = jnp.pad(rpb.astype(F32), ((0, 0), (0, 0), (0, GRID_W - (2 * KC - 1))))
    pairs = jnp.concatenate([padded[:, :-1], padded[:, 1:]], axis=-1)
    head = lambda c0: pl.BlockSpec((None, tt, LANES), lambda i, h: (i, 0, c0 + h))
    return pl.pallas_call(
        functools.partial(_natten_kernel, ctx_len, rows, kr),
        grid=(b, N_HEADS),
        in_specs=[head(col_q), head(col_k), head(col_v),
                  pl.BlockSpec((None, 2 * KR_MAX - 2, LANES), lambda i, h: (h, 0, 0))],
        out_specs=pl.BlockSpec((None, tt, LANES), lambda i, h: (i, 0, h)),
        out_shape=jax.ShapeDtypeStruct((b, tt, MIX_W), BF16),
        scratch_shapes=[pltpu.VMEM((kr, GRID_W, kr * GRID_W), F32)],
        compiler_params=_params(("arbitrary", "arbitrary")),
        name="natten_mixer",
    )(p, p, p, pairs)


def _merge_kernel(ya_ref, yb_ref, yc_ref, w_ref, ga_ref, gb_ref, gc_ref, o_ref):
    acc = None
    for n, (y_ref, g_ref) in enumerate(((ya_ref, ga_ref), (yb_ref, gb_ref), (yc_ref, gc_ref))):
        z = jax.nn.sigmoid(g_ref[...].astype(F32)) * _dot(y_ref[...], w_ref[n])
        acc = z if acc is None else acc + z
    o_ref[...] = acc.astype(o_ref.dtype)


def _merge(ya, yb, yc, w_branch, layer, p, col_gate, d_model):
    m = ya.shape[0]
    tm = _pick_tile(m, (512, 256, 128))
    tn = _pick_tile(d_model, (1024, 512, 256, 128))
    nj = d_model // tn
    assert (col_gate * LANES) % tn == 0
    yspec = pl.BlockSpec((tm, MIX_W), lambda j, i: (i, 0))
    gspec = lambda n: pl.BlockSpec((tm, tn), lambda j, i, n=n: (i, col_gate * LANES // tn + n * nj + j))
    return pl.pallas_call(
        _merge_kernel,
        grid=(nj, m // tm),
        in_specs=[yspec, yspec, yspec,
                  pl.BlockSpec((None, N_BRANCH, MIX_W, tn), lambda j, i: (layer, 0, 0, j)),
                  gspec(0), gspec(1), gspec(2)],
        out_specs=pl.BlockSpec((tm, tn), lambda j, i: (i, j)),
        out_shape=jax.ShapeDtypeStruct((m, d_model), BF16),
        compiler_params=_params(("arbitrary", "arbitrary")),
        name="branch_merge",
    )(ya, yb, yc, w_branch, p, p, p)


def _proj_ln_kernel(alpha, ctx_len, tiles_per_batch, emit_h, a_ref, w_ref, x_ref, mod_ref, ln_ref, *rest):
    if emit_h:
        xo_ref, ho_ref, acc_ref = rest
    else:
        xo_ref, acc_ref = rest
    kk = pl.program_id(1)

    @pl.when(kk == 0)
    def _():
        acc_ref[...] = jnp.zeros_like(acc_ref)

    acc_ref[...] += _dot(a_ref[...], w_ref[...].astype(BF16))

    @pl.when(kk == pl.num_programs(1) - 1)
    def _():
        tm, d = acc_ref.shape
        rb = LN_ROW_BLOCK
        row0 = (pl.program_id(0) % tiles_per_batch) * tm

        def block(i, carry):
            sl = pl.ds(pl.multiple_of(i * rb, rb), rb)
            mod = mod_ref[jnp.where(row0 + i * rb < ctx_len, 1, 0)]
            sel = lambda j: mod[j:j + 1, :]
            y = alpha * x_ref[sl, :] + sel(0) * acc_ref[sl, :]
            mu = jnp.mean(y, axis=-1, keepdims=True)
            yc = y - mu
            var = jnp.mean(jnp.square(yc), axis=-1, keepdims=True)
            xn = yc * lax.rsqrt(var + LN_EPS) * ln_ref[0:1, :] + ln_ref[1:2, :]
            xo_ref[sl, :] = xn
            if emit_h:
                ho_ref[sl, :] = (xn * (1.0 + sel(1)) + sel(2)).astype(BF16)
            return carry

        lax.fori_loop(0, tm // rb, block, 0)


def _proj_ln(a, w, layer, x, mod_sel, ln_gb, alpha, ctx_len, tt, emit_h, name):
    m, k = a.shape
    d = w.shape[2]
    if w.dtype == BF16:
        tm, tk = _pick_tile(tt, (384, 256, 128)), k
    else:
        tm, tk = _pick_tile(tt, (768, 384, 256, 128)), _pick_tile(k, (512, 256, 128))
    assert tm % LN_ROW_BLOCK == 0 and ctx_len <= tm and ctx_len % LN_ROW_BLOCK == 0
    tpb = tt // tm
    out_shape = [jax.ShapeDtypeStruct((m, d), F32)]
    out_specs = [pl.BlockSpec((tm, d), lambda i, j: (i, 0))]
    if emit_h:
        out_shape.append(jax.ShapeDtypeStruct((m, d), BF16))
        out_specs.append(pl.BlockSpec((tm, d), lambda i, j: (i, 0)))
    res = pl.pallas_call(
        functools.partial(_proj_ln_kernel, alpha, ctx_len, tpb, emit_h),
        grid=(m // tm, k // tk),
        in_specs=[pl.BlockSpec((tm, tk), lambda i, j: (i, j)),
                  pl.BlockSpec((None, tk, d), lambda i, j: (layer, j, 0)),
                  pl.BlockSpec((tm, d), lambda i, j: (i, 0)),
                  pl.BlockSpec((None, 2, 3, d), lambda i, j: (i // tpb, 0, 0, 0)),
                  pl.BlockSpec((2, d), lambda i, j: (0, 0))],
        out_specs=out_specs,
        out_shape=out_shape,
        scratch_shapes=[pltpu.VMEM((tm, d), F32)],
        compiler_params=_params(("arbitrary", "arbitrary")),
        name=name,
    )(a, w, x, mod_sel, ln_gb)
    return (res[0], res[1]) if emit_h else (res[0], None)


def _rope_tables(seq, ctx_len):
    half = B_DK // 2
    freqs = ROPE_THETA ** (-jnp.arange(0, half, 2, dtype=F32) / half)
    pos = jnp.arange(seq)
    ang_r = (pos // GRID_W).astype(F32)[:, None] * freqs
    ang_c = (pos % GRID_W).astype(F32)[:, None] * freqs
    ang = jnp.concatenate([ang_r, ang_r, ang_c, ang_c], axis=-1)
    ang = jnp.concatenate([jnp.zeros((ctx_len, B_DK), F32), ang], axis=0)
    ang = jnp.concatenate([ang, ang], axis=-1)
    return jnp.cos(ang), jnp.sin(ang)


def kernel(x, c, ctx, c_ctx, w_ada, b_ada, w_in, hgrn_lb_logits, hgrn_norm_g, gla_w_gk2, gla_b_gk2, gla_norm_g,
           natten_rpb, w_branch, w_out, ln1_g, ln1_b, ln2_g, ln2_b, w_mlp1, w_mlp2):
    bsz, seq, d = x.shape
    ctx_len = ctx.shape[1]
    depth = w_ada.shape[0]
    tt = ctx_len + seq
    m_tok = bsz * tt
    alpha = (2 * depth) ** 0.25
    assert ctx_len % SCAN_CHUNK == 0 and seq % SCAN_CHUNK == 0 and seq % GRID_W == 0 and bsz + 1 <= 8

    seg = N_HEADS * A_DK
    gk_lo = 5 * seg + 2 * N_HEADS * B_DK + 2 * MIX_W
    gk_hi = gk_lo + 2 * GK_RANK
    n_cg = 3 * MIX_W + N_BRANCH * d
    assert gk_lo % 1024 == 0 and w_in.shape[2] == gk_hi + n_cg
    col_a = 0
    col_bq = 5 * seg // LANES
    col_bk = col_bq + N_HEADS * B_DK // LANES
    col_bv = col_bk + N_HEADS * B_DK // LANES
    col_bg = col_bv + MIX_W // LANES
    col_cq = 0
    col_ck = col_cq + MIX_W // LANES
    col_cv = col_ck + MIX_W // LANES
    col_gate = col_cv + MIX_W // LANES

    c_rows = jnp.zeros((8, d), F32).at[:bsz].set(c).at[bsz].set(c_ctx)
    mod = _ada_all_layers(c_rows, w_ada, b_ada).reshape(depth, 8, 6, d)

    def mod_sel(l, idx):
        lat = mod[l, :bsz][:, idx, :]
        cx = jnp.broadcast_to(mod[l, bsz][idx, :], lat.shape)
        return jnp.stack([lat, cx], axis=1)

    lb_p = jax.nn.softmax(hgrn_lb_logits.astype(F32), axis=0)
    lb_cum = jnp.cumsum(lb_p, axis=0)
    lower_bounds = jnp.concatenate([jnp.zeros_like(lb_cum[:1]), lb_cum[:-1]], axis=0)

    cos, sin = _rope_tables(seq, ctx_len)
    w_out_bf = w_out.astype(BF16)

    xs = jnp.concatenate([ctx, x], axis=1)
    h = _modulate(xs, mod_sel(0, np.array([1, 0])), ctx_len).reshape(m_tok, d)
    xs = xs.reshape(m_tok, d)

    w_ab = w_in[:, :, :gk_lo].astype(BF16)
    w_cg = w_in[:, :, gk_hi:].astype(BF16)
    w_gk = jnp.pad(w_in[:, :, gk_lo:gk_hi], ((0, 0), (0, 0), (0, LANES - 2 * GK_RANK))).astype(BF16)
    w_branch_bf = w_branch.astype(BF16)

    for l in range(depth):
        p = _matmul(h, w_ab, l, name="in_proj_ab").reshape(bsz, tt, gk_lo)
        p_cg = _matmul(h, w_cg, l, name="in_proj_cg").reshape(bsz, tt, n_cg)
        p_gk = _matmul(h, w_gk, l, name="in_proj_gk").reshape(bsz, tt, LANES)

        n_pairs = N_HEADS * B_DK // LANES
        wg = jnp.zeros((2, n_pairs, LANES, LANES), F32)
        w2 = gla_w_gk2[l].reshape(2, GK_RANK, n_pairs, LANES).transpose(0, 2, 1, 3)
        wg = wg.at[0, :, :GK_RANK].set(w2[0]).at[1, :, GK_RANK:2 * GK_RANK].set(w2[1]).astype(BF16)
        bg = gla_b_gk2[l].reshape(2, n_pairs, 1, LANES)

        y_a = _hgrn_mixer(p, lower_bounds[l], hgrn_norm_g[l], ctx_len, col_a)
        y_b = _gla_mixer(p, p_gk, wg, bg, cos, sin, gla_norm_g[l], ctx_len, col_bq, col_bk, col_bv, col_bg)
        y_c = _natten_mixer(p_cg, natten_rpb[l], ctx_len, col_cq, col_ck, col_cv)

        merged = _merge(y_a.reshape(m_tok, MIX_W), y_b.reshape(m_tok, MIX_W), y_c.reshape(m_tok, MIX_W),
                        w_branch_bf, l, p_cg.reshape(m_tok, n_cg), col_gate, d)
        xs, h2 = _proj_ln(merged, w_out_bf, l, xs, mod_sel(l, np.array([2, 4, 3])),
                          jnp.stack([ln1_g[l], ln1_b[l]]), alpha, ctx_len, tt, True, "out_proj_ln")
        u = _matmul(h2, w_mlp1, l, sq_relu=True, name="mlp_up")
        last = l == depth - 1
        if last:
            nxt = jnp.zeros((bsz, 2, 3, d), F32).at[:, :, 0].set(mod_sel(l, np.array([5]))[:, :, 0])
        else:
            nxt = jnp.concatenate([mod_sel(l, np.array([5])), mod_sel(l + 1, np.array([1, 0]))], axis=2)
        xs, h = _proj_ln(u, w_mlp2, l, xs, nxt, jnp.stack([ln2_g[l], ln2_b[l]]),
                         alpha, ctx_len, tt, not last, "mlp_down_ln")

    return xs.reshape(bsz, tt, d)[:, ctx_len:, :]
```

```python
import functools

import jax
import jax.numpy as jnp
import numpy as np
from jax import lax
from jax.experimental import pallas as pl
from jax.experimental.pallas import tpu as pltpu

GRID_W = 64
N_BRANCH = 3
MIX_W = 1024
N_HEADS = 8
A_DK = 128
B_DK = 64
HEAD_DV = MIX_W // N_HEADS
GK_RANK = 16
GATE_LOGIT_NORMALIZER = 16.0
KR_MAX = 8
KC = 16
ROPE_THETA = 10000.0
RMS_EPS = 1e-6
LN_EPS = 1e-5
MASK_VALUE = -1e30
LOG2E = 1.4426950408889634

LANES = 128
SCAN_CHUNK = 128
SCAN_UNROLL = 2
NATTEN_ROWS_PER_TRIP = 4
LN_ROW_BLOCK = 128
VMEM_LIMIT_BYTES = 56 * 1024 * 1024

F32 = jnp.float32
BF16 = jnp.bfloat16


def _pick_tile(n, prefs):
    for t in prefs:
        if n % t == 0:
            return t
    return n


def _params(sem):
    return pltpu.CompilerParams(dimension_semantics=sem, vmem_limit_bytes=VMEM_LIMIT_BYTES)


def _dot(a, b):
    return jnp.dot(a, b, preferred_element_type=F32)


def _dot_nt(a, b):
    return lax.dot_general(a, b, (((1,), (1,)), ((), ())), preferred_element_type=F32)


def _ada_kernel(c_ref, w_ref, b_ref, o_ref):
    c = c_ref[...]
    s = (c * jax.nn.sigmoid(c)).astype(BF16)
    o_ref[...] = _dot(s, w_ref[...].astype(BF16)) + b_ref[...]


def _ada_all_layers(c_rows, w_ada, b_ada):
    depth, d, n = w_ada.shape
    tn = _pick_tile(n, (1024, 512, 256, 128))
    return pl.pallas_call(
        _ada_kernel,
        grid=(depth, n // tn),
        in_specs=[pl.BlockSpec((8, d), lambda l, j: (0, 0)),
                  pl.BlockSpec((None, d, tn), lambda l, j: (l, 0, j)),
                  pl.BlockSpec((None, 1, tn), lambda l, j: (l, 0, j))],
        out_specs=pl.BlockSpec((None, 8, tn), lambda l, j: (l, 0, j)),
        out_shape=jax.ShapeDtypeStruct((depth, 8, n), F32),
        compiler_params=_params(("arbitrary", "arbitrary")),
        name="ada_mod",
    )(c_rows, w_ada, b_ada.reshape(depth, 1, n))


def _modulate_kernel(ctx_len, x_ref, m_ref, h_ref):
    x = x_ref[...]
    rows = lax.broadcasted_iota(jnp.int32, x.shape, 0) + pl.program_id(1) * x.shape[0]
    is_ctx = rows < ctx_len
    sc = jnp.where(is_ctx, m_ref[1, 0:1, :], m_ref[0, 0:1, :])
    sh = jnp.where(is_ctx, m_ref[1, 1:2, :], m_ref[0, 1:2, :])
    h_ref[...] = (x * (1.0 + sc) + sh).astype(BF16)


def _modulate(x, mod_sel, ctx_len):
    b, tt, d = x.shape
    tm = _pick_tile(tt, (768, 384, 256, 128))
    return pl.pallas_call(
        functools.partial(_modulate_kernel, ctx_len),
        grid=(b, tt // tm),
        in_specs=[pl.BlockSpec((None, tm, d), lambda i, j: (i, j, 0)),
                  pl.BlockSpec((None, 2, 2, d), lambda i, j: (i, 0, 0, 0))],
        out_specs=pl.BlockSpec((None, tm, d), lambda i, j: (i, j, 0)),
        out_shape=jax.ShapeDtypeStruct((b, tt, d), BF16),
        compiler_params=_params(("arbitrary", "arbitrary")),
        name="modulate",
    )(x, mod_sel)


def _mm_kernel(sq_relu, a_ref, w_ref, o_ref, *scratch):
    if scratch:
        wb_ref, = scratch

        @pl.when(pl.program_id(1) == 0)
        def _():
            wb_ref[...] = w_ref[...].astype(BF16)
    else:
        wb_ref = w_ref

    r = _dot(a_ref[...], wb_ref[...])
    if sq_relu:
        r = jnp.square(jnp.maximum(r, 0.0))
    o_ref[...] = r.astype(o_ref.dtype)


def _matmul(a, w, layer, sq_relu=False, name="matmul"):
    m, k = a.shape
    n = w.shape[2]
    tm = _pick_tile(m, (1152, 1024, 512, 256, 128))
    tn = _pick_tile(n, (1024, 512, 256, 128))
    return pl.pallas_call(
        functools.partial(_mm_kernel, sq_relu),
        grid=(n // tn, m // tm),
        in_specs=[pl.BlockSpec((tm, k), lambda j, i: (i, 0)),
                  pl.BlockSpec((None, k, tn), lambda j, i: (layer, 0, j))],
        out_specs=pl.BlockSpec((tm, tn), lambda j, i: (i, j)),
        out_shape=jax.ShapeDtypeStruct((m, n), BF16),
        scratch_shapes=[] if w.dtype == BF16 else [pltpu.VMEM((k, tn), BF16)],
        compiler_params=_params(("arbitrary", "arbitrary")),
        name=name,
    )(a, w)


def _mm_shifted_kernel(shift, a_ref, wa_ref, wb_ref, o_ref, w_s):
    @pl.when(pl.program_id(1) == 0)
    def _():
        tn = wa_ref.shape[1]
        both = jnp.concatenate([wa_ref[...], wb_ref[...]], axis=1).astype(F32)
        w_s[...] = both[:, shift:shift + tn].astype(BF16)

    o_ref[...] = _dot(a_ref[...], w_s[...]).astype(o_ref.dtype)


def _matmul_shifted(a, w, layer, shift, n, name):
    m, k = a.shape
    tm = _pick_tile(m, (1152, 1024, 512, 256, 128))
    tn = _pick_tile(n, (1024, 512, 256, 128))
    assert 0 < shift < LANES and w.shape[2] >= shift + n
    return pl.pallas_call(
        functools.partial(_mm_shifted_kernel, shift),
        grid=(n // tn, m // tm),
        in_specs=[pl.BlockSpec((tm, k), lambda j, i: (i, 0)),
                  pl.BlockSpec((None, k, tn), lambda j, i: (layer, 0, j)),
                  pl.BlockSpec((None, k, LANES), lambda j, i: (layer, 0, (j + 1) * (tn // LANES)))],
        out_specs=pl.BlockSpec((tm, tn), lambda j, i: (i, j)),
        out_shape=jax.ShapeDtypeStruct((m, n), BF16),
        scratch_shapes=[pltpu.VMEM((k, tn), BF16)],
        compiler_params=_params(("arbitrary", "arbitrary")),
        name=name,
    )(a, w, w)


def _level_operand(q, k, la, bc, m, rev):
    c, dk = q.shape
    if m >= 8:
        pieces = []
        for jb in range(c // m):
            rows = slice(jb * m, (jb + 1) * m)
            r = (jb // 2) * 2 * m + (m if rev else m - 1)
            cvec = bc[r:r + 1, :]
            if (jb % 2 == 1) != rev:
                pieces.append(q[rows, :] * jnp.exp2(bc[rows, :] - cvec))
            else:
                pieces.append(k[rows, :] * jnp.exp2(cvec - bc[rows, :]))
        return jnp.concatenate(pieces, axis=0)
    shape3 = (c // 8, 8, dk)
    sub = lax.broadcasted_iota(jnp.int32, (1, 8, dk), 1)
    upper = (sub & m) != 0
    q_role = jnp.logical_not(upper) if rev else upper
    q3, k3, bc3 = q.reshape(shape3), k.reshape(shape3), bc.reshape(shape3)
    if m == 1:
        e2 = jnp.where(q_role, la.reshape(shape3), 0.0)
    else:
        if m == 4:
            r = 4 if rev else 3
            c3 = bc3[:, r:r + 1, :]
        else:
            r0, r1 = (2, 6) if rev else (1, 5)
            c3 = jnp.where(sub < 4, bc3[:, r0:r0 + 1, :], bc3[:, r1:r1 + 1, :])
        e2 = (bc3 - c3) * jnp.where(q_role, 1.0, -1.0)
    return (jnp.where(q_role, q3, k3) * jnp.exp2(e2)).reshape(c, dk)


def _level_matrix(c, rev):
    t = lax.broadcasted_iota(jnp.int32, (c, c), 0)
    s = lax.broadcasted_iota(jnp.int32, (c, c), 1)
    x = t ^ s
    lvl = jnp.full((c, c), -1, jnp.int32)
    b = 1
    while b < c:
        lvl = lvl + (x >= b).astype(jnp.int32)
        b *= 2
    return jnp.where((t < s) if rev else (t > s), lvl, -1)


def _scan_both_directions(prep_chunk, q_s, kf_s, kb_s, laf_s, lab_s, v_ref, o_s, st_s, bc_s, vt_s, sc_s, n_ctx_chunks):
    c = SCAN_CHUNK
    n_chunks = q_s.shape[0] // c
    nh = st_s.shape[1]
    dv = v_ref.shape[1] // nh
    ti = lax.broadcasted_iota(jnp.int32, (c, c), 0)
    si = lax.broadcasted_iota(jnp.int32, (c, c), 1)
    lane = lax.broadcasted_iota(jnp.int32, (c, LANES), 1) // (LANES // nh)
    head_lanes = [(lane == h, (lane == h).astype(BF16)) for h in range(nh)]
    k_refs, la_refs = (kf_s, kb_s), (laf_s, lab_s)
    tris = ((si <= ti).astype(BF16), (si >= ti).astype(BF16))
    lvls = (_level_matrix(c, False), _level_matrix(c, True))

    def cumulate(i, carry):
        sl = pl.ds(pl.multiple_of(i * c, c), c)
        las = prep_chunk(sl)
        for h in range(nh):
            vt_s[i, h * dv:(h + 1) * dv, :] = v_ref[sl, h * dv:(h + 1) * dv].T
        for d in range(2):
            la_hi = las[d].astype(BF16)
            la_lo = (las[d] - la_hi.astype(F32)).astype(BF16)
            bb = _dot(tris[d], jnp.concatenate([la_hi, la_lo], axis=1))
            bc_s[d, sl, :] = (bb[:, :LANES] + bb[:, LANES:]) * LOG2E
            la_refs[d][sl, :] = las[d] * LOG2E
        return carry

    lax.fori_loop(0, n_chunks, cumulate, 0, unroll=2)

    st_s[...] = jnp.zeros_like(st_s)
    nu = SCAN_UNROLL
    n_groups = n_chunks // nu
    assert n_chunks % nu == 0 and n_groups >= 2
    streams = [(d, u) for u in range(nu) for d in range(2)]
    slot = {(su, h): (su[0] * nu + su[1]) * nh + h for su in streams for h in range(nh)}

    def group_rows(g):
        seq = [g * nu + u for u in range(nu)]
        cis = [seq, [jnp.where(j < n_ctx_chunks, n_ctx_chunks - 1 - j, n_chunks - 1 + n_ctx_chunks - j) for j in seq]]
        return cis, {(d, u): pl.ds(pl.multiple_of(cis[d][u] * c, c), c) for d, u in streams}

    def state_phase(g):
        cis, sl = group_rows(g)
        q = {su: q_s[sl[su], :] for su in streams}
        k = {(d, u): k_refs[d][sl[d, u], :] for d, u in streams}
        la = {(d, u): la_refs[d][sl[d, u], :] for d, u in streams}
        bc = {(d, u): bc_s[d, sl[d, u], :] for d, u in streams}
        st = [[st_s[d, h] for h in range(nh)] for d in range(2)]
        inter = {}
        for d, u in streams:
            su = (d, u)
            qg = (q[su] * jnp.exp2(bc[su])).astype(BF16)
            tot = bc[su][0:1, :] if d else bc[su][c - 1:c, :]
            kg = (k[su] * jnp.exp2(tot - bc[su])).astype(BF16)
            dec = jnp.exp2(tot)
            for h in range(nh):
                qg_h = qg if nh == 1 else qg * head_lanes[h][1]
                inter[su, h] = _dot_nt(qg_h, st[d][h].astype(BF16))
                st[d][h] = st[d][h] * dec + _dot(vt_s[cis[d][u], h * dv:(h + 1) * dv, :], kg)
        for d in range(2):
            for h in range(nh):
                st_s[d, h] = st[d][h]
        return sl, q, k, la, bc, inter

    def score_matmuls(g):
        _, sl = group_rows(g)
        return sl, {(su, h): _dot(sc_s[slot[su, h]], v_ref[sl[su], h * dv:(h + 1) * dv])
                    for su in streams for h in range(nh)}

    def level_phase(sl, q, k, la, bc, inter):
        scores = {key: jnp.zeros((c, c), F32) for key in slot}
        m, level = c // 2, (c // 2).bit_length() - 1
        while m >= 1:
            for d, u in streams:
                su = (d, u)
                x = _level_operand(q[su], k[su], la[su], bc[su], m, d == 1).astype(BF16)
                for h in range(nh):
                    xh = x if nh == 1 else x * head_lanes[h][1]
                    scores[su, h] = jnp.where(lvls[d] == level, _dot_nt(xh, x), scores[su, h])
            m, level = m // 2, level - 1
        for d, u in streams:
            su = (d, u)
            qk = q[su] * k[su]
            for h in range(nh):
                sc_s[slot[su, h]] = scores[su, h].astype(BF16)
                v = v_ref[sl[su], h * dv:(h + 1) * dv].astype(F32)
                qk_h = qk if nh == 1 else jnp.where(head_lanes[h][0], qk, 0.0)
                o_s[d, sl[su], h * dv:(h + 1) * dv] = inter[su, h] + jnp.sum(qk_h, axis=1, keepdims=True) * v

    def add_scores_part(sl, outs):
        for (su, h), o in outs.items():
            cols = slice(h * dv, (h + 1) * dv)
            o_s[su[0], sl[su], cols] = o_s[su[0], sl[su], cols] + o

    level_phase(*state_phase(0))

    def body(g, carry):
        cur = state_phase(g)
        sl_prev, outs = score_matmuls(g - 1)
        level_phase(*cur)
        add_scores_part(sl_prev, outs)
        return carry

    lax.fori_loop(1, n_groups, body, 0)
    add_scores_part(*score_matmuls(n_groups - 1))


def _scan_scratch(tt, nh):
    n_chunks = tt // SCAN_CHUNK
    return [pltpu.VMEM((2, tt, nh * HEAD_DV), F32), pltpu.VMEM((2, nh, HEAD_DV, LANES), F32),
            pltpu.VMEM((2, tt, LANES), F32), pltpu.VMEM((n_chunks, nh * HEAD_DV, SCAN_CHUNK), BF16),
            pltpu.VMEM((2 * SCAN_UNROLL * nh, SCAN_CHUNK, SCAN_CHUNK), BF16)]


def _rms_gate(o, gain, gate):
    o = o * lax.rsqrt(jnp.mean(jnp.square(o), axis=-1, keepdims=True) + RMS_EPS) * gain
    return o * (gate * jax.nn.sigmoid(gate))


def _hgrn_kernel(n_ctx_chunks, q_ref, i_ref, g_ref, ff_ref, fb_ref, lb_ref, gain_ref, y_ref,
                 q_s, kf_s, kb_s, laf_s, lab_s, *scan_s):
    c = SCAN_CHUNK
    n_chunks = q_s.shape[0] // c
    o_s = scan_s[0]
    lb_f = lb_ref[0:1, :]
    lb_b = lb_ref[1:2, :]

    def prep_chunk(sl):
        q = q_ref[sl, :].astype(F32)
        q_s[sl, :] = q * jax.nn.sigmoid(q) * (A_DK ** -0.5)
        las = []
        for f_ref, lb, k_s in ((ff_ref, lb_f, kf_s), (fb_ref, lb_b, kb_s)):
            f = f_ref[sl, :].astype(F32)
            k_s[sl, :] = (1.0 - lb) * jax.nn.sigmoid(-f)
            las.append(jnp.log(lb + (1.0 - lb) * jax.nn.sigmoid(f)))
        return las

    _scan_both_directions(prep_chunk, q_s, kf_s, kb_s, laf_s, lab_s, i_ref, *scan_s, n_ctx_chunks)

    def fin(i, carry):
        sl = pl.ds(pl.multiple_of(i * c, c), c)
        o = o_s[0, sl, :] + o_s[1, sl, :]
        y_ref[sl, :] = _rms_gate(o, gain_ref[...], g_ref[sl, :].astype(F32)).astype(y_ref.dtype)
        return carry

    lax.fori_loop(0, n_chunks, fin, 0, unroll=2)


def _hgrn_mixer(p, lb, gain, ctx_len, col0):
    b, tt, _ = p.shape
    blk = lambda seg: pl.BlockSpec((None, tt, LANES), lambda i, h, s=seg: (i, 0, col0 + s * N_HEADS + h))
    seq = lambda: pltpu.VMEM((tt, LANES), F32)
    return pl.pallas_call(
        functools.partial(_hgrn_kernel, ctx_len // SCAN_CHUNK),
        grid=(b, N_HEADS),
        in_specs=[blk(0), blk(1), blk(2), blk(3), blk(4),
                  pl.BlockSpec((2, LANES), lambda i, h: (0, h)),
                  pl.BlockSpec((1, LANES), lambda i, h: (0, 0))],
        out_specs=pl.BlockSpec((None, tt, LANES), lambda i, h: (i, 0, h)),
        out_shape=jax.ShapeDtypeStruct((b, tt, MIX_W), BF16),
        scratch_shapes=[seq(), seq(), seq(), seq(), seq()] + _scan_scratch(tt, 1),
        compiler_params=_params(("arbitrary", "arbitrary")),
        name="hgrn2_mixer",
    )(p, p, p, p, p, lb, gain.reshape(1, HEAD_DV))


def _rope_rotate(x):
    n = x.shape[-1]
    lane = lax.broadcasted_iota(jnp.int32, x.shape, x.ndim - 1)
    first = (lane % 32) < 16
    return jnp.where(first, -pltpu.roll(x, n - 16, x.ndim - 1), pltpu.roll(x, 16, x.ndim - 1))


def _gla_kernel(n_ctx_chunks, q_ref, k_ref, v_ref, g_ref, gk_ref, wg_ref, bg_ref, cos_ref, sin_ref,
                gain_ref, y_ref, q_s, k_s, laf_s, lab_s, *scan_s):
    c = SCAN_CHUNK
    n_chunks = q_s.shape[0] // c
    nh = LANES // B_DK
    o_s = scan_s[0]

    def prep_chunk(sl):
        cos = cos_ref[sl, :]
        sin = sin_ref[sl, :]
        q = q_ref[sl, :].astype(F32)
        k = k_ref[sl, :].astype(F32)
        q_s[sl, :] = (q * cos + _rope_rotate(q) * sin) * (B_DK ** -0.5)
        k_s[sl, :] = k * cos + _rope_rotate(k) * sin
        gk = gk_ref[sl, :]
        return [jax.nn.log_sigmoid(_dot(gk, wg_ref[d]) + bg_ref[d]) / GATE_LOGIT_NORMALIZER for d in range(2)]

    _scan_both_directions(prep_chunk, q_s, k_s, k_s, laf_s, lab_s, v_ref, *scan_s, n_ctx_chunks)

    def fin(i, carry):
        sl = pl.ds(pl.multiple_of(i * c, c), c)
        for h in range(nh):
            cols = slice(h * HEAD_DV, (h + 1) * HEAD_DV)
            o = o_s[0, sl, cols] + o_s[1, sl, cols]
            y_ref[sl, cols] = _rms_gate(o, gain_ref[...], g_ref[sl, cols].astype(F32)).astype(y_ref.dtype)
        return carry

    lax.fori_loop(0, n_chunks, fin, 0, unroll=2)


def _gla_mixer(p, p_gk, wg, bg, cos, sin, gain, ctx_len, col_q, col_k, col_v, col_g):
    b, tt, _ = p.shape
    nh = LANES // B_DK
    wide = nh * HEAD_DV
    pair = lambda c0: pl.BlockSpec((None, tt, LANES), lambda i, h: (i, 0, c0 + h))
    pair_v = lambda c0: pl.BlockSpec((None, tt, wide), lambda i, h: (i, 0, c0 * LANES // wide + h))
    seq = lambda: pltpu.VMEM((tt, LANES), F32)
    assert (col_v * LANES) % wide == 0 and (col_g * LANES) % wide == 0
    return pl.pallas_call(
        functools.partial(_gla_kernel, ctx_len // SCAN_CHUNK),
        grid=(b, N_HEADS // nh),
        in_specs=[pair(col_q), pair(col_k), pair_v(col_v), pair_v(col_g),
                  pl.BlockSpec((None, tt, LANES), lambda i, h: (i, 0, 0)),
                  pl.BlockSpec((2, None, LANES, LANES), lambda i, h: (0, h, 0, 0)),
                  pl.BlockSpec((2, None, 1, LANES), lambda i, h: (0, h, 0, 0)),
                  pl.BlockSpec((tt, LANES), lambda i, h: (0, 0)),
                  pl.BlockSpec((tt, LANES), lambda i, h: (0, 0)),
                  pl.BlockSpec((1, LANES), lambda i, h: (0, 0))],
        out_specs=pl.BlockSpec((None, tt, wide), lambda i, h: (i, 0, h)),
        out_shape=jax.ShapeDtypeStruct((b, tt, MIX_W), BF16),
        scratch_shapes=[seq(), seq(), seq(), seq()] + _scan_scratch(tt, nh),
        compiler_params=_params(("arbitrary", "arbitrary")),
        name="gla_mixer",
    )(p, p, p, p, p_gk, wg, bg, cos, sin, gain.reshape(1, HEAD_DV))


def _natten_kernel(ctx_len, rows, kr, q_ref, k_ref, v_ref, rpb_ref, y_ref, bias_ref):
    scale = HEAD_DV ** -0.5
    kc = k_ref[0:ctx_len, :]
    vc = v_ref[0:ctx_len, :]

    qi = lax.broadcasted_iota(jnp.int32, (GRID_W, LANES), 0)
    kj = lax.broadcasted_iota(jnp.int32, (GRID_W, LANES), 1) % GRID_W
    col_start = jnp.clip(qi - KC // 2, 0, GRID_W - KC)
    col_ok = (kj >= col_start) & (kj < col_start + KC)
    for v in range(kr):
        for blk in range(kr * GRID_W // LANES):
            d = v + blk * (LANES // GRID_W)
            src = jnp.broadcast_to(rpb_ref[d:d + 1, :], (GRID_W, LANES))
            rot = pltpu.roll(src, LANES - (KC - 1), 1, stride=1, stride_axis=0)
            bias_ref[v, :, blk * LANES:(blk + 1) * LANES] = jnp.where(col_ok, rot, MASK_VALUE)

    s = _dot_nt(q_ref[0:ctx_len, :], kc) * scale
    p = jnp.exp(s - jnp.max(s, axis=-1, keepdims=True))
    o = _dot(p.astype(BF16), vc) / jnp.sum(p, axis=-1, keepdims=True)
    y_ref[0:ctx_len, :] = o.astype(y_ref.dtype)

    nr = NATTEN_ROWS_PER_TRIP
    assert rows % nr == 0

    def body(i, carry):
        q_sl, b_sl, s_b, s_c = [], [], [], []
        for u in range(nr):
            r = i * nr + u
            start = jnp.clip(r - kr // 2, 0, rows - kr)
            q_sl.append(pl.ds(pl.multiple_of(ctx_len + r * GRID_W, GRID_W), GRID_W))
            b_sl.append(pl.ds(pl.multiple_of(ctx_len + start * GRID_W, GRID_W), kr * GRID_W))
            q = q_ref[q_sl[u], :]
            s_b.append(_dot_nt(q, k_ref[b_sl[u], :]) * scale + bias_ref[start - r + (KR_MAX - 1)])
            s_c.append(_dot_nt(q, kc) * scale)
        for u in range(nr):
            mx = jnp.maximum(jnp.max(s_b[u], axis=-1, keepdims=True), jnp.max(s_c[u], axis=-1, keepdims=True))
            p_b = jnp.exp(s_b[u] - mx)
            p_c = jnp.exp(s_c[u] - mx)
            den = jnp.sum(p_b, axis=-1, keepdims=True) + jnp.sum(p_c, axis=-1, keepdims=True)
            o = (_dot(p_b.astype(BF16), v_ref[b_sl[u], :]) + _dot(p_c.astype(BF16), vc)) / den
            y_ref[q_sl[u], :] = o.astype(y_ref.dtype)
        return carry

    lax.fori_loop(0, rows // nr, body, 0)


def _natten_mixer(p, rpb, ctx_len, col_q, col_k, col_v):
    b, tt, _ = p.shape
    rows = (tt - ctx_len) // GRID_W
    kr = KR_MAX
    assert rows >= KR_MAX and rows % 2 == 0 and 2 * GRID_W == LANES and 2 * KC - 1 <= GRID_W
    padded = jnp.pad(rpb.astype(F32), ((0, 0), (0, 0), (0, GRID_W - (2 * KC - 1))))
    pairs = jnp.concatenate([padded[:, :-1], padded[:, 1:]], axis=-1)
    head = lambda c0: pl.BlockSpec((None, tt, LANES), lambda i, h: (i, 0, c0 + h))
    return pl.pallas_call(
        functools.partial(_natten_kernel, ctx_len, rows, kr),
        grid=(b, N_HEADS),
        in_specs=[head(col_q), head(col_k), head(col_v),
                  pl.BlockSpec((None, 2 * KR_MAX - 2, LANES), lambda i, h: (h, 0, 0))],
        out_specs=pl.BlockSpec((None, tt, LANES), lambda i, h: (i, 0, h)),
        out_shape=jax.ShapeDtypeStruct((b, tt, MIX_W), BF16),
        scratch_shapes=[pltpu.VMEM((kr, GRID_W, kr * GRID_W), F32)],
        compiler_params=_params(("arbitrary", "arbitrary")),
        name="natten_mixer",
    )(p, p, p, pairs)


def _merge_kernel(ya_ref, yb_ref, yc_ref, w_ref, ga_ref, gb_ref, gc_ref, o_ref):
    acc = None
    for n, (y_ref, g_ref) in enumerate(((ya_ref, ga_ref), (yb_ref, gb_ref), (yc_ref, gc_ref))):
        z = jax.nn.sigmoid(g_ref[...].astype(F32)) * _dot(y_ref[...], w_ref[n])
        acc = z if acc is None else acc + z
    o_ref[...] = acc.astype(o_ref.dtype)


def _merge(ya, yb, yc, w_branch, layer, p, col_gate, d_model):
    m = ya.shape[0]
    tm = _pick_tile(m, (512, 256, 128))
    tn = _pick_tile(d_model, (1024, 512, 256, 128))
    nj = d_model // tn
    assert (col_gate * LANES) % tn == 0
    yspec = pl.BlockSpec((tm, MIX_W), lambda j, i: (i, 0))
    gspec = lambda n: pl.BlockSpec((tm, tn), lambda j, i, n=n: (i, col_gate * LANES // tn + n * nj + j))
    return pl.pallas_call(
        _merge_kernel,
        grid=(nj, m // tm),
        in_specs=[yspec, yspec, yspec,
                  pl.BlockSpec((None, N_BRANCH, MIX_W, tn), lambda j, i: (layer, 0, 0, j)),
                  gspec(0), gspec(1), gspec(2)],
        out_specs=pl.BlockSpec((tm, tn), lambda j, i: (i, j)),
        out_shape=jax.ShapeDtypeStruct((m, d_model), BF16),
        compiler_params=_params(("arbitrary", "arbitrary")),
        name="branch_merge",
    )(ya, yb, yc, w_branch, p, p, p)


def _proj_ln_kernel(alpha, ctx_len, tiles_per_batch, emit_h, a_ref, w_ref, x_ref, mod_ref, ln_ref, *rest):
    if emit_h:
        xo_ref, ho_ref, acc_ref = rest
    else:
        xo_ref, acc_ref = rest
    kk = pl.program_id(1)

    @pl.when(kk == 0)
    def _():
        acc_ref[...] = jnp.zeros_like(acc_ref)

    acc_ref[...] += _dot(a_ref[...], w_ref[...].astype(BF16))

    @pl.when(kk == pl.num_programs(1) - 1)
    def _():
        tm, d = acc_ref.shape
        rb = LN_ROW_BLOCK
        row0 = (pl.program_id(0) % tiles_per_batch) * tm

        def block(i, carry):
            sl = pl.ds(pl.multiple_of(i * rb, rb), rb)
            mod = mod_ref[jnp.where(row0 + i * rb < ctx_len, 1, 0)]
            sel = lambda j: mod[j:j + 1, :]
            y = alpha * x_ref[sl, :] + sel(0) * acc_ref[sl, :]
            mu = jnp.mean(y, axis=-1, keepdims=True)
            yc = y - mu
            var = jnp.mean(jnp.square(yc), axis=-1, keepdims=True)
            xn = yc * lax.rsqrt(var + LN_EPS) * ln_ref[0:1, :] + ln_ref[1:2, :]
            xo_ref[sl, :] = xn
            if emit_h:
                ho_ref[sl, :] = (xn * (1.0 + sel(1)) + sel(2)).astype(BF16)
            return carry

        lax.fori_loop(0, tm // rb, block, 0)


def _proj_ln(a, w, layer, x, mod_sel, ln_gb, alpha, ctx_len, tt, emit_h, name):
    m, k = a.shape
    d = w.shape[2]
    if w.dtype == BF16:
        tm, tk = _pick_tile(tt, (384, 256, 128)), k
    else:
        tm, tk = _pick_tile(tt, (768, 384, 256, 128)), _pick_tile(k, (512, 256, 128))
    assert tm % LN_ROW_BLOCK == 0 and ctx_len <= tm and ctx_len % LN_ROW_BLOCK == 0
    tpb = tt // tm
    out_shape = [jax.ShapeDtypeStruct((m, d), F32)]
    out_specs = [pl.BlockSpec((tm, d), lambda i, j: (i, 0))]
    if emit_h:
        out_shape.append(jax.ShapeDtypeStruct((m, d), BF16))
        out_specs.append(pl.BlockSpec((tm, d), lambda i, j: (i, 0)))
    res = pl.pallas_call(
        functools.partial(_proj_ln_kernel, alpha, ctx_len, tpb, emit_h),
        grid=(m // tm, k // tk),
        in_specs=[pl.BlockSpec((tm, tk), lambda i, j: (i, j)),
                  pl.BlockSpec((None, tk, d), lambda i, j: (layer, j, 0)),
                  pl.BlockSpec((tm, d), lambda i, j: (i, 0)),
                  pl.BlockSpec((None, 2, 3, d), lambda i, j: (i // tpb, 0, 0, 0)),
                  pl.BlockSpec((2, d), lambda i, j: (0, 0))],
        out_specs=out_specs,
        out_shape=out_shape,
        scratch_shapes=[pltpu.VMEM((tm, d), F32)],
        compiler_params=_params(("arbitrary", "arbitrary")),
        name=name,
    )(a, w, x, mod_sel, ln_gb)
    return (res[0], res[1]) if emit_h else (res[0], None)


def _rope_tables(seq, ctx_len):
    half = B_DK // 2
    freqs = ROPE_THETA ** (-jnp.arange(0, half, 2, dtype=F32) / half)
    pos = jnp.arange(seq)
    ang_r = (pos // GRID_W).astype(F32)[:, None] * freqs
    ang_c = (pos % GRID_W).astype(F32)[:, None] * freqs
    ang = jnp.concatenate([ang_r, ang_r, ang_c, ang_c], axis=-1)
    ang = jnp.concatenate([jnp.zeros((ctx_len, B_DK), F32), ang], axis=0)
    ang = jnp.concatenate([ang, ang], axis=-1)
    return jnp.cos(ang), jnp.sin(ang)


def kernel(x, c, ctx, c_ctx, w_ada, b_ada, w_in, hgrn_lb_logits, hgrn_norm_g, gla_w_gk2, gla_b_gk2, gla_norm_g,
           natten_rpb, w_branch, w_out, ln1_g, ln1_b, ln2_g, ln2_b, w_mlp1, w_mlp2):
    bsz, seq, d = x.shape
    ctx_len = ctx.shape[1]
    depth = w_ada.shape[0]
    tt = ctx_len + seq
    m_tok = bsz * tt
    alpha = (2 * depth) ** 0.25
    assert ctx_len % SCAN_CHUNK == 0 and seq % SCAN_CHUNK == 0 and seq % GRID_W == 0 and bsz + 1 <= 8

    seg = N_HEADS * A_DK
    gk_lo = 5 * seg + 2 * N_HEADS * B_DK + 2 * MIX_W
    gk_hi = gk_lo + 2 * GK_RANK
    n_cg = 3 * MIX_W + N_BRANCH * d
    assert gk_lo % 1024 == 0 and w_in.shape[2] == gk_hi + n_cg
    col_a = 0
    col_bq = 5 * seg // LANES
    col_bk = col_bq + N_HEADS * B_DK // LANES
    col_bv = col_bk + N_HEADS * B_DK // LANES
    col_bg = col_bv + MIX_W // LANES
    col_cq = 0
    col_ck = col_cq + MIX_W // LANES
    col_cv = col_ck + MIX_W // LANES
    col_gate = col_cv + MIX_W // LANES

    c_rows = jnp.zeros((8, d), F32).at[:bsz].set(c).at[bsz].set(c_ctx)
    mod = _ada_all_layers(c_rows, w_ada, b_ada).reshape(depth, 8, 6, d)

    def mod_sel(l, idx):
        lat = mod[l, :bsz][:, idx, :]
        cx = jnp.broadcast_to(mod[l, bsz][idx, :], lat.shape)
        return jnp.stack([lat, cx], axis=1)

    lb_p = jax.nn.softmax(hgrn_lb_logits.astype(F32), axis=0)
    lb_cum = jnp.cumsum(lb_p, axis=0)
    lower_bounds = jnp.concatenate([jnp.zeros_like(lb_cum[:1]), lb_cum[:-1]], axis=0)

    cos, sin = _rope_tables(seq, ctx_len)
    w_out_bf = w_out.astype(BF16)

    xs = jnp.concatenate([ctx, x], axis=1)
    h = _modulate(xs, mod_sel(0, np.array([1, 0])), ctx_len).reshape(m_tok, d)
    xs = xs.reshape(m_tok, d)

    w_ab = w_in[:, :, :gk_lo].astype(BF16)
    w_cg = w_in[:, :, gk_lo:].astype(BF16)
    w_gk = jnp.pad(w_in[:, :, gk_lo:gk_hi], ((0, 0), (0, 0), (0, LANES - 2 * GK_RANK))).astype(BF16)
    w_branch_bf = w_branch.astype(BF16)

    for l in range(depth):
        p = _matmul(h, w_ab, l, name="in_proj_ab").reshape(bsz, tt, gk_lo)
        p_cg = _matmul_shifted(h, w_cg, l, gk_hi - gk_lo, n_cg, "in_proj_cg").reshape(bsz, tt, n_cg)
        p_gk = _matmul(h, w_gk, l, name="in_proj_gk").reshape(bsz, tt, LANES)

        n_pairs = N_HEADS * B_DK // LANES
        wg = jnp.zeros((2, n_pairs, LANES, LANES), F32)
        w2 = gla_w_gk2[l].reshape(2, GK_RANK, n_pairs, LANES).transpose(0, 2, 1, 3)
        wg = wg.at[0, :, :GK_RANK].set(w2[0]).at[1, :, GK_RANK:2 * GK_RANK].set(w2[1]).astype(BF16)
        bg = gla_b_gk2[l].reshape(2, n_pairs, 1, LANES)

        y_a = _hgrn_mixer(p, lower_bounds[l], hgrn_norm_g[l], ctx_len, col_a)
        y_b = _gla_mixer(p, p_gk, wg, bg, cos, sin, gla_norm_g[l], ctx_len, col_bq, col_bk, col_bv, col_bg)
        y_c = _natten_mixer(p_cg, natten_rpb[l], ctx_len, col_cq, col_ck, col_cv)

        merged = _merge(y_a.reshape(m_tok, MIX_W), y_b.reshape(m_tok, MIX_W), y_c.reshape(m_tok, MIX_W),
                        w_branch_bf, l, p_cg.reshape(m_tok, n_cg), col_gate, d)
        xs, h2 = _proj_ln(merged, w_out_bf, l, xs, mod_sel(l, np.array([2, 4, 3])),
                          jnp.stack([ln1_g[l], ln1_b[l]]), alpha, ctx_len, tt, True, "out_proj_ln")
        u = _matmul(h2, w_mlp1, l, sq_relu=True, name="mlp_up")
        last = l == depth - 1
        if last:
            nxt = jnp.zeros((bsz, 2, 3, d), F32).at[:, :, 0].set(mod_sel(l, np.array([5]))[:, :, 0])
        else:
            nxt = jnp.concatenate([mod_sel(l, np.array([5])), mod_sel(l + 1, np.array([1, 0]))], axis=2)
        xs, h = _proj_ln(u, w_mlp2, l, xs, nxt, jnp.stack([ln2_g[l], ln2_b[l]]),
                         alpha, ctx_len, tt, not last, "mlp_down_ln")

    return xs.reshape(bsz, tt, d)[:, ctx_len:, :]
```

```python
import functools

import jax
import jax.numpy as jnp
import numpy as np
from jax import lax
from jax.experimental import pallas as pl
from jax.experimental.pallas import tpu as pltpu

GRID_W = 64
N_BRANCH = 3
MIX_W = 1024
N_HEADS = 8
A_DK = 128
B_DK = 64
HEAD_DV = MIX_W // N_HEADS
GK_RANK = 16
GATE_LOGIT_NORMALIZER = 16.0
KR_MAX = 8
KC = 16
ROPE_THETA = 10000.0
RMS_EPS = 1e-6
LN_EPS = 1e-5
MASK_VALUE = -1e30
LOG2E = 1.4426950408889634

LANES = 128
SCAN_CHUNK = 128
SCAN_UNROLL = 2
NATTEN_ROWS_PER_TRIP = 8
LN_ROW_BLOCK = 128
VMEM_LIMIT_BYTES = 56 * 1024 * 1024

F32 = jnp.float32
BF16 = jnp.bfloat16


def _pick_tile(n, prefs):
    for t in prefs:
        if n % t == 0:
            return t
    return n


def _params(sem):
    return pltpu.CompilerParams(dimension_semantics=sem, vmem_limit_bytes=VMEM_LIMIT_BYTES)


def _dot(a, b):
    return jnp.dot(a, b, preferred_element_type=F32)


def _dot_nt(a, b):
    return lax.dot_general(a, b, (((1,), (1,)), ((), ())), preferred_element_type=F32)


def _ada_kernel(c_ref, w_ref, b_ref, o_ref):
    c = c_ref[...]
    s = (c * jax.nn.sigmoid(c)).astype(BF16)
    o_ref[...] = _dot(s, w_ref[...].astype(BF16)) + b_ref[...]


def _ada_all_layers(c_rows, w_ada, b_ada):
    depth, d, n = w_ada.shape
    tn = _pick_tile(n, (1024, 512, 256, 128))
    return pl.pallas_call(
        _ada_kernel,
        grid=(depth, n // tn),
        in_specs=[pl.BlockSpec((8, d), lambda l, j: (0, 0)),
                  pl.BlockSpec((None, d, tn), lambda l, j: (l, 0, j)),
                  pl.BlockSpec((None, 1, tn), lambda l, j: (l, 0, j))],
        out_specs=pl.BlockSpec((None, 8, tn), lambda l, j: (l, 0, j)),
        out_shape=jax.ShapeDtypeStruct((depth, 8, n), F32),
        compiler_params=_params(("arbitrary", "arbitrary")),
        name="ada_mod",
    )(c_rows, w_ada, b_ada.reshape(depth, 1, n))


def _modulate_kernel(ctx_len, x_ref, m_ref, h_ref):
    x = x_ref[...]
    rows = lax.broadcasted_iota(jnp.int32, x.shape, 0) + pl.program_id(1) * x.shape[0]
    is_ctx = rows < ctx_len
    sc = jnp.where(is_ctx, m_ref[1, 0:1, :], m_ref[0, 0:1, :])
    sh = jnp.where(is_ctx, m_ref[1, 1:2, :], m_ref[0, 1:2, :])
    h_ref[...] = (x * (1.0 + sc) + sh).astype(BF16)


def _modulate(x, mod_sel, ctx_len):
    b, tt, d = x.shape
    tm = _pick_tile(tt, (768, 384, 256, 128))
    return pl.pallas_call(
        functools.partial(_modulate_kernel, ctx_len),
        grid=(b, tt // tm),
        in_specs=[pl.BlockSpec((None, tm, d), lambda i, j: (i, j, 0)),
                  pl.BlockSpec((None, 2, 2, d), lambda i, j: (i, 0, 0, 0))],
        out_specs=pl.BlockSpec((None, tm, d), lambda i, j: (i, j, 0)),
        out_shape=jax.ShapeDtypeStruct((b, tt, d), BF16),
        compiler_params=_params(("arbitrary", "arbitrary")),
        name="modulate",
    )(x, mod_sel)


def _mm_kernel(sq_relu, a_ref, w_ref, o_ref, *scratch):
    if scratch:
        wb_ref, = scratch

        @pl.when(pl.program_id(1) == 0)
        def _():
            wb_ref[...] = w_ref[...].astype(BF16)
    else:
        wb_ref = w_ref

    r = _dot(a_ref[...], wb_ref[...])
    if sq_relu:
        r = jnp.square(jnp.maximum(r, 0.0))
    o_ref[...] = r.astype(o_ref.dtype)


def _matmul(a, w, layer, col0=0, n=None, sq_relu=False, name="matmul"):
    m, k = a.shape
    n = w.shape[2] if n is None else n
    tm = _pick_tile(m, (1152, 1024, 512, 256, 128))
    tn = _pick_tile(n, (1024, 512, 256, 128))
    assert col0 % tn == 0
    return pl.pallas_call(
        functools.partial(_mm_kernel, sq_relu),
        grid=(n // tn, m // tm),
        in_specs=[pl.BlockSpec((tm, k), lambda j, i: (i, 0)),
                  pl.BlockSpec((None, k, tn), lambda j, i: (layer, 0, col0 // tn + j))],
        out_specs=pl.BlockSpec((tm, tn), lambda j, i: (i, j)),
        out_shape=jax.ShapeDtypeStruct((m, n), BF16),
        scratch_shapes=[] if w.dtype == BF16 else [pltpu.VMEM((k, tn), BF16)],
        compiler_params=_params(("arbitrary", "arbitrary")),
        name=name,
    )(a, w)


def _mm_shifted_kernel(shift, a_ref, wa_ref, wb_ref, o_ref, w_s):
    @pl.when(pl.program_id(1) == 0)
    def _():
        tn = wa_ref.shape[1]
        both = jnp.concatenate([wa_ref[...], wb_ref[...]], axis=1).astype(F32)
        w_s[...] = both[:, shift:shift + tn].astype(BF16)

    o_ref[...] = _dot(a_ref[...], w_s[...]).astype(o_ref.dtype)


def _matmul_shifted(a, w, layer, col0, n, name):
    m, k = a.shape
    tm = _pick_tile(m, (1152, 1024, 512, 256, 128))
    tn = _pick_tile(n, (1024, 512, 256, 128))
    shift, base = col0 % LANES, col0 - col0 % LANES
    assert shift > 0 and base % tn == 0 and w.shape[2] >= col0 + n
    return pl.pallas_call(
        functools.partial(_mm_shifted_kernel, shift),
        grid=(n // tn, m // tm),
        in_specs=[pl.BlockSpec((tm, k), lambda j, i: (i, 0)),
                  pl.BlockSpec((None, k, tn), lambda j, i: (layer, 0, base // tn + j)),
                  pl.BlockSpec((None, k, LANES), lambda j, i: (layer, 0, (base + (j + 1) * tn) // LANES))],
        out_specs=pl.BlockSpec((tm, tn), lambda j, i: (i, j)),
        out_shape=jax.ShapeDtypeStruct((m, n), BF16),
        scratch_shapes=[pltpu.VMEM((k, tn), BF16)],
        compiler_params=_params(("arbitrary", "arbitrary")),
        name=name,
    )(a, w, w)


def _level_operand(q, k, la, bc, m, rev):
    c, dk = q.shape
    if m >= 8:
        pieces = []
        for jb in range(c // m):
            rows = slice(jb * m, (jb + 1) * m)
            r = (jb // 2) * 2 * m + (m if rev else m - 1)
            cvec = bc[r:r + 1, :]
            if (jb % 2 == 1) != rev:
                pieces.append(q[rows, :] * jnp.exp2(bc[rows, :] - cvec))
            else:
                pieces.append(k[rows, :] * jnp.exp2(cvec - bc[rows, :]))
        return jnp.concatenate(pieces, axis=0)
    shape3 = (c // 8, 8, dk)
    sub = lax.broadcasted_iota(jnp.int32, (1, 8, dk), 1)
    upper = (sub & m) != 0
    q_role = jnp.logical_not(upper) if rev else upper
    q3, k3, bc3 = q.reshape(shape3), k.reshape(shape3), bc.reshape(shape3)
    if m == 1:
        e2 = jnp.where(q_role, la.reshape(shape3), 0.0)
    else:
        if m == 4:
            r = 4 if rev else 3
            c3 = bc3[:, r:r + 1, :]
        else:
            r0, r1 = (2, 6) if rev else (1, 5)
            c3 = jnp.where(sub < 4, bc3[:, r0:r0 + 1, :], bc3[:, r1:r1 + 1, :])
        e2 = (bc3 - c3) * jnp.where(q_role, 1.0, -1.0)
    return (jnp.where(q_role, q3, k3) * jnp.exp2(e2)).reshape(c, dk)


def _level_matrix(c, rev):
    t = lax.broadcasted_iota(jnp.int32, (c, c), 0)
    s = lax.broadcasted_iota(jnp.int32, (c, c), 1)
    x = t ^ s
    lvl = jnp.full((c, c), -1, jnp.int32)
    b = 1
    while b < c:
        lvl = lvl + (x >= b).astype(jnp.int32)
        b *= 2
    return jnp.where((t < s) if rev else (t > s), lvl, -1)


def _scan_both_directions(prep_chunk, q_s, kf_s, kb_s, laf_s, lab_s, v_ref, o_s, st_s, bc_s, vt_s, sc_s, n_ctx_chunks):
    c = SCAN_CHUNK
    n_chunks = q_s.shape[0] // c
    nh = st_s.shape[1]
    dv = v_ref.shape[1] // nh
    ti = lax.broadcasted_iota(jnp.int32, (c, c), 0)
    si = lax.broadcasted_iota(jnp.int32, (c, c), 1)
    lane = lax.broadcasted_iota(jnp.int32, (c, LANES), 1) // (LANES // nh)
    head_lanes = [(lane == h, (lane == h).astype(BF16)) for h in range(nh)]
    k_refs, la_refs = (kf_s, kb_s), (laf_s, lab_s)
    tris = ((si <= ti).astype(BF16), (si >= ti).astype(BF16))
    lvls = (_level_matrix(c, False), _level_matrix(c, True))

    def cumulate(i, carry):
        sl = pl.ds(pl.multiple_of(i * c, c), c)
        las = prep_chunk(sl)
        for h in range(nh):
            vt_s[i, h * dv:(h + 1) * dv, :] = v_ref[sl, h * dv:(h + 1) * dv].T
        for d in range(2):
            la_hi = las[d].astype(BF16)
            la_lo = (las[d] - la_hi.astype(F32)).astype(BF16)
            bb = _dot(tris[d], jnp.concatenate([la_hi, la_lo], axis=1))
            bc_s[d, sl, :] = (bb[:, :LANES] + bb[:, LANES:]) * LOG2E
            la_refs[d][sl, :] = las[d] * LOG2E
        return carry

    lax.fori_loop(0, n_chunks, cumulate, 0, unroll=2)

    st_s[...] = jnp.zeros_like(st_s)
    nu = SCAN_UNROLL
    n_groups = n_chunks // nu
    assert n_chunks % nu == 0 and n_groups >= 2
    streams = [(d, u) for u in range(nu) for d in range(2)]
    slot = {(su, h): (su[0] * nu + su[1]) * nh + h for su in streams for h in range(nh)}

    def group_rows(g):
        seq = [g * nu + u for u in range(nu)]
        cis = [seq, [jnp.where(j < n_ctx_chunks, n_ctx_chunks - 1 - j, n_chunks - 1 + n_ctx_chunks - j) for j in seq]]
        return cis, {(d, u): pl.ds(pl.multiple_of(cis[d][u] * c, c), c) for d, u in streams}

    def state_phase(g):
        cis, sl = group_rows(g)
        q = {su: q_s[sl[su], :] for su in streams}
        k = {(d, u): k_refs[d][sl[d, u], :] for d, u in streams}
        la = {(d, u): la_refs[d][sl[d, u], :] for d, u in streams}
        bc = {(d, u): bc_s[d, sl[d, u], :] for d, u in streams}
        st = [[st_s[d, h] for h in range(nh)] for d in range(2)]
        inter = {}
        for d, u in streams:
            su = (d, u)
            qg = (q[su] * jnp.exp2(bc[su])).astype(BF16)
            tot = bc[su][0:1, :] if d else bc[su][c - 1:c, :]
            kg = (k[su] * jnp.exp2(tot - bc[su])).astype(BF16)
            dec = jnp.exp2(tot)
            for h in range(nh):
                qg_h = qg if nh == 1 else qg * head_lanes[h][1]
                inter[su, h] = _dot_nt(qg_h, st[d][h].astype(BF16))
                st[d][h] = st[d][h] * dec + _dot(vt_s[cis[d][u], h * dv:(h + 1) * dv, :], kg)
        for d in range(2):
            for h in range(nh):
                st_s[d, h] = st[d][h]
        return sl, q, k, la, bc, inter

    def score_matmuls(g):
        _, sl = group_rows(g)
        return sl, {(su, h): _dot(sc_s[slot[su, h]], v_ref[sl[su], h * dv:(h + 1) * dv])
                    for su in streams for h in range(nh)}

    def level_phase(sl, q, k, la, bc, inter):
        scores = {key: jnp.zeros((c, c), F32) for key in slot}
        m, level = c // 2, (c // 2).bit_length() - 1
        while m >= 1:
            for d, u in streams:
                su = (d, u)
                x = _level_operand(q[su], k[su], la[su], bc[su], m, d == 1).astype(BF16)
                for h in range(nh):
                    xh = x if nh == 1 else x * head_lanes[h][1]
                    scores[su, h] = jnp.where(lvls[d] == level, _dot_nt(xh, x), scores[su, h])
            m, level = m // 2, level - 1
        for d, u in streams:
            su = (d, u)
            qk = q[su] * k[su]
            for h in range(nh):
                sc_s[slot[su, h]] = scores[su, h].astype(BF16)
                v = v_ref[sl[su], h * dv:(h + 1) * dv].astype(F32)
                qk_h = qk if nh == 1 else jnp.where(head_lanes[h][0], qk, 0.0)
                o_s[d, sl[su], h * dv:(h + 1) * dv] = inter[su, h] + jnp.sum(qk_h, axis=1, keepdims=True) * v

    def add_scores_part(sl, outs):
        for (su, h), o in outs.items():
            cols = slice(h * dv, (h + 1) * dv)
            o_s[su[0], sl[su], cols] = o_s[su[0], sl[su], cols] + o

    level_phase(*state_phase(0))

    def body(g, carry):
        cur = state_phase(g)
        sl_prev, outs = score_matmuls(g - 1)
        level_phase(*cur)
        add_scores_part(sl_prev, outs)
        return carry

    lax.fori_loop(1, n_groups, body, 0)
    add_scores_part(*score_matmuls(n_groups - 1))


def _scan_scratch(tt, nh):
    n_chunks = tt // SCAN_CHUNK
    return [pltpu.VMEM((2, tt, nh * HEAD_DV), F32), pltpu.VMEM((2, nh, HEAD_DV, LANES), F32),
            pltpu.VMEM((2, tt, LANES), F32), pltpu.VMEM((n_chunks, nh * HEAD_DV, SCAN_CHUNK), BF16),
            pltpu.VMEM((2 * SCAN_UNROLL * nh, SCAN_CHUNK, SCAN_CHUNK), BF16)]


def _rms_gate(o, gain, gate):
    o = o * lax.rsqrt(jnp.mean(jnp.square(o), axis=-1, keepdims=True) + RMS_EPS) * gain
    return o * (gate * jax.nn.sigmoid(gate))


def _hgrn_kernel(n_ctx_chunks, q_ref, i_ref, g_ref, ff_ref, fb_ref, lb_ref, gain_ref, y_ref,
                 q_s, kf_s, kb_s, laf_s, lab_s, *scan_s):
    c = SCAN_CHUNK
    n_chunks = q_s.shape[0] // c
    o_s = scan_s[0]
    lb_f = lb_ref[0:1, :]
    lb_b = lb_ref[1:2, :]

    def prep_chunk(sl):
        q = q_ref[sl, :].astype(F32)
        q_s[sl, :] = q * jax.nn.sigmoid(q) * (A_DK ** -0.5)
        las = []
        for f_ref, lb, k_s in ((ff_ref, lb_f, kf_s), (fb_ref, lb_b, kb_s)):
            f = f_ref[sl, :].astype(F32)
            k_s[sl, :] = (1.0 - lb) * jax.nn.sigmoid(-f)
            las.append(jnp.log(lb + (1.0 - lb) * jax.nn.sigmoid(f)))
        return las

    _scan_both_directions(prep_chunk, q_s, kf_s, kb_s, laf_s, lab_s, i_ref, *scan_s, n_ctx_chunks)

    def fin(i, carry):
        sl = pl.ds(pl.multiple_of(i * c, c), c)
        o = o_s[0, sl, :] + o_s[1, sl, :]
        y_ref[sl, :] = _rms_gate(o, gain_ref[...], g_ref[sl, :].astype(F32)).astype(y_ref.dtype)
        return carry

    lax.fori_loop(0, n_chunks, fin, 0, unroll=2)


def _hgrn_mixer(p, lb, gain, ctx_len, col0):
    b, tt, _ = p.shape
    blk = lambda seg: pl.BlockSpec((None, tt, LANES), lambda i, h, s=seg: (i, 0, col0 + s * N_HEADS + h))
    seq = lambda: pltpu.VMEM((tt, LANES), F32)
    return pl.pallas_call(
        functools.partial(_hgrn_kernel, ctx_len // SCAN_CHUNK),
        grid=(b, N_HEADS),
        in_specs=[blk(0), blk(1), blk(2), blk(3), blk(4),
                  pl.BlockSpec((2, LANES), lambda i, h: (0, h)),
                  pl.BlockSpec((1, LANES), lambda i, h: (0, 0))],
        out_specs=pl.BlockSpec((None, tt, LANES), lambda i, h: (i, 0, h)),
        out_shape=jax.ShapeDtypeStruct((b, tt, MIX_W), BF16),
        scratch_shapes=[seq(), seq(), seq(), seq(), seq()] + _scan_scratch(tt, 1),
        compiler_params=_params(("arbitrary", "arbitrary")),
        name="hgrn2_mixer",
    )(p, p, p, p, p, lb, gain.reshape(1, HEAD_DV))


def _rope_rotate(x):
    n = x.shape[-1]
    lane = lax.broadcasted_iota(jnp.int32, x.shape, x.ndim - 1)
    first = (lane % 32) < 16
    return jnp.where(first, -pltpu.roll(x, n - 16, x.ndim - 1), pltpu.roll(x, 16, x.ndim - 1))


def _gla_kernel(n_ctx_chunks, q_ref, k_ref, v_ref, g_ref, gk_ref, wg_ref, bg_ref, cos_ref, sin_ref,
                gain_ref, y_ref, q_s, k_s, laf_s, lab_s, *scan_s):
    c = SCAN_CHUNK
    n_chunks = q_s.shape[0] // c
    nh = LANES // B_DK
    o_s = scan_s[0]

    def prep_chunk(sl):
        cos = cos_ref[sl, :]
        sin = sin_ref[sl, :]
        q = q_ref[sl, :].astype(F32)
        k = k_ref[sl, :].astype(F32)
        q_s[sl, :] = (q * cos + _rope_rotate(q) * sin) * (B_DK ** -0.5)
        k_s[sl, :] = k * cos + _rope_rotate(k) * sin
        gk = gk_ref[sl, :]
        return [jax.nn.log_sigmoid(_dot(gk, wg_ref[d]) + bg_ref[d]) / GATE_LOGIT_NORMALIZER for d in range(2)]

    _scan_both_directions(prep_chunk, q_s, k_s, k_s, laf_s, lab_s, v_ref, *scan_s, n_ctx_chunks)

    def fin(i, carry):
        sl = pl.ds(pl.multiple_of(i * c, c), c)
        for h in range(nh):
            cols = slice(h * HEAD_DV, (h + 1) * HEAD_DV)
            o = o_s[0, sl, cols] + o_s[1, sl, cols]
            y_ref[sl, cols] = _rms_gate(o, gain_ref[...], g_ref[sl, cols].astype(F32)).astype(y_ref.dtype)
        return carry

    lax.fori_loop(0, n_chunks, fin, 0, unroll=2)


def _gla_mixer(p, p_gk, wg, bg, cos, sin, gain, ctx_len, col_q, col_k, col_v, col_g):
    b, tt, _ = p.shape
    nh = LANES // B_DK
    wide = nh * HEAD_DV
    pair = lambda c0: pl.BlockSpec((None, tt, LANES), lambda i, h: (i, 0, c0 + h))
    pair_v = lambda c0: pl.BlockSpec((None, tt, wide), lambda i, h: (i, 0, c0 * LANES // wide + h))
    seq = lambda: pltpu.VMEM((tt, LANES), F32)
    assert (col_v * LANES) % wide == 0 and (col_g * LANES) % wide == 0
    return pl.pallas_call(
        functools.partial(_gla_kernel, ctx_len // SCAN_CHUNK),
        grid=(b, N_HEADS // nh),
        in_specs=[pair(col_q), pair(col_k), pair_v(col_v), pair_v(col_g),
                  pl.BlockSpec((None, tt, LANES), lambda i, h: (i, 0, 0)),
                  pl.BlockSpec((2, None, LANES, LANES), lambda i, h: (0, h, 0, 0)),
                  pl.BlockSpec((2, None, 1, LANES), lambda i, h: (0, h, 0, 0)),
                  pl.BlockSpec((tt, LANES), lambda i, h: (0, 0)),
                  pl.BlockSpec((tt, LANES), lambda i, h: (0, 0)),
                  pl.BlockSpec((1, LANES), lambda i, h: (0, 0))],
        out_specs=pl.BlockSpec((None, tt, wide), lambda i, h: (i, 0, h)),
        out_shape=jax.ShapeDtypeStruct((b, tt, MIX_W), BF16),
        scratch_shapes=[seq(), seq(), seq(), seq()] + _scan_scratch(tt, nh),
        compiler_params=_params(("arbitrary", "arbitrary")),
        name="gla_mixer",
    )(p, p, p, p, p_gk, wg, bg, cos, sin, gain.reshape(1, HEAD_DV))


def _natten_kernel(ctx_len, rows, kr, q_ref, k_ref, v_ref, rpb_ref, y_ref, bias_ref):
    scale = HEAD_DV ** -0.5
    kc = k_ref[0:ctx_len, :]
    vc = v_ref[0:ctx_len, :]

    qi = lax.broadcasted_iota(jnp.int32, (GRID_W, LANES), 0)
    kj = lax.broadcasted_iota(jnp.int32, (GRID_W, LANES), 1) % GRID_W
    col_start = jnp.clip(qi - KC // 2, 0, GRID_W - KC)
    col_ok = (kj >= col_start) & (kj < col_start + KC)
    for v in range(kr):
        for blk in range(kr * GRID_W // LANES):
            d = v + blk * (LANES // GRID_W)
            src = jnp.broadcast_to(rpb_ref[d:d + 1, :], (GRID_W, LANES))
            rot = pltpu.roll(src, LANES - (KC - 1), 1, stride=1, stride_axis=0)
            bias_ref[v, :, blk * LANES:(blk + 1) * LANES] = jnp.where(col_ok, rot, MASK_VALUE)

    s = _dot_nt(q_ref[0:ctx_len, :], kc) * scale
    p = jnp.exp(s - jnp.max(s, axis=-1, keepdims=True))
    o = _dot(p.astype(BF16), vc) / jnp.sum(p, axis=-1, keepdims=True)
    y_ref[0:ctx_len, :] = o.astype(y_ref.dtype)

    nr = NATTEN_ROWS_PER_TRIP
    assert rows % nr == 0

    def body(i, carry):
        q_sl, b_sl, s_b, s_c = [], [], [], []
        for u in range(nr):
            r = i * nr + u
            start = jnp.clip(r - kr // 2, 0, rows - kr)
            q_sl.append(pl.ds(pl.multiple_of(ctx_len + r * GRID_W, GRID_W), GRID_W))
            b_sl.append(pl.ds(pl.multiple_of(ctx_len + start * GRID_W, GRID_W), kr * GRID_W))
            q = q_ref[q_sl[u], :]
            s_b.append(_dot_nt(q, k_ref[b_sl[u], :]) * scale + bias_ref[start - r + (KR_MAX - 1)])
            s_c.append(_dot_nt(q, kc) * scale)
        for u in range(nr):
            mx = jnp.maximum(jnp.max(s_b[u], axis=-1, keepdims=True), jnp.max(s_c[u], axis=-1, keepdims=True))
            p_b = jnp.exp(s_b[u] - mx)
            p_c = jnp.exp(s_c[u] - mx)
            den = jnp.sum(p_b, axis=-1, keepdims=True) + jnp.sum(p_c, axis=-1, keepdims=True)
            o = (_dot(p_b.astype(BF16), v_ref[b_sl[u], :]) + _dot(p_c.astype(BF16), vc)) / den
            y_ref[q_sl[u], :] = o.astype(y_ref.dtype)
        return carry

    lax.fori_loop(0, rows // nr, body, 0)


def _natten_mixer(p, rpb, ctx_len, col_q, col_k, col_v):
    b, tt, _ = p.shape
    rows = (tt - ctx_len) // GRID_W
    kr = KR_MAX
    assert rows >= KR_MAX and rows % 2 == 0 and 2 * GRID_W == LANES and 2 * KC - 1 <= GRID_W
    padded = jnp.pad(rpb.astype(F32), ((0, 0), (0, 0), (0, GRID_W - (2 * KC - 1))))
    pairs = jnp.concatenate([padded[:, :-1], padded[:, 1:]], axis=-1)
    head = lambda c0: pl.BlockSpec((None, tt, LANES), lambda i, h: (i, 0, c0 + h))
    return pl.pallas_call(
        functools.partial(_natten_kernel, ctx_len, rows, kr),
        grid=(b, N_HEADS),
        in_specs=[head(col_q), head(col_k), head(col_v),
                  pl.BlockSpec((None, 2 * KR_MAX - 2, LANES), lambda i, h: (h, 0, 0))],
        out_specs=pl.BlockSpec((None, tt, LANES), lambda i, h: (i, 0, h)),
        out_shape=jax.ShapeDtypeStruct((b, tt, MIX_W), BF16),
        scratch_shapes=[pltpu.VMEM((kr, GRID_W, kr * GRID_W), F32)],
        compiler_params=_params(("arbitrary", "arbitrary")),
        name="natten_mixer",
    )(p, p, p, pairs)


def _merge_kernel(ya_ref, yb_ref, yc_ref, w_ref, ga_ref, gb_ref, gc_ref, o_ref):
    acc = None
    for n, (y_ref, g_ref) in enumerate(((ya_ref, ga_ref), (yb_ref, gb_ref), (yc_ref, gc_ref))):
        z = jax.nn.sigmoid(g_ref[...].astype(F32)) * _dot(y_ref[...], w_ref[n])
        acc = z if acc is None else acc + z
    o_ref[...] = acc.astype(o_ref.dtype)


def _merge(ya, yb, yc, w_branch, layer, p, col_gate, d_model):
    m = ya.shape[0]
    tm = _pick_tile(m, (512, 256, 128))
    tn = _pick_tile(d_model, (1024, 512, 256, 128))
    nj = d_model // tn
    assert (col_gate * LANES) % tn == 0
    yspec = pl.BlockSpec((tm, MIX_W), lambda j, i: (i, 0))
    gspec = lambda n: pl.BlockSpec((tm, tn), lambda j, i, n=n: (i, col_gate * LANES // tn + n * nj + j))
    return pl.pallas_call(
        _merge_kernel,
        grid=(nj, m // tm),
        in_specs=[yspec, yspec, yspec,
                  pl.BlockSpec((None, N_BRANCH, MIX_W, tn), lambda j, i: (layer, 0, 0, j)),
                  gspec(0), gspec(1), gspec(2)],
        out_specs=pl.BlockSpec((tm, tn), lambda j, i: (i, j)),
        out_shape=jax.ShapeDtypeStruct((m, d_model), BF16),
        compiler_params=_params(("arbitrary", "arbitrary")),
        name="branch_merge",
    )(ya, yb, yc, w_branch, p, p, p)


def _proj_ln_kernel(alpha, ctx_len, tiles_per_batch, emit_h, a_ref, w_ref, x_ref, mod_ref, ln_ref, *rest):
    if emit_h:
        xo_ref, ho_ref, acc_ref = rest
    else:
        xo_ref, acc_ref = rest
    kk = pl.program_id(1)

    @pl.when(kk == 0)
    def _():
        acc_ref[...] = jnp.zeros_like(acc_ref)

    acc_ref[...] += _dot(a_ref[...], w_ref[...].astype(BF16))

    @pl.when(kk == pl.num_programs(1) - 1)
    def _():
        tm, d = acc_ref.shape
        rb = LN_ROW_BLOCK
        row0 = (pl.program_id(0) % tiles_per_batch) * tm

        def block(i, carry):
            sl = pl.ds(pl.multiple_of(i * rb, rb), rb)
            mod = mod_ref[jnp.where(row0 + i * rb < ctx_len, 1, 0)]
            sel = lambda j: mod[j:j + 1, :]
            y = alpha * x_ref[sl, :] + sel(0) * acc_ref[sl, :]
            mu = jnp.mean(y, axis=-1, keepdims=True)
            yc = y - mu
            var = jnp.mean(jnp.square(yc), axis=-1, keepdims=True)
            xn = yc * lax.rsqrt(var + LN_EPS) * ln_ref[0:1, :] + ln_ref[1:2, :]
            xo_ref[sl, :] = xn
            if emit_h:
                ho_ref[sl, :] = (xn * (1.0 + sel(1)) + sel(2)).astype(BF16)
            return carry

        lax.fori_loop(0, tm // rb, block, 0)


def _proj_ln(a, w, layer, x, mod_sel, ln_gb, alpha, ctx_len, tt, emit_h, name):
    m, k = a.shape
    d = w.shape[2]
    if w.dtype == BF16:
        tm, tk = _pick_tile(tt, (384, 256, 128)), k
    else:
        tm, tk = _pick_tile(tt, (768, 384, 256, 128)), _pick_tile(k, (512, 256, 128))
    assert tm % LN_ROW_BLOCK == 0 and ctx_len <= tm and ctx_len % LN_ROW_BLOCK == 0
    tpb = tt // tm
    out_shape = [jax.ShapeDtypeStruct((m, d), F32)]
    out_specs = [pl.BlockSpec((tm, d), lambda i, j: (i, 0))]
    if emit_h:
        out_shape.append(jax.ShapeDtypeStruct((m, d), BF16))
        out_specs.append(pl.BlockSpec((tm, d), lambda i, j: (i, 0)))
    res = pl.pallas_call(
        functools.partial(_proj_ln_kernel, alpha, ctx_len, tpb, emit_h),
        grid=(m // tm, k // tk),
        in_specs=[pl.BlockSpec((tm, tk), lambda i, j: (i, j)),
                  pl.BlockSpec((None, tk, d), lambda i, j: (layer, j, 0)),
                  pl.BlockSpec((tm, d), lambda i, j: (i, 0)),
                  pl.BlockSpec((None, 2, 3, d), lambda i, j: (i // tpb, 0, 0, 0)),
                  pl.BlockSpec((2, d), lambda i, j: (0, 0))],
        out_specs=out_specs,
        out_shape=out_shape,
        scratch_shapes=[pltpu.VMEM((tm, d), F32)],
        compiler_params=_params(("arbitrary", "arbitrary")),
        name=name,
    )(a, w, x, mod_sel, ln_gb)
    return (res[0], res[1]) if emit_h else (res[0], None)


def _rope_tables(seq, ctx_len):
    half = B_DK // 2
    freqs = ROPE_THETA ** (-jnp.arange(0, half, 2, dtype=F32) / half)
    pos = jnp.arange(seq)
    ang_r = (pos // GRID_W).astype(F32)[:, None] * freqs
    ang_c = (pos % GRID_W).astype(F32)[:, None] * freqs
    ang = jnp.concatenate([ang_r, ang_r, ang_c, ang_c], axis=-1)
    ang = jnp.concatenate([jnp.zeros((ctx_len, B_DK), F32), ang], axis=0)
    ang = jnp.concatenate([ang, ang], axis=-1)
    return jnp.cos(ang), jnp.sin(ang)


def kernel(x, c, ctx, c_ctx, w_ada, b_ada, w_in, hgrn_lb_logits, hgrn_norm_g, gla_w_gk2, gla_b_gk2, gla_norm_g,
           natten_rpb, w_branch, w_out, ln1_g, ln1_b, ln2_g, ln2_b, w_mlp1, w_mlp2):
    bsz, seq, d = x.shape
    ctx_len = ctx.shape[1]
    depth = w_ada.shape[0]
    tt = ctx_len + seq
    m_tok = bsz * tt
    alpha = (2 * depth) ** 0.25
    assert ctx_len % SCAN_CHUNK == 0 and seq % SCAN_CHUNK == 0 and seq % GRID_W == 0 and bsz + 1 <= 8

    seg = N_HEADS * A_DK
    gk_lo = 5 * seg + 2 * N_HEADS * B_DK + 2 * MIX_W
    gk_hi = gk_lo + 2 * GK_RANK
    n_cg = 3 * MIX_W + N_BRANCH * d
    assert gk_lo % 1024 == 0 and w_in.shape[2] == gk_hi + n_cg
    col_a = 0
    col_bq = 5 * seg // LANES
    col_bk = col_bq + N_HEADS * B_DK // LANES
    col_bv = col_bk + N_HEADS * B_DK // LANES
    col_bg = col_bv + MIX_W // LANES
    col_cq = 0
    col_ck = col_cq + MIX_W // LANES
    col_cv = col_ck + MIX_W // LANES
    col_gate = col_cv + MIX_W // LANES

    c_rows = jnp.zeros((8, d), F32).at[:bsz].set(c).at[bsz].set(c_ctx)
    mod = _ada_all_layers(c_rows, w_ada, b_ada).reshape(depth, 8, 6, d)

    def mod_sel(l, idx):
        lat = mod[l, :bsz][:, idx, :]
        cx = jnp.broadcast_to(mod[l, bsz][idx, :], lat.shape)
        return jnp.stack([lat, cx], axis=1)

    lb_p = jax.nn.softmax(hgrn_lb_logits.astype(F32), axis=0)
    lb_cum = jnp.cumsum(lb_p, axis=0)
    lower_bounds = jnp.concatenate([jnp.zeros_like(lb_cum[:1]), lb_cum[:-1]], axis=0)

    cos, sin = _rope_tables(seq, ctx_len)
    w_out_bf = w_out.astype(BF16)

    xs = jnp.concatenate([ctx, x], axis=1)
    h = _modulate(xs, mod_sel(0, np.array([1, 0])), ctx_len).reshape(m_tok, d)
    xs = xs.reshape(m_tok, d)

    w_in_bf = w_in.astype(BF16)
    w_branch_bf = w_branch.astype(BF16)

    for l in range(depth):
        p = _matmul(h, w_in_bf, l, 0, gk_lo, name="in_proj_ab").reshape(bsz, tt, gk_lo)
        p_gk = _matmul(h, w_in_bf, l, gk_lo, LANES, name="in_proj_gk").reshape(bsz, tt, LANES)
        p_cg = _matmul_shifted(h, w_in_bf, l, gk_hi, n_cg, "in_proj_cg").reshape(bsz, tt, n_cg)

        n_pairs = N_HEADS * B_DK // LANES
        wg = jnp.zeros((2, n_pairs, LANES, LANES), F32)
        w2 = gla_w_gk2[l].reshape(2, GK_RANK, n_pairs, LANES).transpose(0, 2, 1, 3)
        wg = wg.at[0, :, :GK_RANK].set(w2[0]).at[1, :, GK_RANK:2 * GK_RANK].set(w2[1]).astype(BF16)
        bg = gla_b_gk2[l].reshape(2, n_pairs, 1, LANES)

        y_a = _hgrn_mixer(p, lower_bounds[l], hgrn_norm_g[l], ctx_len, col_a)
        y_b = _gla_mixer(p, p_gk, wg, bg, cos, sin, gla_norm_g[l], ctx_len, col_bq, col_bk, col_bv, col_bg)
        y_c = _natten_mixer(p_cg, natten_rpb[l], ctx_len, col_cq, col_ck, col_cv)

        merged = _merge(y_a.reshape(m_tok, MIX_W), y_b.reshape(m_tok, MIX_W), y_c.reshape(m_tok, MIX_W),
                        w_branch_bf, l, p_cg.reshape(m_tok, n_cg), col_gate, d)
        xs, h2 = _proj_ln(merged, w_out_bf, l, xs, mod_sel(l, np.array([2, 4, 3])),
                          jnp.stack([ln1_g[l], ln1_b[l]]), alpha, ctx_len, tt, True, "out_proj_ln")
        u = _matmul(h2, w_mlp1, l, sq_relu=True, name="mlp_up")
        last = l == depth - 1
        if last:
            nxt = jnp.zeros((bsz, 2, 3, d), F32).at[:, :, 0].set(mod_sel(l, np.array([5]))[:, :, 0])
        else:
            nxt = jnp.concatenate([mod_sel(l, np.array([5])), mod_sel(l + 1, np.array([1, 0]))], axis=2)
        xs, h = _proj_ln(u, w_mlp2, l, xs, nxt, jnp.stack([ln2_g[l], ln2_b[l]]),
                         alpha, ctx_len, tt, not last, "mlp_down_ln")

    return xs.reshape(bsz, tt, d)[:, ctx_len:, :]
```

```python
import functools

import jax
import jax.numpy as jnp
import numpy as np
from jax import lax
from jax.experimental import pallas as pl
from jax.experimental.pallas import tpu as pltpu

GRID_W = 64
N_BRANCH = 3
MIX_W = 1024
N_HEADS = 8
A_DK = 128
B_DK = 64
HEAD_DV = MIX_W // N_HEADS
GK_RANK = 16
GATE_LOGIT_NORMALIZER = 16.0
KR_MAX = 8
KC = 16
ROPE_THETA = 10000.0
RMS_EPS = 1e-6
LN_EPS = 1e-5
MASK_VALUE = -1e30
LOG2E = 1.4426950408889634

LANES = 128
SCAN_CHUNK = 128
SCAN_UNROLL = 2
NATTEN_ROWS_PER_TRIP = 8
LN_ROW_BLOCK = 128
VMEM_LIMIT_BYTES = 56 * 1024 * 1024
RESIDENT_WEIGHT_BYTES = 16 * 1024 * 1024

F32 = jnp.float32
BF16 = jnp.bfloat16


def _pick_tile(n, prefs):
    for t in prefs:
        if n % t == 0:
            return t
    return n


def _params(sem):
    return pltpu.CompilerParams(dimension_semantics=sem, vmem_limit_bytes=VMEM_LIMIT_BYTES)


def _dot(a, b):
    return jnp.dot(a, b, preferred_element_type=F32)


def _dot_nt(a, b):
    return lax.dot_general(a, b, (((1,), (1,)), ((), ())), preferred_element_type=F32)


def _ada_kernel(c_ref, w_ref, b_ref, o_ref):
    c = c_ref[...]
    s = (c * jax.nn.sigmoid(c)).astype(BF16)
    o_ref[...] = _dot(s, w_ref[...].astype(BF16)) + b_ref[...]


def _ada_all_layers(c_rows, w_ada, b_ada):
    depth, d, n = w_ada.shape
    tn = _pick_tile(n, (1024, 512, 256, 128))
    return pl.pallas_call(
        _ada_kernel,
        grid=(depth, n // tn),
        in_specs=[pl.BlockSpec((8, d), lambda l, j: (0, 0)),
                  pl.BlockSpec((None, d, tn), lambda l, j: (l, 0, j)),
                  pl.BlockSpec((None, 1, tn), lambda l, j: (l, 0, j))],
        out_specs=pl.BlockSpec((None, 8, tn), lambda l, j: (l, 0, j)),
        out_shape=jax.ShapeDtypeStruct((depth, 8, n), F32),
        compiler_params=_params(("arbitrary", "arbitrary")),
        name="ada_mod",
    )(c_rows, w_ada, b_ada.reshape(depth, 1, n))


def _modulate_kernel(ctx_len, x_ref, m_ref, h_ref):
    x = x_ref[...]
    rows = lax.broadcasted_iota(jnp.int32, x.shape, 0) + pl.program_id(1) * x.shape[0]
    is_ctx = rows < ctx_len
    sc = jnp.where(is_ctx, m_ref[1, 0:1, :], m_ref[0, 0:1, :])
    sh = jnp.where(is_ctx, m_ref[1, 1:2, :], m_ref[0, 1:2, :])
    h_ref[...] = (x * (1.0 + sc) + sh).astype(BF16)


def _modulate(x, mod_sel, ctx_len):
    b, tt, d = x.shape
    tm = _pick_tile(tt, (768, 384, 256, 128))
    return pl.pallas_call(
        functools.partial(_modulate_kernel, ctx_len),
        grid=(b, tt // tm),
        in_specs=[pl.BlockSpec((None, tm, d), lambda i, j: (i, j, 0)),
                  pl.BlockSpec((None, 2, 2, d), lambda i, j: (i, 0, 0, 0))],
        out_specs=pl.BlockSpec((None, tm, d), lambda i, j: (i, j, 0)),
        out_shape=jax.ShapeDtypeStruct((b, tt, d), BF16),
        compiler_params=_params(("arbitrary", "arbitrary")),
        name="modulate",
    )(x, mod_sel)


def _mm_kernel(sq_relu, a_ref, w_ref, o_ref, *scratch):
    if scratch:
        wb_ref, = scratch

        @pl.when(pl.program_id(1) == 0)
        def _():
            wb_ref[...] = w_ref[...].astype(BF16)
    else:
        wb_ref = w_ref

    r = _dot(a_ref[...], wb_ref[...])
    if sq_relu:
        r = jnp.square(jnp.maximum(r, 0.0))
    o_ref[...] = r.astype(o_ref.dtype)


def _matmul(a, w, layer, col0=0, n=None, sq_relu=False, name="matmul"):
    m, k = a.shape
    n = w.shape[2] if n is None else n
    tm = _pick_tile(m, (1152, 1024, 512, 256, 128))
    tn = _pick_tile(n, (1024, 512, 256, 128))
    assert col0 % tn == 0
    return pl.pallas_call(
        functools.partial(_mm_kernel, sq_relu),
        grid=(n // tn, m // tm),
        in_specs=[pl.BlockSpec((tm, k), lambda j, i: (i, 0)),
                  pl.BlockSpec((None, k, tn), lambda j, i: (layer, 0, col0 // tn + j))],
        out_specs=pl.BlockSpec((tm, tn), lambda j, i: (i, j)),
        out_shape=jax.ShapeDtypeStruct((m, n), BF16),
        scratch_shapes=[] if w.dtype == BF16 else [pltpu.VMEM((k, tn), BF16)],
        compiler_params=_params(("arbitrary", "arbitrary")),
        name=name,
    )(a, w)


def _mm_shifted_kernel(shift, a_ref, wa_ref, wb_ref, o_ref, w_s):
    @pl.when(pl.program_id(1) == 0)
    def _():
        tn = wa_ref.shape[1]
        both = jnp.concatenate([wa_ref[...], wb_ref[...]], axis=1).astype(F32)
        w_s[...] = both[:, shift:shift + tn].astype(BF16)

    o_ref[...] = _dot(a_ref[...], w_s[...]).astype(o_ref.dtype)


def _matmul_shifted(a, w, layer, col0, n, name):
    m, k = a.shape
    tm = _pick_tile(m, (1152, 1024, 512, 256, 128))
    tn = _pick_tile(n, (1024, 512, 256, 128))
    shift, base = col0 % LANES, col0 - col0 % LANES
    assert shift > 0 and base % tn == 0 and w.shape[2] >= col0 + n
    return pl.pallas_call(
        functools.partial(_mm_shifted_kernel, shift),
        grid=(n // tn, m // tm),
        in_specs=[pl.BlockSpec((tm, k), lambda j, i: (i, 0)),
                  pl.BlockSpec((None, k, tn), lambda j, i: (layer, 0, base // tn + j)),
                  pl.BlockSpec((None, k, LANES), lambda j, i: (layer, 0, (base + (j + 1) * tn) // LANES))],
        out_specs=pl.BlockSpec((tm, tn), lambda j, i: (i, j)),
        out_shape=jax.ShapeDtypeStruct((m, n), BF16),
        scratch_shapes=[pltpu.VMEM((k, tn), BF16)],
        compiler_params=_params(("arbitrary", "arbitrary")),
        name=name,
    )(a, w, w)


def _level_operand(q, k, la, bc, m, rev):
    c, dk = q.shape
    if m >= 8:
        pieces = []
        for jb in range(c // m):
            rows = slice(jb * m, (jb + 1) * m)
            r = (jb // 2) * 2 * m + (m if rev else m - 1)
            cvec = bc[r:r + 1, :]
            if (jb % 2 == 1) != rev:
                pieces.append(q[rows, :] * jnp.exp2(bc[rows, :] - cvec))
            else:
                pieces.append(k[rows, :] * jnp.exp2(cvec - bc[rows, :]))
        return jnp.concatenate(pieces, axis=0)
    shape3 = (c // 8, 8, dk)
    sub = lax.broadcasted_iota(jnp.int32, (1, 8, dk), 1)
    upper = (sub & m) != 0
    q_role = jnp.logical_not(upper) if rev else upper
    q3, k3, bc3 = q.reshape(shape3), k.reshape(shape3), bc.reshape(shape3)
    if m == 1:
        e2 = jnp.where(q_role, la.reshape(shape3), 0.0)
    else:
        if m == 4:
            r = 4 if rev else 3
            c3 = bc3[:, r:r + 1, :]
        else:
            r0, r1 = (2, 6) if rev else (1, 5)
            c3 = jnp.where(sub < 4, bc3[:, r0:r0 + 1, :], bc3[:, r1:r1 + 1, :])
        e2 = (bc3 - c3) * jnp.where(q_role, 1.0, -1.0)
    return (jnp.where(q_role, q3, k3) * jnp.exp2(e2)).reshape(c, dk)


def _level_matrix(c, rev):
    t = lax.broadcasted_iota(jnp.int32, (c, c), 0)
    s = lax.broadcasted_iota(jnp.int32, (c, c), 1)
    x = t ^ s
    lvl = jnp.full((c, c), -1, jnp.int32)
    b = 1
    while b < c:
        lvl = lvl + (x >= b).astype(jnp.int32)
        b *= 2
    return jnp.where((t < s) if rev else (t > s), lvl, -1)


def _scan_both_directions(prep_chunk, q_s, kf_s, kb_s, laf_s, lab_s, v_ref, o_s, st_s, bc_s, vt_s, sc_s, n_ctx_chunks):
    c = SCAN_CHUNK
    n_chunks = q_s.shape[0] // c
    nh = st_s.shape[1]
    dv = v_ref.shape[1] // nh
    ti = lax.broadcasted_iota(jnp.int32, (c, c), 0)
    si = lax.broadcasted_iota(jnp.int32, (c, c), 1)
    lane = lax.broadcasted_iota(jnp.int32, (c, LANES), 1) // (LANES // nh)
    head_lanes = [(lane == h, (lane == h).astype(BF16)) for h in range(nh)]
    k_refs, la_refs = (kf_s, kb_s), (laf_s, lab_s)
    tris = ((si <= ti).astype(BF16), (si >= ti).astype(BF16))
    lvls = (_level_matrix(c, False), _level_matrix(c, True))

    def cumulate(i, carry):
        sl = pl.ds(pl.multiple_of(i * c, c), c)
        las = prep_chunk(sl)
        for h in range(nh):
            vt_s[i, h * dv:(h + 1) * dv, :] = v_ref[sl, h * dv:(h + 1) * dv].T
        for d in range(2):
            la_hi = las[d].astype(BF16)
            la_lo = (las[d] - la_hi.astype(F32)).astype(BF16)
            bb = _dot(tris[d], jnp.concatenate([la_hi, la_lo], axis=1))
            bc_s[d, sl, :] = (bb[:, :LANES] + bb[:, LANES:]) * LOG2E
            la_refs[d][sl, :] = las[d] * LOG2E
        return carry

    lax.fori_loop(0, n_chunks, cumulate, 0, unroll=2)

    st_s[...] = jnp.zeros_like(st_s)
    nu = SCAN_UNROLL
    n_groups = n_chunks // nu
    assert n_chunks % nu == 0 and n_groups >= 2
    streams = [(d, u) for u in range(nu) for d in range(2)]
    slot = {(su, h): (su[0] * nu + su[1]) * nh + h for su in streams for h in range(nh)}

    def group_rows(g):
        seq = [g * nu + u for u in range(nu)]
        cis = [seq, [jnp.where(j < n_ctx_chunks, n_ctx_chunks - 1 - j, n_chunks - 1 + n_ctx_chunks - j) for j in seq]]
        return cis, {(d, u): pl.ds(pl.multiple_of(cis[d][u] * c, c), c) for d, u in streams}

    def state_phase(g):
        cis, sl = group_rows(g)
        q = {su: q_s[sl[su], :] for su in streams}
        k = {(d, u): k_refs[d][sl[d, u], :] for d, u in streams}
        la = {(d, u): la_refs[d][sl[d, u], :] for d, u in streams}
        bc = {(d, u): bc_s[d, sl[d, u], :] for d, u in streams}
        st = [[st_s[d, h] for h in range(nh)] for d in range(2)]
        inter = {}
        for d, u in streams:
            su = (d, u)
            qg = (q[su] * jnp.exp2(bc[su])).astype(BF16)
            tot = bc[su][0:1, :] if d else bc[su][c - 1:c, :]
            kg = (k[su] * jnp.exp2(tot - bc[su])).astype(BF16)
            dec = jnp.exp2(tot)
            for h in range(nh):
                qg_h = qg if nh == 1 else qg * head_lanes[h][1]
                inter[su, h] = _dot_nt(qg_h, st[d][h].astype(BF16))
                st[d][h] = st[d][h] * dec + _dot(vt_s[cis[d][u], h * dv:(h + 1) * dv, :], kg)
        for d in range(2):
            for h in range(nh):
                st_s[d, h] = st[d][h]
        return sl, q, k, la, bc, inter

    def score_matmuls(g):
        _, sl = group_rows(g)
        return sl, {(su, h): _dot(sc_s[slot[su, h]], v_ref[sl[su], h * dv:(h + 1) * dv])
                    for su in streams for h in range(nh)}

    def level_phase(sl, q, k, la, bc, inter):
        scores = {key: jnp.zeros((c, c), F32) for key in slot}
        m, level = c // 2, (c // 2).bit_length() - 1
        while m >= 1:
            for d, u in streams:
                su = (d, u)
                x = _level_operand(q[su], k[su], la[su], bc[su], m, d == 1).astype(BF16)
                x_heads = x if nh == 1 else jnp.concatenate([x * head_lanes[h][1] for h in range(nh)], axis=0)
                s_all = _dot_nt(x, x_heads)
                for h in range(nh):
                    scores[su, h] = jnp.where(lvls[d] == level, s_all[:, h * c:(h + 1) * c], scores[su, h])
            m, level = m // 2, level - 1
        for d, u in streams:
            su = (d, u)
            qk = q[su] * k[su]
            for h in range(nh):
                sc_s[slot[su, h]] = scores[su, h].astype(BF16)
                v = v_ref[sl[su], h * dv:(h + 1) * dv].astype(F32)
                qk_h = qk if nh == 1 else jnp.where(head_lanes[h][0], qk, 0.0)
                o_s[d, sl[su], h * dv:(h + 1) * dv] = inter[su, h] + jnp.sum(qk_h, axis=1, keepdims=True) * v

    def add_scores_part(sl, outs):
        for (su, h), o in outs.items():
            cols = slice(h * dv, (h + 1) * dv)
            o_s[su[0], sl[su], cols] = o_s[su[0], sl[su], cols] + o

    level_phase(*state_phase(0))

    def body(g, carry):
        cur = state_phase(g)
        sl_prev, outs = score_matmuls(g - 1)
        level_phase(*cur)
        add_scores_part(sl_prev, outs)
        return carry

    lax.fori_loop(1, n_groups, body, 0)
    add_scores_part(*score_matmuls(n_groups - 1))


def _scan_scratch(tt, nh):
    n_chunks = tt // SCAN_CHUNK
    return [pltpu.VMEM((2, tt, nh * HEAD_DV), F32), pltpu.VMEM((2, nh, HEAD_DV, LANES), F32),
            pltpu.VMEM((2, tt, LANES), F32), pltpu.VMEM((n_chunks, nh * HEAD_DV, SCAN_CHUNK), BF16),
            pltpu.VMEM((2 * SCAN_UNROLL * nh, SCAN_CHUNK, SCAN_CHUNK), BF16)]


def _rms_gate(o, gain, gate):
    o = o * lax.rsqrt(jnp.mean(jnp.square(o), axis=-1, keepdims=True) + RMS_EPS) * gain
    return o * (gate * jax.nn.sigmoid(gate))


def _hgrn_kernel(n_ctx_chunks, q_ref, i_ref, g_ref, ff_ref, fb_ref, lb_ref, gain_ref, y_ref,
                 q_s, kf_s, kb_s, laf_s, lab_s, *scan_s):
    c = SCAN_CHUNK
    n_chunks = q_s.shape[0] // c
    o_s = scan_s[0]
    lb_f = lb_ref[0:1, :]
    lb_b = lb_ref[1:2, :]

    def prep_chunk(sl):
        q = q_ref[sl, :].astype(F32)
        q_s[sl, :] = q * jax.nn.sigmoid(q) * (A_DK ** -0.5)
        las = []
        for f_ref, lb, k_s in ((ff_ref, lb_f, kf_s), (fb_ref, lb_b, kb_s)):
            f = f_ref[sl, :].astype(F32)
            k_s[sl, :] = (1.0 - lb) * jax.nn.sigmoid(-f)
            las.append(jnp.log(lb + (1.0 - lb) * jax.nn.sigmoid(f)))
        return las

    _scan_both_directions(prep_chunk, q_s, kf_s, kb_s, laf_s, lab_s, i_ref, *scan_s, n_ctx_chunks)

    def fin(i, carry):
        sl = pl.ds(pl.multiple_of(i * c, c), c)
        o = o_s[0, sl, :] + o_s[1, sl, :]
        y_ref[sl, :] = _rms_gate(o, gain_ref[...], g_ref[sl, :].astype(F32)).astype(y_ref.dtype)
        return carry

    lax.fori_loop(0, n_chunks, fin, 0, unroll=2)


def _hgrn_mixer(p, lb, gain, ctx_len, col0):
    b, tt, _ = p.shape
    blk = lambda seg: pl.BlockSpec((None, tt, LANES), lambda i, h, s=seg: (i, 0, col0 + s * N_HEADS + h))
    seq = lambda: pltpu.VMEM((tt, LANES), F32)
    return pl.pallas_call(
        functools.partial(_hgrn_kernel, ctx_len // SCAN_CHUNK),
        grid=(b, N_HEADS),
        in_specs=[blk(0), blk(1), blk(2), blk(3), blk(4),
                  pl.BlockSpec((2, LANES), lambda i, h: (0, h)),
                  pl.BlockSpec((1, LANES), lambda i, h: (0, 0))],
        out_specs=pl.BlockSpec((None, tt, LANES), lambda i, h: (i, 0, h)),
        out_shape=jax.ShapeDtypeStruct((b, tt, MIX_W), BF16),
        scratch_shapes=[seq(), seq(), seq(), seq(), seq()] + _scan_scratch(tt, 1),
        compiler_params=_params(("arbitrary", "arbitrary")),
        name="hgrn2_mixer",
    )(p, p, p, p, p, lb, gain.reshape(1, HEAD_DV))


def _rope_rotate(x):
    n = x.shape[-1]
    lane = lax.broadcasted_iota(jnp.int32, x.shape, x.ndim - 1)
    first = (lane % 32) < 16
    return jnp.where(first, -pltpu.roll(x, n - 16, x.ndim - 1), pltpu.roll(x, 16, x.ndim - 1))


def _gla_kernel(n_ctx_chunks, q_ref, k_ref, v_ref, g_ref, gk_ref, wg_ref, bg_ref, cos_ref, sin_ref,
                gain_ref, y_ref, q_s, k_s, laf_s, lab_s, *scan_s):
    c = SCAN_CHUNK
    n_chunks = q_s.shape[0] // c
    nh = LANES // B_DK
    o_s = scan_s[0]

    def prep_chunk(sl):
        cos = cos_ref[sl, :]
        sin = sin_ref[sl, :]
        q = q_ref[sl, :].astype(F32)
        k = k_ref[sl, :].astype(F32)
        q_s[sl, :] = (q * cos + _rope_rotate(q) * sin) * (B_DK ** -0.5)
        k_s[sl, :] = k * cos + _rope_rotate(k) * sin
        gk = gk_ref[sl, :]
        return [jax.nn.log_sigmoid(_dot(gk, wg_ref[d]) + bg_ref[d]) / GATE_LOGIT_NORMALIZER for d in range(2)]

    _scan_both_directions(prep_chunk, q_s, k_s, k_s, laf_s, lab_s, v_ref, *scan_s, n_ctx_chunks)

    def fin(i, carry):
        sl = pl.ds(pl.multiple_of(i * c, c), c)
        for h in range(nh):
            cols = slice(h * HEAD_DV, (h + 1) * HEAD_DV)
            o = o_s[0, sl, cols] + o_s[1, sl, cols]
            y_ref[sl, cols] = _rms_gate(o, gain_ref[...], g_ref[sl, cols].astype(F32)).astype(y_ref.dtype)
        return carry

    lax.fori_loop(0, n_chunks, fin, 0, unroll=2)


def _gla_mixer(p, p_gk, wg, bg, cos, sin, gain, ctx_len, col_q, col_k, col_v, col_g):
    b, tt, _ = p.shape
    nh = LANES // B_DK
    wide = nh * HEAD_DV
    pair = lambda c0: pl.BlockSpec((None, tt, LANES), lambda i, h: (i, 0, c0 + h))
    pair_v = lambda c0: pl.BlockSpec((None, tt, wide), lambda i, h: (i, 0, c0 * LANES // wide + h))
    seq = lambda: pltpu.VMEM((tt, LANES), F32)
    assert (col_v * LANES) % wide == 0 and (col_g * LANES) % wide == 0
    return pl.pallas_call(
        functools.partial(_gla_kernel, ctx_len // SCAN_CHUNK),
        grid=(b, N_HEADS // nh),
        in_specs=[pair(col_q), pair(col_k), pair_v(col_v), pair_v(col_g),
                  pl.BlockSpec((None, tt, LANES), lambda i, h: (i, 0, 0)),
                  pl.BlockSpec((2, None, LANES, LANES), lambda i, h: (0, h, 0, 0)),
                  pl.BlockSpec((2, None, 1, LANES), lambda i, h: (0, h, 0, 0)),
                  pl.BlockSpec((tt, LANES), lambda i, h: (0, 0)),
                  pl.BlockSpec((tt, LANES), lambda i, h: (0, 0)),
                  pl.BlockSpec((1, LANES), lambda i, h: (0, 0))],
        out_specs=pl.BlockSpec((None, tt, wide), lambda i, h: (i, 0, h)),
        out_shape=jax.ShapeDtypeStruct((b, tt, MIX_W), BF16),
        scratch_shapes=[seq(), seq(), seq(), seq()] + _scan_scratch(tt, nh),
        compiler_params=_params(("arbitrary", "arbitrary")),
        name="gla_mixer",
    )(p, p, p, p, p_gk, wg, bg, cos, sin, gain.reshape(1, HEAD_DV))


def _natten_kernel(ctx_len, rows, kr, q_ref, k_ref, v_ref, rpb_ref, y_ref, bias_ref):
    scale = HEAD_DV ** -0.5
    kc = k_ref[0:ctx_len, :]
    vc = v_ref[0:ctx_len, :]

    qi = lax.broadcasted_iota(jnp.int32, (GRID_W, LANES), 0)
    kj = lax.broadcasted_iota(jnp.int32, (GRID_W, LANES), 1) % GRID_W
    col_start = jnp.clip(qi - KC // 2, 0, GRID_W - KC)
    col_ok = (kj >= col_start) & (kj < col_start + KC)
    for v in range(kr):
        for blk in range(kr * GRID_W // LANES):
            d = v + blk * (LANES // GRID_W)
            src = jnp.broadcast_to(rpb_ref[d:d + 1, :], (GRID_W, LANES))
            rot = pltpu.roll(src, LANES - (KC - 1), 1, stride=1, stride_axis=0)
            bias_ref[v, :, blk * LANES:(blk + 1) * LANES] = jnp.where(col_ok, rot, MASK_VALUE)

    s = _dot_nt(q_ref[0:ctx_len, :], kc) * scale
    p = jnp.exp(s - jnp.max(s, axis=-1, keepdims=True))
    o = _dot(p.astype(BF16), vc) / jnp.sum(p, axis=-1, keepdims=True)
    y_ref[0:ctx_len, :] = o.astype(y_ref.dtype)

    nr = NATTEN_ROWS_PER_TRIP
    assert rows % nr == 0

    def body(i, carry):
        q_sl, b_sl, s_b, s_c = [], [], [], []
        for u in range(nr):
            r = i * nr + u
            start = jnp.clip(r - kr // 2, 0, rows - kr)
            q_sl.append(pl.ds(pl.multiple_of(ctx_len + r * GRID_W, GRID_W), GRID_W))
            b_sl.append(pl.ds(pl.multiple_of(ctx_len + start * GRID_W, GRID_W), kr * GRID_W))
            q = q_ref[q_sl[u], :]
            s_b.append(_dot_nt(q, k_ref[b_sl[u], :]) * scale + bias_ref[start - r + (KR_MAX - 1)])
            s_c.append(_dot_nt(q, kc) * scale)
        for u in range(nr):
            mx = jnp.maximum(jnp.max(s_b[u], axis=-1, keepdims=True), jnp.max(s_c[u], axis=-1, keepdims=True))
            p_b = jnp.exp(s_b[u] - mx)
            p_c = jnp.exp(s_c[u] - mx)
            den = jnp.sum(p_b, axis=-1, keepdims=True) + jnp.sum(p_c, axis=-1, keepdims=True)
            o = (_dot(p_b.astype(BF16), v_ref[b_sl[u], :]) + _dot(p_c.astype(BF16), vc)) / den
            y_ref[q_sl[u], :] = o.astype(y_ref.dtype)
        return carry

    lax.fori_loop(0, rows // nr, body, 0)


def _natten_mixer(p, rpb, ctx_len, col_q, col_k, col_v):
    b, tt, _ = p.shape
    rows = (tt - ctx_len) // GRID_W
    kr = KR_MAX
    assert rows >= KR_MAX and rows % 2 == 0 and 2 * GRID_W == LANES and 2 * KC - 1 <= GRID_W
    padded = jnp.pad(rpb.astype(F32), ((0, 0), (0, 0), (0, GRID_W - (2 * KC - 1))))
    pairs = jnp.concatenate([padded[:, :-1], padded[:, 1:]], axis=-1)
    head = lambda c0: pl.BlockSpec((None, tt, LANES), lambda i, h: (i, 0, c0 + h))
    return pl.pallas_call(
        functools.partial(_natten_kernel, ctx_len, rows, kr),
        grid=(b, N_HEADS),
        in_specs=[head(col_q), head(col_k), head(col_v),
                  pl.BlockSpec((None, 2 * KR_MAX - 2, LANES), lambda i, h: (h, 0, 0))],
        out_specs=pl.BlockSpec((None, tt, LANES), lambda i, h: (i, 0, h)),
        out_shape=jax.ShapeDtypeStruct((b, tt, MIX_W), BF16),
        scratch_shapes=[pltpu.VMEM((kr, GRID_W, kr * GRID_W), F32)],
        compiler_params=_params(("arbitrary", "arbitrary")),
        name="natten_mixer",
    )(p, p, p, pairs)


def _merge_kernel(ya_ref, yb_ref, yc_ref, w_ref, ga_ref, gb_ref, gc_ref, o_ref):
    acc = None
    for n, (y_ref, g_ref) in enumerate(((ya_ref, ga_ref), (yb_ref, gb_ref), (yc_ref, gc_ref))):
        z = jax.nn.sigmoid(g_ref[...].astype(F32)) * _dot(y_ref[...], w_ref[n])
        acc = z if acc is None else acc + z
    o_ref[...] = acc.astype(o_ref.dtype)


def _merge(ya, yb, yc, w_branch, layer, p, col_gate, d_model):
    m = ya.shape[0]
    tm = _pick_tile(m, (512, 256, 128))
    tn = _pick_tile(d_model, (1024, 512, 256, 128))
    nj = d_model // tn
    assert (col_gate * LANES) % tn == 0
    yspec = pl.BlockSpec((tm, MIX_W), lambda j, i: (i, 0))
    gspec = lambda n: pl.BlockSpec((tm, tn), lambda j, i, n=n: (i, col_gate * LANES // tn + n * nj + j))
    return pl.pallas_call(
        _merge_kernel,
        grid=(nj, m // tm),
        in_specs=[yspec, yspec, yspec,
                  pl.BlockSpec((None, N_BRANCH, MIX_W, tn), lambda j, i: (layer, 0, 0, j)),
                  gspec(0), gspec(1), gspec(2)],
        out_specs=pl.BlockSpec((tm, tn), lambda j, i: (i, j)),
        out_shape=jax.ShapeDtypeStruct((m, d_model), BF16),
        compiler_params=_params(("arbitrary", "arbitrary")),
        name="branch_merge",
    )(ya, yb, yc, w_branch, p, p, p)


def _proj_ln_kernel(alpha, ctx_len, tiles_per_batch, emit_h, a_ref, w_ref, x_ref, mod_ref, ln_ref, *rest):
    if emit_h:
        xo_ref, ho_ref, acc_ref = rest
    else:
        xo_ref, acc_ref = rest
    kk = pl.program_id(1)

    @pl.when(kk == 0)
    def _():
        acc_ref[...] = jnp.zeros_like(acc_ref)

    acc_ref[...] += _dot(a_ref[...], w_ref[...])

    @pl.when(kk == pl.num_programs(1) - 1)
    def _():
        tm, d = acc_ref.shape
        rb = LN_ROW_BLOCK
        row0 = (pl.program_id(0) % tiles_per_batch) * tm

        def block(i, carry):
            sl = pl.ds(pl.multiple_of(i * rb, rb), rb)
            mod = mod_ref[jnp.where(row0 + i * rb < ctx_len, 1, 0)]
            sel = lambda j: mod[j:j + 1, :]
            y = alpha * x_ref[sl, :] + sel(0) * acc_ref[sl, :]
            mu = jnp.mean(y, axis=-1, keepdims=True)
            yc = y - mu
            var = jnp.mean(jnp.square(yc), axis=-1, keepdims=True)
            xn = yc * lax.rsqrt(var + LN_EPS) * ln_ref[0:1, :] + ln_ref[1:2, :]
            xo_ref[sl, :] = xn
            if emit_h:
                ho_ref[sl, :] = (xn * (1.0 + sel(1)) + sel(2)).astype(BF16)
            return carry

        lax.fori_loop(0, tm // rb, block, 0)


def _proj_ln(a, w, layer, x, mod_sel, ln_gb, alpha, ctx_len, tt, emit_h, name):
    m, k = a.shape
    d = w.shape[2]
    assert w.dtype == BF16
    if k * d * 2 <= RESIDENT_WEIGHT_BYTES:
        tm, tk = _pick_tile(tt, (384, 256, 128)), k
    else:
        tm, tk = _pick_tile(tt, (768, 384, 256, 128)), _pick_tile(k, (1024, 512, 256, 128))
    assert tm % LN_ROW_BLOCK == 0 and ctx_len <= tm and ctx_len % LN_ROW_BLOCK == 0
    tpb = tt // tm
    out_shape = [jax.ShapeDtypeStruct((m, d), F32)]
    out_specs = [pl.BlockSpec((tm, d), lambda i, j: (i, 0))]
    if emit_h:
        out_shape.append(jax.ShapeDtypeStruct((m, d), BF16))
        out_specs.append(pl.BlockSpec((tm, d), lambda i, j: (i, 0)))
    res = pl.pallas_call(
        functools.partial(_proj_ln_kernel, alpha, ctx_len, tpb, emit_h),
        grid=(m // tm, k // tk),
        in_specs=[pl.BlockSpec((tm, tk), lambda i, j: (i, j)),
                  pl.BlockSpec((None, tk, d), lambda i, j: (layer, j, 0)),
                  pl.BlockSpec((tm, d), lambda i, j: (i, 0)),
                  pl.BlockSpec((None, 2, 3, d), lambda i, j: (i // tpb, 0, 0, 0)),
                  pl.BlockSpec((2, d), lambda i, j: (0, 0))],
        out_specs=out_specs,
        out_shape=out_shape,
        scratch_shapes=[pltpu.VMEM((tm, d), F32)],
        compiler_params=_params(("arbitrary", "arbitrary")),
        name=name,
    )(a, w, x, mod_sel, ln_gb)
    return (res[0], res[1]) if emit_h else (res[0], None)


def _rope_tables(seq, ctx_len):
    half = B_DK // 2
    freqs = ROPE_THETA ** (-jnp.arange(0, half, 2, dtype=F32) / half)
    pos = jnp.arange(seq)
    ang_r = (pos // GRID_W).astype(F32)[:, None] * freqs
    ang_c = (pos % GRID_W).astype(F32)[:, None] * freqs
    ang = jnp.concatenate([ang_r, ang_r, ang_c, ang_c], axis=-1)
    ang = jnp.concatenate([jnp.zeros((ctx_len, B_DK), F32), ang], axis=0)
    ang = jnp.concatenate([ang, ang], axis=-1)
    return jnp.cos(ang), jnp.sin(ang)


def kernel(x, c, ctx, c_ctx, w_ada, b_ada, w_in, hgrn_lb_logits, hgrn_norm_g, gla_w_gk2, gla_b_gk2, gla_norm_g,
           natten_rpb, w_branch, w_out, ln1_g, ln1_b, ln2_g, ln2_b, w_mlp1, w_mlp2):
    bsz, seq, d = x.shape
    ctx_len = ctx.shape[1]
    depth = w_ada.shape[0]
    tt = ctx_len + seq
    m_tok = bsz * tt
    alpha = (2 * depth) ** 0.25
    assert ctx_len % SCAN_CHUNK == 0 and seq % SCAN_CHUNK == 0 and seq % GRID_W == 0 and bsz + 1 <= 8

    seg = N_HEADS * A_DK
    gk_lo = 5 * seg + 2 * N_HEADS * B_DK + 2 * MIX_W
    gk_hi = gk_lo + 2 * GK_RANK
    n_cg = 3 * MIX_W + N_BRANCH * d
    assert gk_lo % 1024 == 0 and w_in.shape[2] == gk_hi + n_cg
    col_a = 0
    col_bq = 5 * seg // LANES
    col_bk = col_bq + N_HEADS * B_DK // LANES
    col_bv = col_bk + N_HEADS * B_DK // LANES
    col_bg = col_bv + MIX_W // LANES
    col_cq = 0
    col_ck = col_cq + MIX_W // LANES
    col_cv = col_ck + MIX_W // LANES
    col_gate = col_cv + MIX_W // LANES

    c_rows = jnp.zeros((8, d), F32).at[:bsz].set(c).at[bsz].set(c_ctx)
    mod = _ada_all_layers(c_rows, w_ada, b_ada).reshape(depth, 8, 6, d)

    def mod_sel(l, idx):
        lat = mod[l, :bsz][:, idx, :]
        cx = jnp.broadcast_to(mod[l, bsz][idx, :], lat.shape)
        return jnp.stack([lat, cx], axis=1)

    lb_p = jax.nn.softmax(hgrn_lb_logits.astype(F32), axis=0)
    lb_cum = jnp.cumsum(lb_p, axis=0)
    lower_bounds = jnp.concatenate([jnp.zeros_like(lb_cum[:1]), lb_cum[:-1]], axis=0)

    cos, sin = _rope_tables(seq, ctx_len)
    w_out_bf = w_out.astype(BF16)
    w_mlp2_bf = w_mlp2.astype(BF16)

    xs = jnp.concatenate([ctx, x], axis=1)
    h = _modulate(xs, mod_sel(0, np.array([1, 0])), ctx_len).reshape(m_tok, d)
    xs = xs.reshape(m_tok, d)

    w_in_bf = w_in.astype(BF16)
    w_branch_bf = w_branch.astype(BF16)

    for l in range(depth):
        p = _matmul(h, w_in_bf, l, 0, gk_lo, name="in_proj_ab").reshape(bsz, tt, gk_lo)
        p_gk = _matmul(h, w_in_bf, l, gk_lo, LANES, name="in_proj_gk").reshape(bsz, tt, LANES)
        p_cg = _matmul_shifted(h, w_in_bf, l, gk_hi, n_cg, "in_proj_cg").reshape(bsz, tt, n_cg)

        n_pairs = N_HEADS * B_DK // LANES
        wg = jnp.zeros((2, n_pairs, LANES, LANES), F32)
        w2 = gla_w_gk2[l].reshape(2, GK_RANK, n_pairs, LANES).transpose(0, 2, 1, 3)
        wg = wg.at[0, :, :GK_RANK].set(w2[0]).at[1, :, GK_RANK:2 * GK_RANK].set(w2[1]).astype(BF16)
        bg = gla_b_gk2[l].reshape(2, n_pairs, 1, LANES)

        y_a = _hgrn_mixer(p, lower_bounds[l], hgrn_norm_g[l], ctx_len, col_a)
        y_b = _gla_mixer(p, p_gk, wg, bg, cos, sin, gla_norm_g[l], ctx_len, col_bq, col_bk, col_bv, col_bg)
        y_c = _natten_mixer(p_cg, natten_rpb[l], ctx_len, col_cq, col_ck, col_cv)

        merged = _merge(y_a.reshape(m_tok, MIX_W), y_b.reshape(m_tok, MIX_W), y_c.reshape(m_tok, MIX_W),
                        w_branch_bf, l, p_cg.reshape(m_tok, n_cg), col_gate, d)
        xs, h2 = _proj_ln(merged, w_out_bf, l, xs, mod_sel(l, np.array([2, 4, 3])),
                          jnp.stack([ln1_g[l], ln1_b[l]]), alpha, ctx_len, tt, True, "out_proj_ln")
        u = _matmul(h2, w_mlp1, l, sq_relu=True, name="mlp_up")
        last = l == depth - 1
        if last:
            nxt = jnp.zeros((bsz, 2, 3, d), F32).at[:, :, 0].set(mod_sel(l, np.array([5]))[:, :, 0])
        else:
            nxt = jnp.concatenate([mod_sel(l, np.array([5])), mod_sel(l + 1, np.array([1, 0]))], axis=2)
        xs, h = _proj_ln(u, w_mlp2_bf, l, xs, nxt, jnp.stack([ln2_g[l], ln2_b[l]]),
                         alpha, ctx_len, tt, not last, "mlp_down_ln")

    return xs.reshape(bsz, tt, d)[:, ctx_len:, :]
```

```python
import functools

import jax
import jax.numpy as jnp
import numpy as np
from jax import lax
from jax.experimental import pallas as pl
from jax.experimental.pallas import tpu as pltpu

GRID_W = 64
N_BRANCH = 3
MIX_W = 1024
N_HEADS = 8
A_DK = 128
B_DK = 64
HEAD_DV = MIX_W // N_HEADS
GK_RANK = 16
GATE_LOGIT_NORMALIZER = 16.0
KR_MAX = 8
KC = 16
ROPE_THETA = 10000.0
RMS_EPS = 1e-6
LN_EPS = 1e-5
MASK_VALUE = -1e30
LOG2E = 1.4426950408889634

LANES = 128
SCAN_CHUNK = 128
SCAN_UNROLL = 2
NATTEN_ROWS_PER_TRIP = 8
LN_ROW_BLOCK = 128
VMEM_LIMIT_BYTES = 56 * 1024 * 1024
RESIDENT_WEIGHT_BYTES = 16 * 1024 * 1024

F32 = jnp.float32
BF16 = jnp.bfloat16


def _pick_tile(n, prefs):
    for t in prefs:
        if n % t == 0:
            return t
    return n


def _params(sem):
    return pltpu.CompilerParams(dimension_semantics=sem, vmem_limit_bytes=VMEM_LIMIT_BYTES)


def _dot(a, b):
    return jnp.dot(a, b, preferred_element_type=F32)


def _dot_nt(a, b):
    return lax.dot_general(a, b, (((1,), (1,)), ((), ())), preferred_element_type=F32)


def _ada_kernel(c_ref, w_ref, b_ref, o_ref):
    c = c_ref[...]
    s = (c * jax.nn.sigmoid(c)).astype(BF16)
    o_ref[...] = _dot(s, w_ref[...].astype(BF16)) + b_ref[...]


def _ada_all_layers(c_rows, w_ada, b_ada):
    depth, d, n = w_ada.shape
    tn = _pick_tile(n, (1024, 512, 256, 128))
    return pl.pallas_call(
        _ada_kernel,
        grid=(depth, n // tn),
        in_specs=[pl.BlockSpec((8, d), lambda l, j: (0, 0)),
                  pl.BlockSpec((None, d, tn), lambda l, j: (l, 0, j)),
                  pl.BlockSpec((None, 1, tn), lambda l, j: (l, 0, j))],
        out_specs=pl.BlockSpec((None, 8, tn), lambda l, j: (l, 0, j)),
        out_shape=jax.ShapeDtypeStruct((depth, 8, n), F32),
        compiler_params=_params(("arbitrary", "arbitrary")),
        name="ada_mod",
    )(c_rows, w_ada, b_ada.reshape(depth, 1, n))


def _modulate_kernel(ctx_len, x_ref, m_ref, h_ref):
    x = x_ref[...]
    rows = lax.broadcasted_iota(jnp.int32, x.shape, 0) + pl.program_id(1) * x.shape[0]
    is_ctx = rows < ctx_len
    sc = jnp.where(is_ctx, m_ref[1, 0:1, :], m_ref[0, 0:1, :])
    sh = jnp.where(is_ctx, m_ref[1, 1:2, :], m_ref[0, 1:2, :])
    h_ref[...] = (x * (1.0 + sc) + sh).astype(BF16)


def _modulate(x, mod_sel, ctx_len):
    b, tt, d = x.shape
    tm = _pick_tile(tt, (768, 384, 256, 128))
    return pl.pallas_call(
        functools.partial(_modulate_kernel, ctx_len),
        grid=(b, tt // tm),
        in_specs=[pl.BlockSpec((None, tm, d), lambda i, j: (i, j, 0)),
                  pl.BlockSpec((None, 2, 2, d), lambda i, j: (i, 0, 0, 0))],
        out_specs=pl.BlockSpec((None, tm, d), lambda i, j: (i, j, 0)),
        out_shape=jax.ShapeDtypeStruct((b, tt, d), BF16),
        compiler_params=_params(("arbitrary", "arbitrary")),
        name="modulate",
    )(x, mod_sel)


def _mm_kernel(sq_relu, a_ref, w_ref, o_ref, *scratch):
    if scratch:
        wb_ref, = scratch

        @pl.when(pl.program_id(1) == 0)
        def _():
            wb_ref[...] = w_ref[...].astype(BF16)
    else:
        wb_ref = w_ref

    r = _dot(a_ref[...], wb_ref[...])
    if sq_relu:
        r = jnp.square(jnp.maximum(r, 0.0))
    o_ref[...] = r.astype(o_ref.dtype)


def _matmul(a, w, layer, col0=0, n=None, sq_relu=False, name="matmul"):
    m, k = a.shape
    n = w.shape[2] if n is None else n
    tm = _pick_tile(m, (1152, 1024, 512, 256, 128))
    tn = _pick_tile(n, (1024, 512, 256, 128))
    assert col0 % tn == 0
    return pl.pallas_call(
        functools.partial(_mm_kernel, sq_relu),
        grid=(n // tn, m // tm),
        in_specs=[pl.BlockSpec((tm, k), lambda j, i: (i, 0)),
                  pl.BlockSpec((None, k, tn), lambda j, i: (layer, 0, col0 // tn + j))],
        out_specs=pl.BlockSpec((tm, tn), lambda j, i: (i, j)),
        out_shape=jax.ShapeDtypeStruct((m, n), BF16),
        scratch_shapes=[] if w.dtype == BF16 else [pltpu.VMEM((k, tn), BF16)],
        compiler_params=_params(("arbitrary", "arbitrary")),
        name=name,
    )(a, w)


def _mm_t_kernel(a_ref, w_ref, o_ref, wb_ref):
    @pl.when(pl.program_id(1) == 0)
    def _():
        wb_ref[...] = w_ref[0].astype(BF16)

    o_ref[...] = _dot_nt(a_ref[...], wb_ref[...]).astype(o_ref.dtype)


def _matmul_t(a, wt, layer, row0, n, name):
    m, k = a.shape
    tm = _pick_tile(m, (1152, 1024, 512, 256, 128))
    tn = _pick_tile(n, (1024, 512, 256, 128))
    assert row0 % 16 == 0 and wt.shape[1] >= row0 + n
    return pl.pallas_call(
        _mm_t_kernel,
        grid=(n // tn, m // tm),
        in_specs=[pl.BlockSpec((tm, k), lambda j, i: (i, 0)),
                  pl.BlockSpec((pl.Element(1), pl.Element(tn), pl.Element(k)),
                               lambda j, i: (layer, pl.multiple_of(row0 + j * tn, 16), 0))],
        out_specs=pl.BlockSpec((tm, tn), lambda j, i: (i, j)),
        out_shape=jax.ShapeDtypeStruct((m, n), BF16),
        scratch_shapes=[pltpu.VMEM((tn, k), BF16)],
        compiler_params=_params(("arbitrary", "arbitrary")),
        name=name,
    )(a, wt)


def _level_operand(q, k, la, bc, m, rev):
    c, dk = q.shape
    if m >= 8:
        pieces = []
        for jb in range(c // m):
            rows = slice(jb * m, (jb + 1) * m)
            r = (jb // 2) * 2 * m + (m if rev else m - 1)
            cvec = bc[r:r + 1, :]
            if (jb % 2 == 1) != rev:
                pieces.append(q[rows, :] * jnp.exp2(bc[rows, :] - cvec))
            else:
                pieces.append(k[rows, :] * jnp.exp2(cvec - bc[rows, :]))
        return jnp.concatenate(pieces, axis=0)
    shape3 = (c // 8, 8, dk)
    sub = lax.broadcasted_iota(jnp.int32, (1, 8, dk), 1)
    upper = (sub & m) != 0
    q_role = jnp.logical_not(upper) if rev else upper
    q3, k3, bc3 = q.reshape(shape3), k.reshape(shape3), bc.reshape(shape3)
    if m == 1:
        e2 = jnp.where(q_role, la.reshape(shape3), 0.0)
    else:
        if m == 4:
            r = 4 if rev else 3
            c3 = bc3[:, r:r + 1, :]
        else:
            r0, r1 = (2, 6) if rev else (1, 5)
            c3 = jnp.where(sub < 4, bc3[:, r0:r0 + 1, :], bc3[:, r1:r1 + 1, :])
        e2 = (bc3 - c3) * jnp.where(q_role, 1.0, -1.0)
    return (jnp.where(q_role, q3, k3) * jnp.exp2(e2)).reshape(c, dk)


def _level_matrix(c, rev):
    t = lax.broadcasted_iota(jnp.int32, (c, c), 0)
    s = lax.broadcasted_iota(jnp.int32, (c, c), 1)
    x = t ^ s
    lvl = jnp.full((c, c), -1, jnp.int32)
    b = 1
    while b < c:
        lvl = lvl + (x >= b).astype(jnp.int32)
        b *= 2
    return jnp.where((t < s) if rev else (t > s), lvl, -1)


def _scan_both_directions(prep_chunk, q_s, kf_s, kb_s, laf_s, lab_s, v_ref, o_s, st_s, bc_s, vt_s, sc_s, n_ctx_chunks):
    c = SCAN_CHUNK
    n_chunks = q_s.shape[0] // c
    nh = st_s.shape[1]
    dv = v_ref.shape[1] // nh
    ti = lax.broadcasted_iota(jnp.int32, (c, c), 0)
    si = lax.broadcasted_iota(jnp.int32, (c, c), 1)
    lane = lax.broadcasted_iota(jnp.int32, (c, LANES), 1) // (LANES // nh)
    head_lanes = [(lane == h, (lane == h).astype(BF16)) for h in range(nh)]
    k_refs, la_refs = (kf_s, kb_s), (laf_s, lab_s)
    tris = ((si <= ti).astype(BF16), (si >= ti).astype(BF16))
    lvls = (_level_matrix(c, False), _level_matrix(c, True))

    def cumulate(i, carry):
        sl = pl.ds(pl.multiple_of(i * c, c), c)
        las = prep_chunk(sl)
        for h in range(nh):
            vt_s[i, h * dv:(h + 1) * dv, :] = v_ref[sl, h * dv:(h + 1) * dv].T
        for d in range(2):
            la_hi = las[d].astype(BF16)
            la_lo = (las[d] - la_hi.astype(F32)).astype(BF16)
            bb = _dot(tris[d], jnp.concatenate([la_hi, la_lo], axis=1))
            bc_s[d, sl, :] = (bb[:, :LANES] + bb[:, LANES:]) * LOG2E
            la_refs[d][sl, :] = las[d] * LOG2E
        return carry

    lax.fori_loop(0, n_chunks, cumulate, 0, unroll=3)

    st_s[...] = jnp.zeros_like(st_s)
    nu = SCAN_UNROLL
    n_groups = n_chunks // nu
    assert n_chunks % nu == 0 and n_groups >= 2
    streams = [(d, u) for u in range(nu) for d in range(2)]
    slot = {(su, h): (su[0] * nu + su[1]) * nh + h for su in streams for h in range(nh)}

    def group_rows(g):
        seq = [g * nu + u for u in range(nu)]
        cis = [seq, [jnp.where(j < n_ctx_chunks, n_ctx_chunks - 1 - j, n_chunks - 1 + n_ctx_chunks - j) for j in seq]]
        return cis, {(d, u): pl.ds(pl.multiple_of(cis[d][u] * c, c), c) for d, u in streams}

    def state_phase(g):
        cis, sl = group_rows(g)
        q = {su: q_s[sl[su], :] for su in streams}
        k = {(d, u): k_refs[d][sl[d, u], :] for d, u in streams}
        la = {(d, u): la_refs[d][sl[d, u], :] for d, u in streams}
        bc = {(d, u): bc_s[d, sl[d, u], :] for d, u in streams}
        st = [[st_s[d, h] for h in range(nh)] for d in range(2)]
        inter = {}
        for d, u in streams:
            su = (d, u)
            qg = (q[su] * jnp.exp2(bc[su])).astype(BF16)
            tot = bc[su][0:1, :] if d else bc[su][c - 1:c, :]
            kg = (k[su] * jnp.exp2(tot - bc[su])).astype(BF16)
            dec = jnp.exp2(tot)
            for h in range(nh):
                qg_h = qg if nh == 1 else qg * head_lanes[h][1]
                inter[su, h] = _dot_nt(qg_h, st[d][h].astype(BF16))
                st[d][h] = st[d][h] * dec + _dot(vt_s[cis[d][u], h * dv:(h + 1) * dv, :], kg)
        for d in range(2):
            for h in range(nh):
                st_s[d, h] = st[d][h]
        return sl, q, k, la, bc, inter

    def score_matmuls(g):
        _, sl = group_rows(g)
        return sl, {(su, h): _dot(sc_s[slot[su, h]], v_ref[sl[su], h * dv:(h + 1) * dv])
                    for su in streams for h in range(nh)}

    def level_phase(sl, q, k, la, bc, inter):
        scores = {key: jnp.zeros((c, c), F32) for key in slot}
        m, level = c // 2, (c // 2).bit_length() - 1
        while m >= 1:
            for d, u in streams:
                su = (d, u)
                x = _level_operand(q[su], k[su], la[su], bc[su], m, d == 1).astype(BF16)
                x_heads = x if nh == 1 else jnp.concatenate([x * head_lanes[h][1] for h in range(nh)], axis=0)
                s_all = _dot_nt(x, x_heads)
                for h in range(nh):
                    scores[su, h] = jnp.where(lvls[d] == level, s_all[:, h * c:(h + 1) * c], scores[su, h])
            m, level = m // 2, level - 1
        for d, u in streams:
            su = (d, u)
            qk = q[su] * k[su]
            for h in range(nh):
                sc_s[slot[su, h]] = scores[su, h].astype(BF16)
                v = v_ref[sl[su], h * dv:(h + 1) * dv].astype(F32)
                qk_h = qk if nh == 1 else jnp.where(head_lanes[h][0], qk, 0.0)
                o_s[d, sl[su], h * dv:(h + 1) * dv] = inter[su, h] + jnp.sum(qk_h, axis=1, keepdims=True) * v

    def add_scores_part(sl, outs):
        for (su, h), o in outs.items():
            cols = slice(h * dv, (h + 1) * dv)
            o_s[su[0], sl[su], cols] = o_s[su[0], sl[su], cols] + o

    level_phase(*state_phase(0))

    def body(g, carry):
        cur = state_phase(g)
        sl_prev, outs = score_matmuls(g - 1)
        level_phase(*cur)
        add_scores_part(sl_prev, outs)
        return carry

    lax.fori_loop(1, n_groups, body, 0)
    add_scores_part(*score_matmuls(n_groups - 1))


def _scan_scratch(tt, nh):
    n_chunks = tt // SCAN_CHUNK
    return [pltpu.VMEM((2, tt, nh * HEAD_DV), F32), pltpu.VMEM((2, nh, HEAD_DV, LANES), F32),
            pltpu.VMEM((2, tt, LANES), F32), pltpu.VMEM((n_chunks, nh * HEAD_DV, SCAN_CHUNK), BF16),
            pltpu.VMEM((2 * SCAN_UNROLL * nh, SCAN_CHUNK, SCAN_CHUNK), BF16)]


def _rms_gate(o, gain, gate):
    o = o * lax.rsqrt(jnp.mean(jnp.square(o), axis=-1, keepdims=True) + RMS_EPS) * gain
    return o * (gate * jax.nn.sigmoid(gate))


def _hgrn_kernel(n_ctx_chunks, q_ref, i_ref, g_ref, ff_ref, fb_ref, lb_ref, gain_ref, y_ref,
                 q_s, kf_s, kb_s, laf_s, lab_s, *scan_s):
    c = SCAN_CHUNK
    n_chunks = q_s.shape[0] // c
    o_s = scan_s[0]
    lb_f = lb_ref[0:1, :]
    lb_b = lb_ref[1:2, :]

    def prep_chunk(sl):
        q = q_ref[sl, :].astype(F32)
        q_s[sl, :] = q * jax.nn.sigmoid(q) * (A_DK ** -0.5)
        las = []
        for f_ref, lb, k_s in ((ff_ref, lb_f, kf_s), (fb_ref, lb_b, kb_s)):
            s = jax.nn.sigmoid(f_ref[sl, :].astype(F32))
            k_s[sl, :] = (1.0 - lb) * (1.0 - s)
            las.append(jnp.log(lb + (1.0 - lb) * s))
        return las

    _scan_both_directions(prep_chunk, q_s, kf_s, kb_s, laf_s, lab_s, i_ref, *scan_s, n_ctx_chunks)

    def fin(i, carry):
        sl = pl.ds(pl.multiple_of(i * c, c), c)
        o = o_s[0, sl, :] + o_s[1, sl, :]
        y_ref[sl, :] = _rms_gate(o, gain_ref[...], g_ref[sl, :].astype(F32)).astype(y_ref.dtype)
        return carry

    lax.fori_loop(0, n_chunks, fin, 0, unroll=2)


def _hgrn_mixer(p, lb, gain, ctx_len, col0):
    b, tt, _ = p.shape
    blk = lambda seg: pl.BlockSpec((None, tt, LANES), lambda i, h, s=seg: (i, 0, col0 + s * N_HEADS + h))
    seq = lambda: pltpu.VMEM((tt, LANES), F32)
    return pl.pallas_call(
        functools.partial(_hgrn_kernel, ctx_len // SCAN_CHUNK),
        grid=(b, N_HEADS),
        in_specs=[blk(0), blk(1), blk(2), blk(3), blk(4),
                  pl.BlockSpec((2, LANES), lambda i, h: (0, h)),
                  pl.BlockSpec((1, LANES), lambda i, h: (0, 0))],
        out_specs=pl.BlockSpec((None, tt, LANES), lambda i, h: (i, 0, h)),
        out_shape=jax.ShapeDtypeStruct((b, tt, MIX_W), BF16),
        scratch_shapes=[seq(), seq(), seq(), seq(), seq()] + _scan_scratch(tt, 1),
        compiler_params=_params(("arbitrary", "arbitrary")),
        name="hgrn2_mixer",
    )(p, p, p, p, p, lb, gain.reshape(1, HEAD_DV))


def _rope_rotate(x):
    n = x.shape[-1]
    lane = lax.broadcasted_iota(jnp.int32, x.shape, x.ndim - 1)
    first = (lane % 32) < 16
    return jnp.where(first, -pltpu.roll(x, n - 16, x.ndim - 1), pltpu.roll(x, 16, x.ndim - 1))


def _gla_kernel(n_ctx_chunks, q_ref, k_ref, v_ref, g_ref, gk_ref, wg_ref, bg_ref, cos_ref, sin_ref,
                gain_ref, y_ref, q_s, k_s, laf_s, lab_s, *scan_s):
    c = SCAN_CHUNK
    n_chunks = q_s.shape[0] // c
    nh = LANES // B_DK
    o_s = scan_s[0]

    def prep_chunk(sl):
        cos = cos_ref[sl, :]
        sin = sin_ref[sl, :]
        q = q_ref[sl, :].astype(F32)
        k = k_ref[sl, :].astype(F32)
        q_s[sl, :] = (q * cos + _rope_rotate(q) * sin) * (B_DK ** -0.5)
        k_s[sl, :] = k * cos + _rope_rotate(k) * sin
        gk = gk_ref[sl, :]
        return [jax.nn.log_sigmoid(_dot(gk, wg_ref[d]) + bg_ref[d]) / GATE_LOGIT_NORMALIZER for d in range(2)]

    _scan_both_directions(prep_chunk, q_s, k_s, k_s, laf_s, lab_s, v_ref, *scan_s, n_ctx_chunks)

    def fin(i, carry):
        sl = pl.ds(pl.multiple_of(i * c, c), c)
        for h in range(nh):
            cols = slice(h * HEAD_DV, (h + 1) * HEAD_DV)
            o = o_s[0, sl, cols] + o_s[1, sl, cols]
            y_ref[sl, cols] = _rms_gate(o, gain_ref[...], g_ref[sl, cols].astype(F32)).astype(y_ref.dtype)
        return carry

    lax.fori_loop(0, n_chunks, fin, 0, unroll=2)


def _gla_mixer(p, p_gk, wg, bg, cos, sin, gain, ctx_len, col_q, col_k, col_v, col_g):
    b, tt, _ = p.shape
    nh = LANES // B_DK
    wide = nh * HEAD_DV
    pair = lambda c0: pl.BlockSpec((None, tt, LANES), lambda i, h: (i, 0, c0 + h))
    pair_v = lambda c0: pl.BlockSpec((None, tt, wide), lambda i, h: (i, 0, c0 * LANES // wide + h))
    seq = lambda: pltpu.VMEM((tt, LANES), F32)
    assert (col_v * LANES) % wide == 0 and (col_g * LANES) % wide == 0
    return pl.pallas_call(
        functools.partial(_gla_kernel, ctx_len // SCAN_CHUNK),
        grid=(b, N_HEADS // nh),
        in_specs=[pair(col_q), pair(col_k), pair_v(col_v), pair_v(col_g),
                  pl.BlockSpec((None, tt, LANES), lambda i, h: (i, 0, 0)),
                  pl.BlockSpec((2, None, LANES, LANES), lambda i, h: (0, h, 0, 0)),
                  pl.BlockSpec((2, None, 1, LANES), lambda i, h: (0, h, 0, 0)),
                  pl.BlockSpec((tt, LANES), lambda i, h: (0, 0)),
                  pl.BlockSpec((tt, LANES), lambda i, h: (0, 0)),
                  pl.BlockSpec((1, LANES), lambda i, h: (0, 0))],
        out_specs=pl.BlockSpec((None, tt, wide), lambda i, h: (i, 0, h)),
        out_shape=jax.ShapeDtypeStruct((b, tt, MIX_W), BF16),
        scratch_shapes=[seq(), seq(), seq(), seq()] + _scan_scratch(tt, nh),
        compiler_params=_params(("arbitrary", "arbitrary")),
        name="gla_mixer",
    )(p, p, p, p, p_gk, wg, bg, cos, sin, gain.reshape(1, HEAD_DV))


def _natten_kernel(ctx_len, rows, kr, q_ref, k_ref, v_ref, rpb_ref, y_ref, bias_ref):
    scale = HEAD_DV ** -0.5
    kc = k_ref[0:ctx_len, :]
    vc = v_ref[0:ctx_len, :]

    qi = lax.broadcasted_iota(jnp.int32, (GRID_W, LANES), 0)
    kj = lax.broadcasted_iota(jnp.int32, (GRID_W, LANES), 1) % GRID_W
    col_start = jnp.clip(qi - KC // 2, 0, GRID_W - KC)
    col_ok = (kj >= col_start) & (kj < col_start + KC)
    for v in range(kr):
        for blk in range(kr * GRID_W // LANES):
            d = v + blk * (LANES // GRID_W)
            src = jnp.broadcast_to(rpb_ref[d:d + 1, :], (GRID_W, LANES))
            rot = pltpu.roll(src, LANES - (KC - 1), 1, stride=1, stride_axis=0)
            bias_ref[v, :, blk * LANES:(blk + 1) * LANES] = jnp.where(col_ok, rot, MASK_VALUE)

    s = _dot_nt(q_ref[0:ctx_len, :], kc) * scale
    p = jnp.exp(s - jnp.max(s, axis=-1, keepdims=True))
    o = _dot(p.astype(BF16), vc) / jnp.sum(p, axis=-1, keepdims=True)
    y_ref[0:ctx_len, :] = o.astype(y_ref.dtype)

    nr = NATTEN_ROWS_PER_TRIP
    assert rows % nr == 0

    def body(i, carry):
        q_sl, b_sl, s_b, s_c = [], [], [], []
        for u in range(nr):
            r = i * nr + u
            start = jnp.clip(r - kr // 2, 0, rows - kr)
            q_sl.append(pl.ds(pl.multiple_of(ctx_len + r * GRID_W, GRID_W), GRID_W))
            b_sl.append(pl.ds(pl.multiple_of(ctx_len + start * GRID_W, GRID_W), kr * GRID_W))
            q = q_ref[q_sl[u], :]
            s_b.append(_dot_nt(q, k_ref[b_sl[u], :]) * scale + bias_ref[start - r + (KR_MAX - 1)])
            s_c.append(_dot_nt(q, kc) * scale)
        for u in range(nr):
            mx = jnp.maximum(jnp.max(s_b[u], axis=-1, keepdims=True), jnp.max(s_c[u], axis=-1, keepdims=True))
            p_b = jnp.exp(s_b[u] - mx)
            p_c = jnp.exp(s_c[u] - mx)
            den = jnp.sum(p_b, axis=-1, keepdims=True) + jnp.sum(p_c, axis=-1, keepdims=True)
            o = (_dot(p_b.astype(BF16), v_ref[b_sl[u], :]) + _dot(p_c.astype(BF16), vc)) / den
            y_ref[q_sl[u], :] = o.astype(y_ref.dtype)
        return carry

    lax.fori_loop(0, rows // nr, body, 0)


def _natten_mixer(p, rpb, ctx_len, col_q, col_k, col_v):
    b, tt, _ = p.shape
    rows = (tt - ctx_len) // GRID_W
    kr = KR_MAX
    assert rows >= KR_MAX and rows % 2 == 0 and 2 * GRID_W == LANES and 2 * KC - 1 <= GRID_W
    padded = jnp.pad(rpb.astype(F32), ((0, 0), (0, 0), (0, GRID_W - (2 * KC - 1))))
    pairs = jnp.concatenate([padded[:, :-1], padded[:, 1:]], axis=-1)
    head = lambda c0: pl.BlockSpec((None, tt, LANES), lambda i, h: (i, 0, c0 + h))
    return pl.pallas_call(
        functools.partial(_natten_kernel, ctx_len, rows, kr),
        grid=(b, N_HEADS),
        in_specs=[head(col_q), head(col_k), head(col_v),
                  pl.BlockSpec((None, 2 * KR_MAX - 2, LANES), lambda i, h: (h, 0, 0))],
        out_specs=pl.BlockSpec((None, tt, LANES), lambda i, h: (i, 0, h)),
        out_shape=jax.ShapeDtypeStruct((b, tt, MIX_W), BF16),
        scratch_shapes=[pltpu.VMEM((kr, GRID_W, kr * GRID_W), F32)],
        compiler_params=_params(("arbitrary", "arbitrary")),
        name="natten_mixer",
    )(p, p, p, pairs)


def _merge_kernel(ya_ref, yb_ref, yc_ref, w_ref, ga_ref, gb_ref, gc_ref, o_ref):
    acc = None
    for n, (y_ref, g_ref) in enumerate(((ya_ref, ga_ref), (yb_ref, gb_ref), (yc_ref, gc_ref))):
        z = jax.nn.sigmoid(g_ref[...].astype(F32)) * _dot(y_ref[...], w_ref[n])
        acc = z if acc is None else acc + z
    o_ref[...] = acc.astype(o_ref.dtype)


def _merge(ya, yb, yc, w_branch, layer, p, col_gate, d_model):
    m = ya.shape[0]
    tm = _pick_tile(m, (512, 256, 128))
    tn = _pick_tile(d_model, (1024, 512, 256, 128))
    nj = d_model // tn
    assert (col_gate * LANES) % tn == 0
    yspec = pl.BlockSpec((tm, MIX_W), lambda j, i: (i, 0))
    gspec = lambda n: pl.BlockSpec((tm, tn), lambda j, i, n=n: (i, col_gate * LANES // tn + n * nj + j))
    return pl.pallas_call(
        _merge_kernel,
        grid=(nj, m // tm),
        in_specs=[yspec, yspec, yspec,
                  pl.BlockSpec((None, N_BRANCH, MIX_W, tn), lambda j, i: (layer, 0, 0, j)),
                  gspec(0), gspec(1), gspec(2)],
        out_specs=pl.BlockSpec((tm, tn), lambda j, i: (i, j)),
        out_shape=jax.ShapeDtypeStruct((m, d_model), BF16),
        compiler_params=_params(("arbitrary", "arbitrary")),
        name="branch_merge",
    )(ya, yb, yc, w_branch, p, p, p)


def _proj_ln_kernel(alpha, ctx_len, tiles_per_batch, emit_h, a_ref, w_ref, x_ref, mod_ref, ln_ref, *rest):
    if emit_h:
        xo_ref, ho_ref, acc_ref = rest
    else:
        xo_ref, acc_ref = rest
    kk = pl.program_id(1)

    @pl.when(kk == 0)
    def _():
        acc_ref[...] = jnp.zeros_like(acc_ref)

    acc_ref[...] += _dot(a_ref[...], w_ref[...])

    @pl.when(kk == pl.num_programs(1) - 1)
    def _():
        tm, d = acc_ref.shape
        rb = LN_ROW_BLOCK
        row0 = (pl.program_id(0) % tiles_per_batch) * tm

        def block(i, carry):
            sl = pl.ds(pl.multiple_of(i * rb, rb), rb)
            mod = mod_ref[jnp.where(row0 + i * rb < ctx_len, 1, 0)]
            sel = lambda j: mod[j:j + 1, :]
            y = alpha * x_ref[sl, :] + sel(0) * acc_ref[sl, :]
            mu = jnp.mean(y, axis=-1, keepdims=True)
            yc = y - mu
            var = jnp.mean(jnp.square(yc), axis=-1, keepdims=True)
            xn = yc * lax.rsqrt(var + LN_EPS) * ln_ref[0:1, :] + ln_ref[1:2, :]
            xo_ref[sl, :] = xn
            if emit_h:
                ho_ref[sl, :] = (xn * (1.0 + sel(1)) + sel(2)).astype(BF16)
            return carry

        lax.fori_loop(0, tm // rb, block, 0)


def _proj_ln(a, w, layer, x, mod_sel, ln_gb, alpha, ctx_len, tt, emit_h, name):
    m, k = a.shape
    d = w.shape[2]
    assert w.dtype == BF16
    if k * d * 2 <= RESIDENT_WEIGHT_BYTES:
        tm, tk = _pick_tile(tt, (384, 256, 128)), k
    else:
        tm, tk = _pick_tile(tt, (768, 384, 256, 128)), _pick_tile(k, (1024, 512, 256, 128))
    assert tm % LN_ROW_BLOCK == 0 and ctx_len <= tm and ctx_len % LN_ROW_BLOCK == 0
    tpb = tt // tm
    out_shape = [jax.ShapeDtypeStruct((m, d), F32)]
    out_specs = [pl.BlockSpec((tm, d), lambda i, j: (i, 0))]
    if emit_h:
        out_shape.append(jax.ShapeDtypeStruct((m, d), BF16))
        out_specs.append(pl.BlockSpec((tm, d), lambda i, j: (i, 0)))
    res = pl.pallas_call(
        functools.partial(_proj_ln_kernel, alpha, ctx_len, tpb, emit_h),
        grid=(m // tm, k // tk),
        in_specs=[pl.BlockSpec((tm, tk), lambda i, j: (i, j)),
                  pl.BlockSpec((None, tk, d), lambda i, j: (layer, j, 0)),
                  pl.BlockSpec((tm, d), lambda i, j: (i, 0)),
                  pl.BlockSpec((None, 2, 3, d), lambda i, j: (i // tpb, 0, 0, 0)),
                  pl.BlockSpec((2, d), lambda i, j: (0, 0))],
        out_specs=out_specs,
        out_shape=out_shape,
        scratch_shapes=[pltpu.VMEM((tm, d), F32)],
        compiler_params=_params(("arbitrary", "arbitrary")),
        name=name,
    )(a, w, x, mod_sel, ln_gb)
    return (res[0], res[1]) if emit_h else (res[0], None)


def _rope_tables(seq, ctx_len):
    half = B_DK // 2
    freqs = ROPE_THETA ** (-jnp.arange(0, half, 2, dtype=F32) / half)
    pos = jnp.arange(seq)
    ang_r = (pos // GRID_W).astype(F32)[:, None] * freqs
    ang_c = (pos % GRID_W).astype(F32)[:, None] * freqs
    ang = jnp.concatenate([ang_r, ang_r, ang_c, ang_c], axis=-1)
    ang = jnp.concatenate([jnp.zeros((ctx_len, B_DK), F32), ang], axis=0)
    ang = jnp.concatenate([ang, ang], axis=-1)
    return jnp.cos(ang), jnp.sin(ang)


def kernel(x, c, ctx, c_ctx, w_ada, b_ada, w_in, hgrn_lb_logits, hgrn_norm_g, gla_w_gk2, gla_b_gk2, gla_norm_g,
           natten_rpb, w_branch, w_out, ln1_g, ln1_b, ln2_g, ln2_b, w_mlp1, w_mlp2):
    bsz, seq, d = x.shape
    ctx_len = ctx.shape[1]
    depth = w_ada.shape[0]
    tt = ctx_len + seq
    m_tok = bsz * tt
    alpha = (2 * depth) ** 0.25
    assert ctx_len % SCAN_CHUNK == 0 and seq % SCAN_CHUNK == 0 and seq % GRID_W == 0 and bsz + 1 <= 8

    seg = N_HEADS * A_DK
    gk_lo = 5 * seg + 2 * N_HEADS * B_DK + 2 * MIX_W
    gk_hi = gk_lo + 2 * GK_RANK
    n_cg = 3 * MIX_W + N_BRANCH * d
    assert gk_lo % 1024 == 0 and w_in.shape[2] == gk_hi + n_cg
    col_a = 0
    col_bq = 5 * seg // LANES
    col_bk = col_bq + N_HEADS * B_DK // LANES
    col_bv = col_bk + N_HEADS * B_DK // LANES
    col_bg = col_bv + MIX_W // LANES
    col_cq = 0
    col_ck = col_cq + MIX_W // LANES
    col_cv = col_ck + MIX_W // LANES
    col_gate = col_cv + MIX_W // LANES

    c_rows = jnp.zeros((8, d), F32).at[:bsz].set(c).at[bsz].set(c_ctx)
    mod = _ada_all_layers(c_rows, w_ada, b_ada).reshape(depth, 8, 6, d)

    def mod_sel(l, idx):
        lat = mod[l, :bsz][:, idx, :]
        cx = jnp.broadcast_to(mod[l, bsz][idx, :], lat.shape)
        return jnp.stack([lat, cx], axis=1)

    lb_p = jax.nn.softmax(hgrn_lb_logits.astype(F32), axis=0)
    lb_cum = jnp.cumsum(lb_p, axis=0)
    lower_bounds = jnp.concatenate([jnp.zeros_like(lb_cum[:1]), lb_cum[:-1]], axis=0)

    cos, sin = _rope_tables(seq, ctx_len)
    w_out_bf = w_out.astype(BF16)
    w_mlp2_bf = w_mlp2.astype(BF16)

    xs = jnp.concatenate([ctx, x], axis=1)
    h = _modulate(xs, mod_sel(0, np.array([1, 0])), ctx_len).reshape(m_tok, d)
    xs = xs.reshape(m_tok, d)

    w_in_t = jnp.swapaxes(w_in, 1, 2)
    w_branch_bf = w_branch.astype(BF16)

    for l in range(depth):
        p = _matmul_t(h, w_in_t, l, 0, gk_lo, "in_proj_ab").reshape(bsz, tt, gk_lo)
        p_gk = _matmul_t(h, w_in_t, l, gk_lo, LANES, "in_proj_gk").reshape(bsz, tt, LANES)
        p_cg = _matmul_t(h, w_in_t, l, gk_hi, n_cg, "in_proj_cg").reshape(bsz, tt, n_cg)

        n_pairs = N_HEADS * B_DK // LANES
        wg = jnp.zeros((2, n_pairs, LANES, LANES), F32)
        w2 = gla_w_gk2[l].reshape(2, GK_RANK, n_pairs, LANES).transpose(0, 2, 1, 3)
        wg = wg.at[0, :, :GK_RANK].set(w2[0]).at[1, :, GK_RANK:2 * GK_RANK].set(w2[1]).astype(BF16)
        bg = gla_b_gk2[l].reshape(2, n_pairs, 1, LANES)

        y_a = _hgrn_mixer(p, lower_bounds[l], hgrn_norm_g[l], ctx_len, col_a)
        y_b = _gla_mixer(p, p_gk, wg, bg, cos, sin, gla_norm_g[l], ctx_len, col_bq, col_bk, col_bv, col_bg)
        y_c = _natten_mixer(p_cg, natten_rpb[l], ctx_len, col_cq, col_ck, col_cv)

        merged = _merge(y_a.reshape(m_tok, MIX_W), y_b.reshape(m_tok, MIX_W), y_c.reshape(m_tok, MIX_W),
                        w_branch_bf, l, p_cg.reshape(m_tok, n_cg), col_gate, d)
        xs, h2 = _proj_ln(merged, w_out_bf, l, xs, mod_sel(l, np.array([2, 4, 3])),
                          jnp.stack([ln1_g[l], ln1_b[l]]), alpha, ctx_len, tt, True, "out_proj_ln")
        u = _matmul(h2, w_mlp1, l, sq_relu=True, name="mlp_up")
        last = l == depth - 1
        if last:
            nxt = jnp.zeros((bsz, 2, 3, d), F32).at[:, :, 0].set(mod_sel(l, np.array([5]))[:, :, 0])
        else:
            nxt = jnp.concatenate([mod_sel(l, np.array([5])), mod_sel(l + 1, np.array([1, 0]))], axis=2)
        xs, h = _proj_ln(u, w_mlp2_bf, l, xs, nxt, jnp.stack([ln2_g[l], ln2_b[l]]),
                         alpha, ctx_len, tt, not last, "mlp_down_ln")

    return xs.reshape(bsz, tt, d)[:, ctx_len:, :]
```

```python
import functools

import jax
import jax.numpy as jnp
import numpy as np
from jax import lax
from jax.experimental import pallas as pl
from jax.experimental.pallas import tpu as pltpu

GRID_W = 64
N_BRANCH = 3
MIX_W = 1024
N_HEADS = 8
A_DK = 128
B_DK = 64
HEAD_DV = MIX_W // N_HEADS
GK_RANK = 16
GATE_LOGIT_NORMALIZER = 16.0
KR_MAX = 8
KC = 16
ROPE_THETA = 10000.0
RMS_EPS = 1e-6
LN_EPS = 1e-5
MASK_VALUE = -1e30
LOG2E = 1.4426950408889634

LANES = 128
SCAN_CHUNK = 128
SCAN_UNROLL = 2
HGRN_HEADS_PER_STEP = 2
NATTEN_ROWS_PER_TRIP = 8
LN_ROW_BLOCK = 128
VMEM_LIMIT_BYTES = 56 * 1024 * 1024
RESIDENT_WEIGHT_BYTES = 16 * 1024 * 1024

F32 = jnp.float32
BF16 = jnp.bfloat16


def _pick_tile(n, prefs):
    for t in prefs:
        if n % t == 0:
            return t
    return n


def _params(sem):
    return pltpu.CompilerParams(dimension_semantics=sem, vmem_limit_bytes=VMEM_LIMIT_BYTES)


def _dot(a, b):
    return jnp.dot(a, b, preferred_element_type=F32)


def _dot_nt(a, b):
    return lax.dot_general(a, b, (((1,), (1,)), ((), ())), preferred_element_type=F32)


def _ada_kernel(c_ref, w_ref, b_ref, o_ref):
    c = c_ref[...]
    s = (c * jax.nn.sigmoid(c)).astype(BF16)
    o_ref[...] = _dot(s, w_ref[...].astype(BF16)) + b_ref[...]


def _ada_all_layers(c_rows, w_ada, b_ada):
    depth, d, n = w_ada.shape
    tn = _pick_tile(n, (1024, 512, 256, 128))
    return pl.pallas_call(
        _ada_kernel,
        grid=(depth, n // tn),
        in_specs=[pl.BlockSpec((8, d), lambda l, j: (0, 0)),
                  pl.BlockSpec((None, d, tn), lambda l, j: (l, 0, j)),
                  pl.BlockSpec((None, 1, tn), lambda l, j: (l, 0, j))],
        out_specs=pl.BlockSpec((None, 8, tn), lambda l, j: (l, 0, j)),
        out_shape=jax.ShapeDtypeStruct((depth, 8, n), F32),
        compiler_params=_params(("arbitrary", "arbitrary")),
        name="ada_mod",
    )(c_rows, w_ada, b_ada.reshape(depth, 1, n))


def _modulate_kernel(ctx_len, x_ref, m_ref, h_ref):
    x = x_ref[...]
    rows = lax.broadcasted_iota(jnp.int32, x.shape, 0) + pl.program_id(1) * x.shape[0]
    is_ctx = rows < ctx_len
    sc = jnp.where(is_ctx, m_ref[1, 0:1, :], m_ref[0, 0:1, :])
    sh = jnp.where(is_ctx, m_ref[1, 1:2, :], m_ref[0, 1:2, :])
    h_ref[...] = (x * (1.0 + sc) + sh).astype(BF16)


def _modulate(x, mod_sel, ctx_len):
    b, tt, d = x.shape
    tm = _pick_tile(tt, (768, 384, 256, 128))
    return pl.pallas_call(
        functools.partial(_modulate_kernel, ctx_len),
        grid=(b, tt // tm),
        in_specs=[pl.BlockSpec((None, tm, d), lambda i, j: (i, j, 0)),
                  pl.BlockSpec((None, 2, 2, d), lambda i, j: (i, 0, 0, 0))],
        out_specs=pl.BlockSpec((None, tm, d), lambda i, j: (i, j, 0)),
        out_shape=jax.ShapeDtypeStruct((b, tt, d), BF16),
        compiler_params=_params(("arbitrary", "arbitrary")),
        name="modulate",
    )(x, mod_sel)


def _mm_kernel(sq_relu, a_ref, w_ref, o_ref, *scratch):
    if scratch:
        wb_ref, = scratch

        @pl.when(pl.program_id(1) == 0)
        def _():
            wb_ref[...] = w_ref[...].astype(BF16)
    else:
        wb_ref = w_ref

    r = _dot(a_ref[...], wb_ref[...])
    if sq_relu:
        r = jnp.square(jnp.maximum(r, 0.0))
    o_ref[...] = r.astype(o_ref.dtype)


def _matmul(a, w, layer, col0=0, n=None, sq_relu=False, name="matmul"):
    m, k = a.shape
    n = w.shape[2] if n is None else n
    tm = _pick_tile(m, (1152, 1024, 512, 256, 128))
    tn = _pick_tile(n, (1024, 512, 256, 128))
    assert col0 % tn == 0
    return pl.pallas_call(
        functools.partial(_mm_kernel, sq_relu),
        grid=(n // tn, m // tm),
        in_specs=[pl.BlockSpec((tm, k), lambda j, i: (i, 0)),
                  pl.BlockSpec((None, k, tn), lambda j, i: (layer, 0, col0 // tn + j))],
        out_specs=pl.BlockSpec((tm, tn), lambda j, i: (i, j)),
        out_shape=jax.ShapeDtypeStruct((m, n), BF16),
        scratch_shapes=[] if w.dtype == BF16 else [pltpu.VMEM((k, tn), BF16)],
        compiler_params=_params(("arbitrary", "arbitrary")),
        name=name,
    )(a, w)


def _mm_t_kernel(a_ref, w_ref, o_ref, wb_ref):
    @pl.when(pl.program_id(1) == 0)
    def _():
        wb_ref[...] = w_ref[0].astype(BF16)

    o_ref[...] = _dot_nt(a_ref[...], wb_ref[...]).astype(o_ref.dtype)


def _matmul_t(a, wt, layer, row0, n, name):
    m, k = a.shape
    tm = _pick_tile(m, (1152, 1024, 512, 256, 128))
    tn = _pick_tile(n, (1024, 512, 256, 128))
    assert row0 % 16 == 0 and wt.shape[1] >= row0 + n
    return pl.pallas_call(
        _mm_t_kernel,
        grid=(n // tn, m // tm),
        in_specs=[pl.BlockSpec((tm, k), lambda j, i: (i, 0)),
                  pl.BlockSpec((pl.Element(1), pl.Element(tn), pl.Element(k)),
                               lambda j, i: (layer, pl.multiple_of(row0 + j * tn, 16), 0))],
        out_specs=pl.BlockSpec((tm, tn), lambda j, i: (i, j)),
        out_shape=jax.ShapeDtypeStruct((m, n), BF16),
        scratch_shapes=[pltpu.VMEM((tn, k), BF16)],
        compiler_params=_params(("arbitrary", "arbitrary")),
        name=name,
    )(a, wt)


def _level_operand(q, k, la, bc, m, rev):
    c, dk = q.shape
    if m >= 8:
        pieces = []
        for jb in range(c // m):
            rows = slice(jb * m, (jb + 1) * m)
            r = (jb // 2) * 2 * m + (m if rev else m - 1)
            cvec = bc[r:r + 1, :]
            if (jb % 2 == 1) != rev:
                pieces.append(q[rows, :] * jnp.exp2(bc[rows, :] - cvec))
            else:
                pieces.append(k[rows, :] * jnp.exp2(cvec - bc[rows, :]))
        return jnp.concatenate(pieces, axis=0)
    shape3 = (c // 8, 8, dk)
    sub = lax.broadcasted_iota(jnp.int32, (1, 8, dk), 1)
    upper = (sub & m) != 0
    q_role = jnp.logical_not(upper) if rev else upper
    q3, k3, bc3 = q.reshape(shape3), k.reshape(shape3), bc.reshape(shape3)
    if m == 1:
        e2 = jnp.where(q_role, la.reshape(shape3), 0.0)
    else:
        if m == 4:
            r = 4 if rev else 3
            c3 = bc3[:, r:r + 1, :]
        else:
            r0, r1 = (2, 6) if rev else (1, 5)
            c3 = jnp.where(sub < 4, bc3[:, r0:r0 + 1, :], bc3[:, r1:r1 + 1, :])
        e2 = (bc3 - c3) * jnp.where(q_role, 1.0, -1.0)
    return (jnp.where(q_role, q3, k3) * jnp.exp2(e2)).reshape(c, dk)


def _level_matrix(c, rev):
    t = lax.broadcasted_iota(jnp.int32, (c, c), 0)
    s = lax.broadcasted_iota(jnp.int32, (c, c), 1)
    x = t ^ s
    lvl = jnp.full((c, c), -1, jnp.int32)
    b = 1
    while b < c:
        lvl = lvl + (x >= b).astype(jnp.int32)
        b *= 2
    return jnp.where((t < s) if rev else (t > s), lvl, -1)


def _scan_both_directions(prep_chunk, q_s, kf_s, kb_s, laf_s, lab_s, v_ref, o_s, st_s, bc_s, vt_s, sc_s, n_ctx_chunks):
    c = SCAN_CHUNK
    n_chunks = q_s.shape[0] // c
    nh = st_s.shape[1]
    dv = v_ref.shape[1] // nh
    ti = lax.broadcasted_iota(jnp.int32, (c, c), 0)
    si = lax.broadcasted_iota(jnp.int32, (c, c), 1)
    lane = lax.broadcasted_iota(jnp.int32, (c, LANES), 1) // (LANES // nh)
    head_lanes = [(lane == h, (lane == h).astype(BF16)) for h in range(nh)]
    k_refs, la_refs = (kf_s, kb_s), (laf_s, lab_s)
    tris = ((si <= ti).astype(BF16), (si >= ti).astype(BF16))
    lvls = (_level_matrix(c, False), _level_matrix(c, True))

    def cumulate(i, carry):
        sl = pl.ds(pl.multiple_of(i * c, c), c)
        las = prep_chunk(sl)
        for h in range(nh):
            vt_s[i, h * dv:(h + 1) * dv, :] = v_ref[sl, h * dv:(h + 1) * dv].T
        for d in range(2):
            la_hi = las[d].astype(BF16)
            la_lo = (las[d] - la_hi.astype(F32)).astype(BF16)
            bb = _dot(tris[d], jnp.concatenate([la_hi, la_lo], axis=1))
            bc_s[d, sl, :] = (bb[:, :LANES] + bb[:, LANES:]) * LOG2E
            la_refs[d][sl, :] = las[d] * LOG2E
        return carry

    lax.fori_loop(0, n_chunks, cumulate, 0, unroll=3)

    st_s[...] = jnp.zeros_like(st_s)
    nu = SCAN_UNROLL
    n_groups = n_chunks // nu
    assert n_chunks % nu == 0 and n_groups >= 2
    streams = [(d, u) for u in range(nu) for d in range(2)]
    slot = {(su, h): (su[0] * nu + su[1]) * nh + h for su in streams for h in range(nh)}

    def group_rows(g):
        seq = [g * nu + u for u in range(nu)]
        cis = [seq, [jnp.where(j < n_ctx_chunks, n_ctx_chunks - 1 - j, n_chunks - 1 + n_ctx_chunks - j) for j in seq]]
        return cis, {(d, u): pl.ds(pl.multiple_of(cis[d][u] * c, c), c) for d, u in streams}

    def state_phase(g):
        cis, sl = group_rows(g)
        q = {su: q_s[sl[su], :] for su in streams}
        k = {(d, u): k_refs[d][sl[d, u], :] for d, u in streams}
        la = {(d, u): la_refs[d][sl[d, u], :] for d, u in streams}
        bc = {(d, u): bc_s[d, sl[d, u], :] for d, u in streams}
        st = [[st_s[d, h] for h in range(nh)] for d in range(2)]
        inter = {}
        for d, u in streams:
            su = (d, u)
            qg = (q[su] * jnp.exp2(bc[su])).astype(BF16)
            tot = bc[su][0:1, :] if d else bc[su][c - 1:c, :]
            kg = (k[su] * jnp.exp2(tot - bc[su])).astype(BF16)
            dec = jnp.exp2(tot)
            for h in range(nh):
                qg_h = qg if nh == 1 else qg * head_lanes[h][1]
                inter[su, h] = _dot_nt(qg_h, st[d][h].astype(BF16))
                st[d][h] = st[d][h] * dec + _dot(vt_s[cis[d][u], h * dv:(h + 1) * dv, :], kg)
        for d in range(2):
            for h in range(nh):
                st_s[d, h] = st[d][h]
        return sl, q, k, la, bc, inter

    def score_matmuls(g):
        _, sl = group_rows(g)
        return sl, {(su, h): _dot(sc_s[slot[su, h]], v_ref[sl[su], h * dv:(h + 1) * dv])
                    for su in streams for h in range(nh)}

    def level_phase(sl, q, k, la, bc, inter):
        scores = {key: jnp.zeros((c, c), F32) for key in slot}
        m, level = c // 2, (c // 2).bit_length() - 1
        while m >= 1:
            for d, u in streams:
                su = (d, u)
                x = _level_operand(q[su], k[su], la[su], bc[su], m, d == 1).astype(BF16)
                x_heads = x if nh == 1 else jnp.concatenate([x * head_lanes[h][1] for h in range(nh)], axis=0)
                s_all = _dot_nt(x, x_heads)
                for h in range(nh):
                    scores[su, h] = jnp.where(lvls[d] == level, s_all[:, h * c:(h + 1) * c], scores[su, h])
            m, level = m // 2, level - 1
        for d, u in streams:
            su = (d, u)
            qk = q[su] * k[su]
            for h in range(nh):
                sc_s[slot[su, h]] = scores[su, h].astype(BF16)
                v = v_ref[sl[su], h * dv:(h + 1) * dv].astype(F32)
                qk_h = qk if nh == 1 else jnp.where(head_lanes[h][0], qk, 0.0)
                o_s[d, sl[su], h * dv:(h + 1) * dv] = inter[su, h] + jnp.sum(qk_h, axis=1, keepdims=True) * v

    def add_scores_part(sl, outs):
        for (su, h), o in outs.items():
            cols = slice(h * dv, (h + 1) * dv)
            o_s[su[0], sl[su], cols] = o_s[su[0], sl[su], cols] + o

    level_phase(*state_phase(0))

    def body(g, carry):
        cur = state_phase(g)
        sl_prev, outs = score_matmuls(g - 1)
        level_phase(*cur)
        add_scores_part(sl_prev, outs)
        return carry

    lax.fori_loop(1, n_groups, body, 0)
    add_scores_part(*score_matmuls(n_groups - 1))


def _scan_scratch(tt, nh):
    n_chunks = tt // SCAN_CHUNK
    return [pltpu.VMEM((2, tt, nh * HEAD_DV), F32), pltpu.VMEM((2, nh, HEAD_DV, LANES), F32),
            pltpu.VMEM((2, tt, LANES), F32), pltpu.VMEM((n_chunks, nh * HEAD_DV, SCAN_CHUNK), BF16),
            pltpu.VMEM((2 * SCAN_UNROLL * nh, SCAN_CHUNK, SCAN_CHUNK), BF16)]


def _rms_gate(o, gain, gate):
    o = o * lax.rsqrt(jnp.mean(jnp.square(o), axis=-1, keepdims=True) + RMS_EPS) * gain
    return o * (gate * jax.nn.sigmoid(gate))


def _hgrn_kernel(n_ctx_chunks, q_ref, i_ref, g_ref, ff_ref, fb_ref, lb_ref, gain_ref, y_ref,
                 q_s, kf_s, kb_s, laf_s, lab_s, *scan_s):
    c = SCAN_CHUNK
    n_chunks = q_s.shape[0] // c
    o_s = scan_s[0]

    for hh in range(q_ref.shape[1] // LANES):
        cols = slice(hh * LANES, (hh + 1) * LANES)
        lb_f = lb_ref[0:1, cols]
        lb_b = lb_ref[1:2, cols]

        def prep_chunk(sl, cols=cols, lb_f=lb_f, lb_b=lb_b):
            q = q_ref[sl, cols].astype(F32)
            q_s[sl, :] = q * jax.nn.sigmoid(q) * (A_DK ** -0.5)
            las = []
            for f_ref, lb, k_s in ((ff_ref, lb_f, kf_s), (fb_ref, lb_b, kb_s)):
                s = jax.nn.sigmoid(f_ref[sl, cols].astype(F32))
                k_s[sl, :] = (1.0 - lb) * (1.0 - s)
                las.append(jnp.log(lb + (1.0 - lb) * s))
            return las

        _scan_both_directions(prep_chunk, q_s, kf_s, kb_s, laf_s, lab_s, i_ref.at[:, cols], *scan_s, n_ctx_chunks)

        def fin(i, carry, cols=cols):
            sl = pl.ds(pl.multiple_of(i * c, c), c)
            o = o_s[0, sl, :] + o_s[1, sl, :]
            y_ref[sl, cols] = _rms_gate(o, gain_ref[...], g_ref[sl, cols].astype(F32)).astype(y_ref.dtype)
            return carry

        lax.fori_loop(0, n_chunks, fin, 0, unroll=2)


def _hgrn_mixer(p, lb, gain, ctx_len, col0):
    b, tt, _ = p.shape
    hps = HGRN_HEADS_PER_STEP
    wide = hps * LANES
    assert N_HEADS % hps == 0 and (col0 * LANES) % wide == 0
    blk = lambda seg: pl.BlockSpec((None, tt, wide),
                                   lambda i, h, s=seg: (i, 0, (col0 + s * N_HEADS) // hps + h))
    seq = lambda: pltpu.VMEM((tt, LANES), F32)
    return pl.pallas_call(
        functools.partial(_hgrn_kernel, ctx_len // SCAN_CHUNK),
        grid=(b, N_HEADS // hps),
        in_specs=[blk(0), blk(1), blk(2), blk(3), blk(4),
                  pl.BlockSpec((2, wide), lambda i, h: (0, h)),
                  pl.BlockSpec((1, LANES), lambda i, h: (0, 0))],
        out_specs=pl.BlockSpec((None, tt, wide), lambda i, h: (i, 0, h)),
        out_shape=jax.ShapeDtypeStruct((b, tt, MIX_W), BF16),
        scratch_shapes=[seq(), seq(), seq(), seq(), seq()] + _scan_scratch(tt, 1),
        compiler_params=_params(("arbitrary", "arbitrary")),
        name="hgrn2_mixer",
    )(p, p, p, p, p, lb, gain.reshape(1, HEAD_DV))


def _rope_rotate(x):
    n = x.shape[-1]
    lane = lax.broadcasted_iota(jnp.int32, x.shape, x.ndim - 1)
    first = (lane % 32) < 16
    return jnp.where(first, -pltpu.roll(x, n - 16, x.ndim - 1), pltpu.roll(x, 16, x.ndim - 1))


def _gla_kernel(n_ctx_chunks, q_ref, k_ref, v_ref, g_ref, gk_ref, wg_ref, bg_ref, cos_ref, sin_ref,
                gain_ref, y_ref, q_s, k_s, laf_s, lab_s, *scan_s):
    c = SCAN_CHUNK
    n_chunks = q_s.shape[0] // c
    nh = LANES // B_DK
    o_s = scan_s[0]

    def prep_chunk(sl):
        cos = cos_ref[sl, :]
        sin = sin_ref[sl, :]
        q = q_ref[sl, :].astype(F32)
        k = k_ref[sl, :].astype(F32)
        q_s[sl, :] = (q * cos + _rope_rotate(q) * sin) * (B_DK ** -0.5)
        k_s[sl, :] = k * cos + _rope_rotate(k) * sin
        gk = gk_ref[sl, :]
        return [jax.nn.log_sigmoid(_dot(gk, wg_ref[d]) + bg_ref[d]) / GATE_LOGIT_NORMALIZER for d in range(2)]

    _scan_both_directions(prep_chunk, q_s, k_s, k_s, laf_s, lab_s, v_ref, *scan_s, n_ctx_chunks)

    def fin(i, carry):
        sl = pl.ds(pl.multiple_of(i * c, c), c)
        for h in range(nh):
            cols = slice(h * HEAD_DV, (h + 1) * HEAD_DV)
            o = o_s[0, sl, cols] + o_s[1, sl, cols]
            y_ref[sl, cols] = _rms_gate(o, gain_ref[...], g_ref[sl, cols].astype(F32)).astype(y_ref.dtype)
        return carry

    lax.fori_loop(0, n_chunks, fin, 0, unroll=2)


def _gla_mixer(p, p_gk, wg, bg, cos, sin, gain, ctx_len, col_q, col_k, col_v, col_g):
    b, tt, _ = p.shape
    nh = LANES // B_DK
    wide = nh * HEAD_DV
    pair = lambda c0: pl.BlockSpec((None, tt, LANES), lambda i, h: (i, 0, c0 + h))
    pair_v = lambda c0: pl.BlockSpec((None, tt, wide), lambda i, h: (i, 0, c0 * LANES // wide + h))
    seq = lambda: pltpu.VMEM((tt, LANES), F32)
    assert (col_v * LANES) % wide == 0 and (col_g * LANES) % wide == 0
    return pl.pallas_call(
        functools.partial(_gla_kernel, ctx_len // SCAN_CHUNK),
        grid=(b, N_HEADS // nh),
        in_specs=[pair(col_q), pair(col_k), pair_v(col_v), pair_v(col_g),
                  pl.BlockSpec((None, tt, LANES), lambda i, h: (i, 0, 0)),
                  pl.BlockSpec((2, None, LANES, LANES), lambda i, h: (0, h, 0, 0)),
                  pl.BlockSpec((2, None, 1, LANES), lambda i, h: (0, h, 0, 0)),
                  pl.BlockSpec((tt, LANES), lambda i, h: (0, 0)),
                  pl.BlockSpec((tt, LANES), lambda i, h: (0, 0)),
                  pl.BlockSpec((1, LANES), lambda i, h: (0, 0))],
        out_specs=pl.BlockSpec((None, tt, wide), lambda i, h: (i, 0, h)),
        out_shape=jax.ShapeDtypeStruct((b, tt, MIX_W), BF16),
        scratch_shapes=[seq(), seq(), seq(), seq()] + _scan_scratch(tt, nh),
        compiler_params=_params(("arbitrary", "arbitrary")),
        name="gla_mixer",
    )(p, p, p, p, p_gk, wg, bg, cos, sin, gain.reshape(1, HEAD_DV))


def _natten_kernel(ctx_len, rows, kr, q_ref, k_ref, v_ref, rpb_ref, y_ref, bias_ref):
    scale = HEAD_DV ** -0.5
    kc = k_ref[0:ctx_len, :]
    vc = v_ref[0:ctx_len, :]

    @pl.when(pl.program_id(1) == 0)
    def _():
        qi = lax.broadcasted_iota(jnp.int32, (GRID_W, LANES), 0)
        kj = lax.broadcasted_iota(jnp.int32, (GRID_W, LANES), 1) % GRID_W
        col_start = jnp.clip(qi - KC // 2, 0, GRID_W - KC)
        col_ok = (kj >= col_start) & (kj < col_start + KC)
        for v in range(kr):
            for blk in range(kr * GRID_W // LANES):
                d = v + blk * (LANES // GRID_W)
                src = jnp.broadcast_to(rpb_ref[d:d + 1, :], (GRID_W, LANES))
                rot = pltpu.roll(src, LANES - (KC - 1), 1, stride=1, stride_axis=0)
                bias_ref[v, :, blk * LANES:(blk + 1) * LANES] = jnp.where(col_ok, rot, MASK_VALUE)

    s = _dot_nt(q_ref[0:ctx_len, :], kc) * scale
    p = jnp.exp(s - jnp.max(s, axis=-1, keepdims=True))
    o = _dot(p.astype(BF16), vc) / jnp.sum(p, axis=-1, keepdims=True)
    y_ref[0:ctx_len, :] = o.astype(y_ref.dtype)

    nr = NATTEN_ROWS_PER_TRIP
    assert rows % nr == 0

    def body(i, carry):
        q_sl, b_sl, s_b, s_c = [], [], [], []
        for u in range(nr):
            r = i * nr + u
            start = jnp.clip(r - kr // 2, 0, rows - kr)
            q_sl.append(pl.ds(pl.multiple_of(ctx_len + r * GRID_W, GRID_W), GRID_W))
            b_sl.append(pl.ds(pl.multiple_of(ctx_len + start * GRID_W, GRID_W), kr * GRID_W))
            q = q_ref[q_sl[u], :]
            s_b.append(_dot_nt(q, k_ref[b_sl[u], :]) * scale + bias_ref[start - r + (KR_MAX - 1)])
            s_c.append(_dot_nt(q, kc) * scale)
        for u in range(nr):
            mx = jnp.maximum(jnp.max(s_b[u], axis=-1, keepdims=True), jnp.max(s_c[u], axis=-1, keepdims=True))
            p_b = jnp.exp(s_b[u] - mx)
            p_c = jnp.exp(s_c[u] - mx)
            den = jnp.sum(p_b, axis=-1, keepdims=True) + jnp.sum(p_c, axis=-1, keepdims=True)
            o = (_dot(p_b.astype(BF16), v_ref[b_sl[u], :]) + _dot(p_c.astype(BF16), vc)) / den
            y_ref[q_sl[u], :] = o.astype(y_ref.dtype)
        return carry

    lax.fori_loop(0, rows // nr, body, 0)


def _natten_mixer(p, rpb, ctx_len, col_q, col_k, col_v):
    b, tt, _ = p.shape
    rows = (tt - ctx_len) // GRID_W
    kr = KR_MAX
    assert rows >= KR_MAX and rows % 2 == 0 and 2 * GRID_W == LANES and 2 * KC - 1 <= GRID_W
    padded = jnp.pad(rpb.astype(F32), ((0, 0), (0, 0), (0, GRID_W - (2 * KC - 1))))
    pairs = jnp.concatenate([padded[:, :-1], padded[:, 1:]], axis=-1)
    head = lambda c0: pl.BlockSpec((None, tt, LANES), lambda h, i: (i, 0, c0 + h))
    return pl.pallas_call(
        functools.partial(_natten_kernel, ctx_len, rows, kr),
        grid=(N_HEADS, b),
        in_specs=[head(col_q), head(col_k), head(col_v),
                  pl.BlockSpec((None, 2 * KR_MAX - 2, LANES), lambda h, i: (h, 0, 0))],
        out_specs=pl.BlockSpec((None, tt, LANES), lambda h, i: (i, 0, h)),
        out_shape=jax.ShapeDtypeStruct((b, tt, MIX_W), BF16),
        scratch_shapes=[pltpu.VMEM((kr, GRID_W, kr * GRID_W), F32)],
        compiler_params=_params(("arbitrary", "arbitrary")),
        name="natten_mixer",
    )(p, p, p, pairs)


def _merge_kernel(ya_ref, yb_ref, yc_ref, w_ref, ga_ref, gb_ref, gc_ref, o_ref):
    acc = None
    for n, (y_ref, g_ref) in enumerate(((ya_ref, ga_ref), (yb_ref, gb_ref), (yc_ref, gc_ref))):
        z = jax.nn.sigmoid(g_ref[...].astype(F32)) * _dot(y_ref[...], w_ref[n])
        acc = z if acc is None else acc + z
    o_ref[...] = acc.astype(o_ref.dtype)


def _merge(ya, yb, yc, w_branch, layer, p, col_gate, d_model):
    m = ya.shape[0]
    tm = _pick_tile(m, (512, 256, 128))
    tn = _pick_tile(d_model, (1024, 512, 256, 128))
    nj = d_model // tn
    assert (col_gate * LANES) % tn == 0
    yspec = pl.BlockSpec((tm, MIX_W), lambda j, i: (i, 0))
    gspec = lambda n: pl.BlockSpec((tm, tn), lambda j, i, n=n: (i, col_gate * LANES // tn + n * nj + j))
    return pl.pallas_call(
        _merge_kernel,
        grid=(nj, m // tm),
        in_specs=[yspec, yspec, yspec,
                  pl.BlockSpec((None, N_BRANCH, MIX_W, tn), lambda j, i: (layer, 0, 0, j)),
                  gspec(0), gspec(1), gspec(2)],
        out_specs=pl.BlockSpec((tm, tn), lambda j, i: (i, j)),
        out_shape=jax.ShapeDtypeStruct((m, d_model), BF16),
        compiler_params=_params(("arbitrary", "arbitrary")),
        name="branch_merge",
    )(ya, yb, yc, w_branch, p, p, p)


def _proj_ln_kernel(alpha, ctx_len, tiles_per_batch, emit_h, a_ref, w_ref, x_ref, mod_ref, ln_ref, *rest):
    if emit_h:
        xo_ref, ho_ref, acc_ref = rest
    else:
        xo_ref, acc_ref = rest
    kk = pl.program_id(1)

    @pl.when(kk == 0)
    def _():
        acc_ref[...] = jnp.zeros_like(acc_ref)

    acc_ref[...] += _dot(a_ref[...], w_ref[...])

    @pl.when(kk == pl.num_programs(1) - 1)
    def _():
        tm, d = acc_ref.shape
        rb = LN_ROW_BLOCK
        row0 = (pl.program_id(0) % tiles_per_batch) * tm

        def block(i, carry):
            sl = pl.ds(pl.multiple_of(i * rb, rb), rb)
            mod = mod_ref[jnp.where(row0 + i * rb < ctx_len, 1, 0)]
            sel = lambda j: mod[j:j + 1, :]
            y = alpha * x_ref[sl, :] + sel(0) * acc_ref[sl, :]
            mu = jnp.mean(y, axis=-1, keepdims=True)
            yc = y - mu
            var = jnp.mean(jnp.square(yc), axis=-1, keepdims=True)
            xn = yc * lax.rsqrt(var + LN_EPS) * ln_ref[0:1, :] + ln_ref[1:2, :]
            xo_ref[sl, :] = xn
            if emit_h:
                ho_ref[sl, :] = (xn * (1.0 + sel(1)) + sel(2)).astype(BF16)
            return carry

        lax.fori_loop(0, tm // rb, block, 0)


def _proj_ln(a, w, layer, x, mod_sel, ln_gb, alpha, ctx_len, tt, emit_h, name):
    m, k = a.shape
    d = w.shape[2]
    assert w.dtype == BF16
    if k * d * 2 <= RESIDENT_WEIGHT_BYTES:
        tm, tk = _pick_tile(tt, (384, 256, 128)), k
    else:
        tm, tk = _pick_tile(tt, (768, 384, 256, 128)), _pick_tile(k, (1024, 512, 256, 128))
    assert tm % LN_ROW_BLOCK == 0 and ctx_len <= tm and ctx_len % LN_ROW_BLOCK == 0
    tpb = tt // tm
    out_shape = [jax.ShapeDtypeStruct((m, d), F32)]
    out_specs = [pl.BlockSpec((tm, d), lambda i, j: (i, 0))]
    if emit_h:
        out_shape.append(jax.ShapeDtypeStruct((m, d), BF16))
        out_specs.append(pl.BlockSpec((tm, d), lambda i, j: (i, 0)))
    res = pl.pallas_call(
        functools.partial(_proj_ln_kernel, alpha, ctx_len, tpb, emit_h),
        grid=(m // tm, k // tk),
        in_specs=[pl.BlockSpec((tm, tk), lambda i, j: (i, j)),
                  pl.BlockSpec((None, tk, d), lambda i, j: (layer, j, 0)),
                  pl.BlockSpec((tm, d), lambda i, j: (i, 0)),
                  pl.BlockSpec((None, 2, 3, d), lambda i, j: (i // tpb, 0, 0, 0)),
                  pl.BlockSpec((2, d), lambda i, j: (0, 0))],
        out_specs=out_specs,
        out_shape=out_shape,
        scratch_shapes=[pltpu.VMEM((tm, d), F32)],
        compiler_params=_params(("arbitrary", "arbitrary")),
        name=name,
    )(a, w, x, mod_sel, ln_gb)
    return (res[0], res[1]) if emit_h else (res[0], None)


def _rope_tables(seq, ctx_len):
    half = B_DK // 2
    freqs = ROPE_THETA ** (-jnp.arange(0, half, 2, dtype=F32) / half)
    pos = jnp.arange(seq)
    ang_r = (pos // GRID_W).astype(F32)[:, None] * freqs
    ang_c = (pos % GRID_W).astype(F32)[:, None] * freqs
    ang = jnp.concatenate([ang_r, ang_r, ang_c, ang_c], axis=-1)
    ang = jnp.concatenate([jnp.zeros((ctx_len, B_DK), F32), ang], axis=0)
    ang = jnp.concatenate([ang, ang], axis=-1)
    return jnp.cos(ang), jnp.sin(ang)


def kernel(x, c, ctx, c_ctx, w_ada, b_ada, w_in, hgrn_lb_logits, hgrn_norm_g, gla_w_gk2, gla_b_gk2, gla_norm_g,
           natten_rpb, w_branch, w_out, ln1_g, ln1_b, ln2_g, ln2_b, w_mlp1, w_mlp2):
    bsz, seq, d = x.shape
    ctx_len = ctx.shape[1]
    depth = w_ada.shape[0]
    tt = ctx_len + seq
    m_tok = bsz * tt
    alpha = (2 * depth) ** 0.25
    assert ctx_len % SCAN_CHUNK == 0 and seq % SCAN_CHUNK == 0 and seq % GRID_W == 0 and bsz + 1 <= 8

    seg = N_HEADS * A_DK
    gk_lo = 5 * seg + 2 * N_HEADS * B_DK + 2 * MIX_W
    gk_hi = gk_lo + 2 * GK_RANK
    n_cg = 3 * MIX_W + N_BRANCH * d
    assert gk_lo % 1024 == 0 and w_in.shape[2] == gk_hi + n_cg
    col_a = 0
    col_bq = 5 * seg // LANES
    col_bk = col_bq + N_HEADS * B_DK // LANES
    col_bv = col_bk + N_HEADS * B_DK // LANES
    col_bg = col_bv + MIX_W // LANES
    col_cq = 0
    col_ck = col_cq + MIX_W // LANES
    col_cv = col_ck + MIX_W // LANES
    col_gate = col_cv + MIX_W // LANES

    c_rows = jnp.zeros((8, d), F32).at[:bsz].set(c).at[bsz].set(c_ctx)
    mod = _ada_all_layers(c_rows, w_ada, b_ada).reshape(depth, 8, 6, d)

    def mod_sel(l, idx):
        lat = mod[l, :bsz][:, idx, :]
        cx = jnp.broadcast_to(mod[l, bsz][idx, :], lat.shape)
        return jnp.stack([lat, cx], axis=1)

    lb_p = jax.nn.softmax(hgrn_lb_logits.astype(F32), axis=0)
    lb_cum = jnp.cumsum(lb_p, axis=0)
    lower_bounds = jnp.concatenate([jnp.zeros_like(lb_cum[:1]), lb_cum[:-1]], axis=0)

    cos, sin = _rope_tables(seq, ctx_len)
    w_out_bf = w_out.astype(BF16)
    w_mlp2_bf = w_mlp2.astype(BF16)

    xs = jnp.concatenate([ctx, x], axis=1)
    h = _modulate(xs, mod_sel(0, np.array([1, 0])), ctx_len).reshape(m_tok, d)
    xs = xs.reshape(m_tok, d)

    w_in_t = jnp.swapaxes(w_in, 1, 2)
    w_branch_bf = w_branch.astype(BF16)

    for l in range(depth):
        p = _matmul_t(h, w_in_t, l, 0, gk_lo, "in_proj_ab").reshape(bsz, tt, gk_lo)
        p_gk = _matmul_t(h, w_in_t, l, gk_lo, LANES, "in_proj_gk").reshape(bsz, tt, LANES)
        p_cg = _matmul_t(h, w_in_t, l, gk_hi, n_cg, "in_proj_cg").reshape(bsz, tt, n_cg)

        n_pairs = N_HEADS * B_DK // LANES
        wg = jnp.zeros((2, n_pairs, LANES, LANES), F32)
        w2 = gla_w_gk2[l].reshape(2, GK_RANK, n_pairs, LANES).transpose(0, 2, 1, 3)
        wg = wg.at[0, :, :GK_RANK].set(w2[0]).at[1, :, GK_RANK:2 * GK_RANK].set(w2[1]).astype(BF16)
        bg = gla_b_gk2[l].reshape(2, n_pairs, 1, LANES)

        y_a = _hgrn_mixer(p, lower_bounds[l], hgrn_norm_g[l], ctx_len, col_a)
        y_b = _gla_mixer(p, p_gk, wg, bg, cos, sin, gla_norm_g[l], ctx_len, col_bq, col_bk, col_bv, col_bg)
        y_c = _natten_mixer(p_cg, natten_rpb[l], ctx_len, col_cq, col_ck, col_cv)

        merged = _merge(y_a.reshape(m_tok, MIX_W), y_b.reshape(m_tok, MIX_W), y_c.reshape(m_tok, MIX_W),
                        w_branch_bf, l, p_cg.reshape(m_tok, n_cg), col_gate, d)
        xs, h2 = _proj_ln(merged, w_out_bf, l, xs, mod_sel(l, np.array([2, 4, 3])),
                          jnp.stack([ln1_g[l], ln1_b[l]]), alpha, ctx_len, tt, True, "out_proj_ln")
        u = _matmul(h2, w_mlp1, l, sq_relu=True, name="mlp_up")
        last = l == depth - 1
        if last:
            nxt = jnp.zeros((bsz, 2, 3, d), F32).at[:, :, 0].set(mod_sel(l, np.array([5]))[:, :, 0])
        else:
            nxt = jnp.concatenate([mod_sel(l, np.array([5])), mod_sel(l + 1, np.array([1, 0]))], axis=2)
        xs, h = _proj_ln(u, w_mlp2_bf, l, xs, nxt, jnp.stack([ln2_g[l], ln2_b[l]]),
                         alpha, ctx_len, tt, not last, "mlp_down_ln")

    return xs.reshape(bsz, tt, d)[:, ctx_len:, :]
```

```python
import functools

import jax
import jax.numpy as jnp
import numpy as np
from jax import lax
from jax.experimental import pallas as pl
from jax.experimental.pallas import tpu as pltpu

GRID_W = 64
N_BRANCH = 3
MIX_W = 1024
N_HEADS = 8
A_DK = 128
B_DK = 64
HEAD_DV = MIX_W // N_HEADS
GK_RANK = 16
GATE_LOGIT_NORMALIZER = 16.0
KR_MAX = 8
KC = 16
ROPE_THETA = 10000.0
RMS_EPS = 1e-6
LN_EPS = 1e-5
MASK_VALUE = -1e30
LOG2E = 1.4426950408889634

LANES = 128
SUBLANES = 8
BF16_ROWS_PER_TILE = 16
SCAN_CHUNK = 128
SCAN_UNROLL = 2
NATTEN_ROWS_PER_TRIP = 8
LN_ROW_BLOCK = 128
VMEM_LIMIT_BYTES = 56 * 1024 * 1024
RESIDENT_WEIGHT_BYTES = 16 * 1024 * 1024
MATMUL_ROW_TILES = (1536, 1152, 1024, 512, 256, 128)
MATMUL_COL_TILES = (1024, 512, 256, 128)

F32 = jnp.float32
BF16 = jnp.bfloat16


def _pick_tile(n, prefs):
    for t in prefs:
        if n % t == 0:
            return t
    return n


def _params(sem):
    return pltpu.CompilerParams(dimension_semantics=sem, vmem_limit_bytes=VMEM_LIMIT_BYTES)


def _dot(a, b):
    return jnp.dot(a, b, preferred_element_type=F32)


def _dot_nt(a, b):
    return lax.dot_general(a, b, (((1,), (1,)), ((), ())), preferred_element_type=F32)


def _ada_kernel(c_ref, w_ref, b_ref, o_ref):
    c = c_ref[...]
    s = (c * jax.nn.sigmoid(c)).astype(BF16)
    o_ref[...] = _dot(s, w_ref[...].astype(BF16)) + b_ref[...]


def _ada_all_layers(c_rows, w_ada, b_ada):
    depth, d, n = w_ada.shape
    tn = _pick_tile(n, MATMUL_COL_TILES)
    return pl.pallas_call(
        _ada_kernel,
        grid=(depth, n // tn),
        in_specs=[pl.BlockSpec((SUBLANES, d), lambda l, j: (0, 0)),
                  pl.BlockSpec((None, d, tn), lambda l, j: (l, 0, j)),
                  pl.BlockSpec((None, 1, tn), lambda l, j: (l, 0, j))],
        out_specs=pl.BlockSpec((None, SUBLANES, tn), lambda l, j: (l, 0, j)),
        out_shape=jax.ShapeDtypeStruct((depth, SUBLANES, n), F32),
        compiler_params=_params(("arbitrary", "arbitrary")),
        name="ada_mod",
    )(c_rows, w_ada, b_ada.reshape(depth, 1, n))


def _modulate_kernel(ctx_len, x_ref, m_ref, h_ref):
    x = x_ref[...]
    rows = lax.broadcasted_iota(jnp.int32, x.shape, 0) + pl.program_id(1) * x.shape[0]
    is_ctx = rows < ctx_len
    sc = jnp.where(is_ctx, m_ref[1, 0:1, :], m_ref[0, 0:1, :])
    sh = jnp.where(is_ctx, m_ref[1, 1:2, :], m_ref[0, 1:2, :])
    h_ref[...] = (x * (1.0 + sc) + sh).astype(BF16)


def _modulate(x, mod_sel, ctx_len):
    b, tt, d = x.shape
    tm = _pick_tile(tt, (768, 384, 256, 128))
    return pl.pallas_call(
        functools.partial(_modulate_kernel, ctx_len),
        grid=(b, tt // tm),
        in_specs=[pl.BlockSpec((None, tm, d), lambda i, j: (i, j, 0)),
                  pl.BlockSpec((None, 2, 2, d), lambda i, j: (i, 0, 0, 0))],
        out_specs=pl.BlockSpec((None, tm, d), lambda i, j: (i, j, 0)),
        out_shape=jax.ShapeDtypeStruct((b, tt, d), BF16),
        compiler_params=_params(("arbitrary", "arbitrary")),
        name="modulate",
    )(x, mod_sel)


def _mm_sq_relu_kernel(a_ref, w_ref, o_ref, wb_ref):
    @pl.when(pl.program_id(1) == 0)
    def _():
        wb_ref[...] = w_ref[...].astype(BF16)

    o_ref[...] = jnp.square(jnp.maximum(_dot(a_ref[...], wb_ref[...]), 0.0)).astype(o_ref.dtype)


def _matmul_sq_relu(a, w, layer, name):
    m, k = a.shape
    n = w.shape[2]
    tm = _pick_tile(m, MATMUL_ROW_TILES)
    tn = _pick_tile(n, MATMUL_COL_TILES)
    return pl.pallas_call(
        _mm_sq_relu_kernel,
        grid=(n // tn, m // tm),
        in_specs=[pl.BlockSpec((tm, k), lambda j, i: (i, 0)),
                  pl.BlockSpec((None, k, tn), lambda j, i: (layer, 0, j))],
        out_specs=pl.BlockSpec((tm, tn), lambda j, i: (i, j)),
        out_shape=jax.ShapeDtypeStruct((m, n), BF16),
        scratch_shapes=[pltpu.VMEM((k, tn), BF16)],
        compiler_params=_params(("arbitrary", "arbitrary")),
        name=name,
    )(a, w)


def _mm_t_kernel(a_ref, w_ref, o_ref, wb_ref):
    @pl.when(pl.program_id(1) == 0)
    def _():
        wb_ref[...] = w_ref[0].astype(BF16)

    o_ref[...] = _dot_nt(a_ref[...], wb_ref[...]).astype(o_ref.dtype)


def _matmul_t(a, wt, layer, row0, n, name):
    m, k = a.shape
    tm = _pick_tile(m, MATMUL_ROW_TILES)
    tn = _pick_tile(n, MATMUL_COL_TILES)
    assert row0 % BF16_ROWS_PER_TILE == 0 and wt.shape[1] >= row0 + n
    return pl.pallas_call(
        _mm_t_kernel,
        grid=(n // tn, m // tm),
        in_specs=[pl.BlockSpec((tm, k), lambda j, i: (i, 0)),
                  pl.BlockSpec((pl.Element(1), pl.Element(tn), pl.Element(k)),
                               lambda j, i: (layer, pl.multiple_of(row0 + j * tn, BF16_ROWS_PER_TILE), 0))],
        out_specs=pl.BlockSpec((tm, tn), lambda j, i: (i, j)),
        out_shape=jax.ShapeDtypeStruct((m, n), BF16),
        scratch_shapes=[pltpu.VMEM((tn, k), BF16)],
        compiler_params=_params(("arbitrary", "arbitrary")),
        name=name,
    )(a, wt)


def _level_operand(q, k, la, bc, m, rev):
    c, dk = q.shape
    if m >= SUBLANES:
        pieces = []
        for jb in range(c // m):
            rows = slice(jb * m, (jb + 1) * m)
            r = (jb // 2) * 2 * m + (m if rev else m - 1)
            cvec = bc[r:r + 1, :]
            if (jb % 2 == 1) != rev:
                pieces.append(q[rows, :] * jnp.exp2(bc[rows, :] - cvec))
            else:
                pieces.append(k[rows, :] * jnp.exp2(cvec - bc[rows, :]))
        return jnp.concatenate(pieces, axis=0)
    shape3 = (c // SUBLANES, SUBLANES, dk)
    sub = lax.broadcasted_iota(jnp.int32, (1, SUBLANES, dk), 1)
    upper = (sub & m) != 0
    q_role = jnp.logical_not(upper) if rev else upper
    q3, k3, bc3 = q.reshape(shape3), k.reshape(shape3), bc.reshape(shape3)
    if m == 1:
        e2 = jnp.where(q_role, la.reshape(shape3), 0.0)
    else:
        if m == 4:
            r = 4 if rev else 3
            c3 = bc3[:, r:r + 1, :]
        else:
            r0, r1 = (2, 6) if rev else (1, 5)
            c3 = jnp.where(sub < 4, bc3[:, r0:r0 + 1, :], bc3[:, r1:r1 + 1, :])
        e2 = (bc3 - c3) * jnp.where(q_role, 1.0, -1.0)
    return (jnp.where(q_role, q3, k3) * jnp.exp2(e2)).reshape(c, dk)


def _level_matrix(c, rev):
    t = lax.broadcasted_iota(jnp.int32, (c, c), 0)
    s = lax.broadcasted_iota(jnp.int32, (c, c), 1)
    x = t ^ s
    lvl = jnp.full((c, c), -1, jnp.int32)
    b = 1
    while b < c:
        lvl = lvl + (x >= b).astype(jnp.int32)
        b *= 2
    return jnp.where((t < s) if rev else (t > s), lvl, -1)


def _scan_both_directions(prep_chunk, q_s, kf_s, kb_s, laf_s, lab_s, v_ref, o_s, st_s, bc_s, vt_s, sc_s, n_ctx_chunks):
    c = SCAN_CHUNK
    n_chunks = q_s.shape[0] // c
    nh = st_s.shape[1]
    dv = v_ref.shape[1] // nh
    ti = lax.broadcasted_iota(jnp.int32, (c, c), 0)
    si = lax.broadcasted_iota(jnp.int32, (c, c), 1)
    lane = lax.broadcasted_iota(jnp.int32, (c, LANES), 1) // (LANES // nh)
    head_lanes = [(lane == h, (lane == h).astype(BF16)) for h in range(nh)]
    k_refs, la_refs = (kf_s, kb_s), (laf_s, lab_s)
    tris = ((si <= ti).astype(BF16), (si >= ti).astype(BF16))
    lvls = (_level_matrix(c, False), _level_matrix(c, True))

    def cumulate(i, carry):
        sl = pl.ds(pl.multiple_of(i * c, c), c)
        las = prep_chunk(sl)
        for h in range(nh):
            vt_s[i, h * dv:(h + 1) * dv, :] = v_ref[sl, h * dv:(h + 1) * dv].T
        for d in range(2):
            la_hi = las[d].astype(BF16)
            la_lo = (las[d] - la_hi.astype(F32)).astype(BF16)
            bb = _dot(tris[d], jnp.concatenate([la_hi, la_lo], axis=1))
            bc_s[d, sl, :] = (bb[:, :LANES] + bb[:, LANES:]) * LOG2E
            la_refs[d][sl, :] = las[d] * LOG2E
        return carry

    lax.fori_loop(0, n_chunks, cumulate, 0, unroll=3)

    st_s[...] = jnp.zeros_like(st_s)
    nu = SCAN_UNROLL
    n_groups = n_chunks // nu
    assert n_chunks % nu == 0 and n_groups >= 2
    streams = [(d, u) for u in range(nu) for d in range(2)]
    slot = {(su, h): (su[0] * nu + su[1]) * nh + h for su in streams for h in range(nh)}

    def group_rows(g):
        seq = [g * nu + u for u in range(nu)]
        cis = [seq, [jnp.where(j < n_ctx_chunks, n_ctx_chunks - 1 - j, n_chunks - 1 + n_ctx_chunks - j) for j in seq]]
        return cis, {(d, u): pl.ds(pl.multiple_of(cis[d][u] * c, c), c) for d, u in streams}

    def state_phase(g):
        cis, sl = group_rows(g)
        q = {su: q_s[sl[su], :] for su in streams}
        k = {(d, u): k_refs[d][sl[d, u], :] for d, u in streams}
        la = {(d, u): la_refs[d][sl[d, u], :] for d, u in streams}
        bc = {(d, u): bc_s[d, sl[d, u], :] for d, u in streams}
        st = [[st_s[d, h] for h in range(nh)] for d in range(2)]
        inter = {}
        for d, u in streams:
            su = (d, u)
            qg = (q[su] * jnp.exp2(bc[su])).astype(BF16)
            tot = bc[su][0:1, :] if d else bc[su][c - 1:c, :]
            kg = (k[su] * jnp.exp2(tot - bc[su])).astype(BF16)
            dec = jnp.exp2(tot)
            for h in range(nh):
                qg_h = qg if nh == 1 else qg * head_lanes[h][1]
                inter[su, h] = _dot_nt(qg_h, st[d][h].astype(BF16))
                st[d][h] = st[d][h] * dec + _dot(vt_s[cis[d][u], h * dv:(h + 1) * dv, :], kg)
        for d in range(2):
            for h in range(nh):
                st_s[d, h] = st[d][h]
        return sl, q, k, la, bc, inter

    def score_matmuls(g):
        _, sl = group_rows(g)
        return sl, {(su, h): _dot(sc_s[slot[su, h]], v_ref[sl[su], h * dv:(h + 1) * dv])
                    for su in streams for h in range(nh)}

    def level_phase(sl, q, k, la, bc, inter):
        scores = {key: jnp.zeros((c, c), F32) for key in slot}
        m, level = c // 2, (c // 2).bit_length() - 1
        while m >= 1:
            for d, u in streams:
                su = (d, u)
                x = _level_operand(q[su], k[su], la[su], bc[su], m, d == 1).astype(BF16)
                x_heads = x if nh == 1 else jnp.concatenate([x * head_lanes[h][1] for h in range(nh)], axis=0)
                s_all = _dot_nt(x, x_heads)
                for h in range(nh):
                    scores[su, h] = jnp.where(lvls[d] == level, s_all[:, h * c:(h + 1) * c], scores[su, h])
            m, level = m // 2, level - 1
        for d, u in streams:
            su = (d, u)
            qk = q[su] * k[su]
            for h in range(nh):
                sc_s[slot[su, h]] = scores[su, h].astype(BF16)
                v = v_ref[sl[su], h * dv:(h + 1) * dv].astype(F32)
                qk_h = qk if nh == 1 else jnp.where(head_lanes[h][0], qk, 0.0)
                o_s[d, sl[su], h * dv:(h + 1) * dv] = inter[su, h] + jnp.sum(qk_h, axis=1, keepdims=True) * v

    def add_scores_part(sl, outs):
        for (su, h), o in outs.items():
            cols = slice(h * dv, (h + 1) * dv)
            o_s[su[0], sl[su], cols] = o_s[su[0], sl[su], cols] + o

    level_phase(*state_phase(0))

    def body(g, carry):
        cur = state_phase(g)
        sl_prev, outs = score_matmuls(g - 1)
        level_phase(*cur)
        add_scores_part(sl_prev, outs)
        return carry

    lax.fori_loop(1, n_groups, body, 0)
    add_scores_part(*score_matmuls(n_groups - 1))


def _scan_scratch(tt, nh):
    n_chunks = tt // SCAN_CHUNK
    return [pltpu.VMEM((2, tt, nh * HEAD_DV), F32), pltpu.VMEM((2, nh, HEAD_DV, LANES), F32),
            pltpu.VMEM((2, tt, LANES), F32), pltpu.VMEM((n_chunks, nh * HEAD_DV, SCAN_CHUNK), BF16),
            pltpu.VMEM((2 * SCAN_UNROLL * nh, SCAN_CHUNK, SCAN_CHUNK), BF16)]


def _rms_gate(o, gain, gate):
    o = o * lax.rsqrt(jnp.mean(jnp.square(o), axis=-1, keepdims=True) + RMS_EPS) * gain
    return o * (gate * jax.nn.sigmoid(gate))


def _hgrn_kernel(n_ctx_chunks, q_ref, i_ref, g_ref, ff_ref, fb_ref, lb_ref, gain_ref, y_ref,
                 q_s, kf_s, kb_s, laf_s, lab_s, *scan_s):
    c = SCAN_CHUNK
    n_chunks = q_s.shape[0] // c
    o_s = scan_s[0]
    lb_f = lb_ref[0:1, :]
    lb_b = lb_ref[1:2, :]

    def prep_chunk(sl):
        q = q_ref[sl, :].astype(F32)
        q_s[sl, :] = q * jax.nn.sigmoid(q) * (A_DK ** -0.5)
        las = []
        for f_ref, lb, k_s in ((ff_ref, lb_f, kf_s), (fb_ref, lb_b, kb_s)):
            s = jax.nn.sigmoid(f_ref[sl, :].astype(F32))
            k_s[sl, :] = (1.0 - lb) * (1.0 - s)
            las.append(jnp.log(lb + (1.0 - lb) * s))
        return las

    _scan_both_directions(prep_chunk, q_s, kf_s, kb_s, laf_s, lab_s, i_ref, *scan_s, n_ctx_chunks)

    def fin(i, carry):
        sl = pl.ds(pl.multiple_of(i * c, c), c)
        o = o_s[0, sl, :] + o_s[1, sl, :]
        y_ref[sl, :] = _rms_gate(o, gain_ref[...], g_ref[sl, :].astype(F32)).astype(y_ref.dtype)
        return carry

    lax.fori_loop(0, n_chunks, fin, 0, unroll=2)


def _hgrn_mixer(p, lb, gain, ctx_len, col0):
    b, tt, _ = p.shape
    blk = lambda seg: pl.BlockSpec((None, tt, LANES), lambda i, h, s=seg: (i, 0, col0 + s * N_HEADS + h))
    seq = lambda: pltpu.VMEM((tt, LANES), F32)
    return pl.pallas_call(
        functools.partial(_hgrn_kernel, ctx_len // SCAN_CHUNK),
        grid=(b, N_HEADS),
        in_specs=[blk(0), blk(1), blk(2), blk(3), blk(4),
                  pl.BlockSpec((2, LANES), lambda i, h: (0, h)),
                  pl.BlockSpec((1, LANES), lambda i, h: (0, 0))],
        out_specs=pl.BlockSpec((None, tt, LANES), lambda i, h: (i, 0, h)),
        out_shape=jax.ShapeDtypeStruct((b, tt, MIX_W), BF16),
        scratch_shapes=[seq(), seq(), seq(), seq(), seq()] + _scan_scratch(tt, 1),
        compiler_params=_params(("arbitrary", "arbitrary")),
        name="hgrn2_mixer",
    )(p, p, p, p, p, lb, gain.reshape(1, HEAD_DV))


def _rope_rotate(x):
    n = x.shape[-1]
    lane = lax.broadcasted_iota(jnp.int32, x.shape, x.ndim - 1)
    first = (lane % 32) < 16
    return jnp.where(first, -pltpu.roll(x, n - 16, x.ndim - 1), pltpu.roll(x, 16, x.ndim - 1))


def _gla_kernel(n_ctx_chunks, q_ref, k_ref, v_ref, g_ref, gk_ref, wg_ref, bg_ref, cos_ref, sin_ref,
                gain_ref, y_ref, q_s, k_s, laf_s, lab_s, *scan_s):
    c = SCAN_CHUNK
    n_chunks = q_s.shape[0] // c
    nh = LANES // B_DK
    o_s = scan_s[0]

    def prep_chunk(sl):
        cos = cos_ref[sl, :]
        sin = sin_ref[sl, :]
        q = q_ref[sl, :].astype(F32)
        k = k_ref[sl, :].astype(F32)
        q_s[sl, :] = (q * cos + _rope_rotate(q) * sin) * (B_DK ** -0.5)
        k_s[sl, :] = k * cos + _rope_rotate(k) * sin
        gk = gk_ref[sl, :]
        return [jax.nn.log_sigmoid(_dot(gk, wg_ref[d]) + bg_ref[d]) / GATE_LOGIT_NORMALIZER for d in range(2)]

    _scan_both_directions(prep_chunk, q_s, k_s, k_s, laf_s, lab_s, v_ref, *scan_s, n_ctx_chunks)

    def fin(i, carry):
        sl = pl.ds(pl.multiple_of(i * c, c), c)
        for h in range(nh):
            cols = slice(h * HEAD_DV, (h + 1) * HEAD_DV)
            o = o_s[0, sl, cols] + o_s[1, sl, cols]
            y_ref[sl, cols] = _rms_gate(o, gain_ref[...], g_ref[sl, cols].astype(F32)).astype(y_ref.dtype)
        return carry

    lax.fori_loop(0, n_chunks, fin, 0, unroll=2)


def _gla_mixer(p, p_gk, wg, bg, cos, sin, gain, ctx_len, col_q, col_k, col_v, col_g):
    b, tt, _ = p.shape
    nh = LANES // B_DK
    wide = nh * HEAD_DV
    pair = lambda c0: pl.BlockSpec((None, tt, LANES), lambda i, h: (i, 0, c0 + h))
    pair_v = lambda c0: pl.BlockSpec((None, tt, wide), lambda i, h: (i, 0, c0 * LANES // wide + h))
    seq = lambda: pltpu.VMEM((tt, LANES), F32)
    assert (col_v * LANES) % wide == 0 and (col_g * LANES) % wide == 0
    return pl.pallas_call(
        functools.partial(_gla_kernel, ctx_len // SCAN_CHUNK),
        grid=(b, N_HEADS // nh),
        in_specs=[pair(col_q), pair(col_k), pair_v(col_v), pair_v(col_g),
                  pl.BlockSpec((None, tt, LANES), lambda i, h: (i, 0, 0)),
                  pl.BlockSpec((2, None, LANES, LANES), lambda i, h: (0, h, 0, 0)),
                  pl.BlockSpec((2, None, 1, LANES), lambda i, h: (0, h, 0, 0)),
                  pl.BlockSpec((tt, LANES), lambda i, h: (0, 0)),
                  pl.BlockSpec((tt, LANES), lambda i, h: (0, 0)),
                  pl.BlockSpec((1, LANES), lambda i, h: (0, 0))],
        out_specs=pl.BlockSpec((None, tt, wide), lambda i, h: (i, 0, h)),
        out_shape=jax.ShapeDtypeStruct((b, tt, MIX_W), BF16),
        scratch_shapes=[seq(), seq(), seq(), seq()] + _scan_scratch(tt, nh),
        compiler_params=_params(("arbitrary", "arbitrary")),
        name="gla_mixer",
    )(p, p, p, p, p_gk, wg, bg, cos, sin, gain.reshape(1, HEAD_DV))


def _natten_kernel(ctx_len, rows, kr, q_ref, k_ref, v_ref, rpb_ref, y_ref, bias_ref):
    scale = HEAD_DV ** -0.5
    kc = k_ref[0:ctx_len, :]
    vc = v_ref[0:ctx_len, :]

    qi = lax.broadcasted_iota(jnp.int32, (GRID_W, LANES), 0)
    kj = lax.broadcasted_iota(jnp.int32, (GRID_W, LANES), 1) % GRID_W
    col_start = jnp.clip(qi - KC // 2, 0, GRID_W - KC)
    col_ok = (kj >= col_start) & (kj < col_start + KC)
    for v in range(kr):
        for blk in range(kr * GRID_W // LANES):
            d = v + blk * (LANES // GRID_W)
            src = jnp.broadcast_to(rpb_ref[d:d + 1, :], (GRID_W, LANES))
            rot = pltpu.roll(src, LANES - (KC - 1), 1, stride=1, stride_axis=0)
            bias_ref[v, :, blk * LANES:(blk + 1) * LANES] = jnp.where(col_ok, rot, MASK_VALUE)

    s = _dot_nt(q_ref[0:ctx_len, :], kc) * scale
    p = jnp.exp(s - jnp.max(s, axis=-1, keepdims=True))
    o = _dot(p.astype(BF16), vc) / jnp.sum(p, axis=-1, keepdims=True)
    y_ref[0:ctx_len, :] = o.astype(y_ref.dtype)

    nr = NATTEN_ROWS_PER_TRIP
    assert rows % nr == 0

    def body(i, carry):
        q_sl, b_sl, s_b, s_c = [], [], [], []
        for u in range(nr):
            r = i * nr + u
            start = jnp.clip(r - kr // 2, 0, rows - kr)
            q_sl.append(pl.ds(pl.multiple_of(ctx_len + r * GRID_W, GRID_W), GRID_W))
            b_sl.append(pl.ds(pl.multiple_of(ctx_len + start * GRID_W, GRID_W), kr * GRID_W))
            q = q_ref[q_sl[u], :]
            s_b.append(_dot_nt(q, k_ref[b_sl[u], :]) * scale + bias_ref[start - r + (KR_MAX - 1)])
            s_c.append(_dot_nt(q, kc) * scale)
        for u in range(nr):
            mx = jnp.maximum(jnp.max(s_b[u], axis=-1, keepdims=True), jnp.max(s_c[u], axis=-1, keepdims=True))
            p_b = jnp.exp(s_b[u] - mx)
            p_c = jnp.exp(s_c[u] - mx)
            den = jnp.sum(p_b, axis=-1, keepdims=True) + jnp.sum(p_c, axis=-1, keepdims=True)
            o = (_dot(p_b.astype(BF16), v_ref[b_sl[u], :]) + _dot(p_c.astype(BF16), vc)) / den
            y_ref[q_sl[u], :] = o.astype(y_ref.dtype)
        return carry

    lax.fori_loop(0, rows // nr, body, 0)


def _natten_mixer(p, rpb, ctx_len, col_q, col_k, col_v):
    b, tt, _ = p.shape
    rows = (tt - ctx_len) // GRID_W
    kr = KR_MAX
    assert rows >= KR_MAX and rows % 2 == 0 and 2 * GRID_W == LANES and 2 * KC - 1 <= GRID_W
    padded = jnp.pad(rpb.astype(F32), ((0, 0), (0, 0), (0, GRID_W - (2 * KC - 1))))
    pairs = jnp.concatenate([padded[:, :-1], padded[:, 1:]], axis=-1)
    head = lambda c0: pl.BlockSpec((None, tt, LANES), lambda i, h: (i, 0, c0 + h))
    return pl.pallas_call(
        functools.partial(_natten_kernel, ctx_len, rows, kr),
        grid=(b, N_HEADS),
        in_specs=[head(col_q), head(col_k), head(col_v),
                  pl.BlockSpec((None, 2 * KR_MAX - 2, LANES), lambda i, h: (h, 0, 0))],
        out_specs=pl.BlockSpec((None, tt, LANES), lambda i, h: (i, 0, h)),
        out_shape=jax.ShapeDtypeStruct((b, tt, MIX_W), BF16),
        scratch_shapes=[pltpu.VMEM((kr, GRID_W, kr * GRID_W), F32)],
        compiler_params=_params(("arbitrary", "arbitrary")),
        name="natten_mixer",
    )(p, p, p, pairs)


def _merge_kernel(ya_ref, yb_ref, yc_ref, w_ref, ga_ref, gb_ref, gc_ref, o_ref):
    acc = None
    for n, (y_ref, g_ref) in enumerate(((ya_ref, ga_ref), (yb_ref, gb_ref), (yc_ref, gc_ref))):
        z = jax.nn.sigmoid(g_ref[...].astype(F32)) * _dot(y_ref[...], w_ref[n])
        acc = z if acc is None else acc + z
    o_ref[...] = acc.astype(o_ref.dtype)


def _merge(ya, yb, yc, w_branch, layer, p, col_gate, d_model):
    m = ya.shape[0]
    tm = _pick_tile(m, (512, 256, 128))
    tn = _pick_tile(d_model, MATMUL_COL_TILES)
    nj = d_model // tn
    assert (col_gate * LANES) % tn == 0
    yspec = pl.BlockSpec((tm, MIX_W), lambda j, i: (i, 0))
    gspec = lambda n: pl.BlockSpec((tm, tn), lambda j, i, n=n: (i, col_gate * LANES // tn + n * nj + j))
    return pl.pallas_call(
        _merge_kernel,
        grid=(nj, m // tm),
        in_specs=[yspec, yspec, yspec,
                  pl.BlockSpec((None, N_BRANCH, MIX_W, tn), lambda j, i: (layer, 0, 0, j)),
                  gspec(0), gspec(1), gspec(2)],
        out_specs=pl.BlockSpec((tm, tn), lambda j, i: (i, j)),
        out_shape=jax.ShapeDtypeStruct((m, d_model), BF16),
        compiler_params=_params(("arbitrary", "arbitrary")),
        name="branch_merge",
    )(ya, yb, yc, w_branch, p, p, p)


def _proj_ln_kernel(alpha, ctx_len, tiles_per_batch, emit_h, a_ref, w_ref, x_ref, mod_ref, ln_ref, *rest):
    if emit_h:
        xo_ref, ho_ref, acc_ref = rest
    else:
        xo_ref, acc_ref = rest
    kk = pl.program_id(1)

    @pl.when(kk == 0)
    def _():
        acc_ref[...] = jnp.zeros_like(acc_ref)

    acc_ref[...] += _dot(a_ref[...], w_ref[...])

    @pl.when(kk == pl.num_programs(1) - 1)
    def _():
        tm, d = acc_ref.shape
        rb = LN_ROW_BLOCK
        row0 = (pl.program_id(0) % tiles_per_batch) * tm

        def block(i, carry):
            sl = pl.ds(pl.multiple_of(i * rb, rb), rb)
            mod = mod_ref[jnp.where(row0 + i * rb < ctx_len, 1, 0)]
            sel = lambda j: mod[j:j + 1, :]
            y = alpha * x_ref[sl, :] + sel(0) * acc_ref[sl, :]
            mu = jnp.mean(y, axis=-1, keepdims=True)
            yc = y - mu
            var = jnp.mean(jnp.square(yc), axis=-1, keepdims=True)
            xn = yc * lax.rsqrt(var + LN_EPS) * ln_ref[0:1, :] + ln_ref[1:2, :]
            xo_ref[sl, :] = xn
            if emit_h:
                ho_ref[sl, :] = (xn * (1.0 + sel(1)) + sel(2)).astype(BF16)
            return carry

        lax.fori_loop(0, tm // rb, block, 0)


def _proj_ln(a, w, layer, x, mod_sel, ln_gb, alpha, ctx_len, tt, emit_h, name):
    m, k = a.shape
    d = w.shape[2]
    assert w.dtype == BF16
    if k * d * 2 <= RESIDENT_WEIGHT_BYTES:
        tm, tk = _pick_tile(tt, (384, 256, 128)), k
    else:
        tm, tk = _pick_tile(tt, (768, 384, 256, 128)), _pick_tile(k, MATMUL_COL_TILES)
    assert tm % LN_ROW_BLOCK == 0 and ctx_len <= tm and ctx_len % LN_ROW_BLOCK == 0
    tpb = tt // tm
    out_shape = [jax.ShapeDtypeStruct((m, d), F32)]
    out_specs = [pl.BlockSpec((tm, d), lambda i, j: (i, 0))]
    if emit_h:
        out_shape.append(jax.ShapeDtypeStruct((m, d), BF16))
        out_specs.append(pl.BlockSpec((tm, d), lambda i, j: (i, 0)))
    res = pl.pallas_call(
        functools.partial(_proj_ln_kernel, alpha, ctx_len, tpb, emit_h),
        grid=(m // tm, k // tk),
        in_specs=[pl.BlockSpec((tm, tk), lambda i, j: (i, j)),
                  pl.BlockSpec((None, tk, d), lambda i, j: (layer, j, 0)),
                  pl.BlockSpec((tm, d), lambda i, j: (i, 0)),
                  pl.BlockSpec((None, 2, 3, d), lambda i, j: (i // tpb, 0, 0, 0)),
                  pl.BlockSpec((2, d), lambda i, j: (0, 0))],
        out_specs=out_specs,
        out_shape=out_shape,
        scratch_shapes=[pltpu.VMEM((tm, d), F32)],
        compiler_params=_params(("arbitrary", "arbitrary")),
        name=name,
    )(a, w, x, mod_sel, ln_gb)
    return (res[0], res[1]) if emit_h else (res[0], None)


def _rope_tables(seq, ctx_len):
    half = B_DK // 2
    freqs = ROPE_THETA ** (-jnp.arange(0, half, 2, dtype=F32) / half)
    pos = jnp.arange(seq)
    ang_r = (pos // GRID_W).astype(F32)[:, None] * freqs
    ang_c = (pos % GRID_W).astype(F32)[:, None] * freqs
    ang = jnp.concatenate([ang_r, ang_r, ang_c, ang_c], axis=-1)
    ang = jnp.concatenate([jnp.zeros((ctx_len, B_DK), F32), ang], axis=0)
    ang = jnp.concatenate([ang, ang], axis=-1)
    return jnp.cos(ang), jnp.sin(ang)


def kernel(x, c, ctx, c_ctx, w_ada, b_ada, w_in, hgrn_lb_logits, hgrn_norm_g, gla_w_gk2, gla_b_gk2, gla_norm_g,
           natten_rpb, w_branch, w_out, ln1_g, ln1_b, ln2_g, ln2_b, w_mlp1, w_mlp2):
    bsz, seq, d = x.shape
    ctx_len = ctx.shape[1]
    depth = w_ada.shape[0]
    tt = ctx_len + seq
    m_tok = bsz * tt
    alpha = (2 * depth) ** 0.25
    assert ctx_len % SCAN_CHUNK == 0 and seq % SCAN_CHUNK == 0 and seq % GRID_W == 0 and bsz + 1 <= SUBLANES

    seg = N_HEADS * A_DK
    gk_lo = 5 * seg + 2 * N_HEADS * B_DK + 2 * MIX_W
    gk_hi = gk_lo + 2 * GK_RANK
    n_cg = 3 * MIX_W + N_BRANCH * d
    assert gk_lo % 1024 == 0 and w_in.shape[2] == gk_hi + n_cg
    col_a = 0
    col_bq = 5 * seg // LANES
    col_bk = col_bq + N_HEADS * B_DK // LANES
    col_bv = col_bk + N_HEADS * B_DK // LANES
    col_bg = col_bv + MIX_W // LANES
    col_cq = 0
    col_ck = col_cq + MIX_W // LANES
    col_cv = col_ck + MIX_W // LANES
    col_gate = col_cv + MIX_W // LANES

    c_rows = jnp.zeros((SUBLANES, d), F32).at[:bsz].set(c).at[bsz].set(c_ctx)
    mod = _ada_all_layers(c_rows, w_ada, b_ada).reshape(depth, SUBLANES, 6, d)

    def mod_sel(l, idx):
        lat = mod[l, :bsz][:, idx, :]
        cx = jnp.broadcast_to(mod[l, bsz][idx, :], lat.shape)
        return jnp.stack([lat, cx], axis=1)

    lb_p = jax.nn.softmax(hgrn_lb_logits.astype(F32), axis=0)
    lb_cum = jnp.cumsum(lb_p, axis=0)
    lower_bounds = jnp.concatenate([jnp.zeros_like(lb_cum[:1]), lb_cum[:-1]], axis=0)

    cos, sin = _rope_tables(seq, ctx_len)
    w_out_bf = w_out.astype(BF16)
    w_mlp2_bf = w_mlp2.astype(BF16)

    xs = jnp.concatenate([ctx, x], axis=1)
    h = _modulate(xs, mod_sel(0, np.array([1, 0])), ctx_len).reshape(m_tok, d)
    xs = xs.reshape(m_tok, d)

    w_in_t = jnp.swapaxes(w_in, 1, 2)
    w_branch_bf = w_branch.astype(BF16)

    for l in range(depth):
        p = _matmul_t(h, w_in_t, l, 0, gk_lo, "in_proj_ab").reshape(bsz, tt, gk_lo)
        p_gk = _matmul_t(h, w_in_t, l, gk_lo, LANES, "in_proj_gk").reshape(bsz, tt, LANES)
        p_cg = _matmul_t(h, w_in_t, l, gk_hi, n_cg, "in_proj_cg").reshape(bsz, tt, n_cg)

        n_pairs = N_HEADS * B_DK // LANES
        wg = jnp.zeros((2, n_pairs, LANES, LANES), F32)
        w2 = gla_w_gk2[l].reshape(2, GK_RANK, n_pairs, LANES).transpose(0, 2, 1, 3)
        wg = wg.at[0, :, :GK_RANK].set(w2[0]).at[1, :, GK_RANK:2 * GK_RANK].set(w2[1]).astype(BF16)
        bg = gla_b_gk2[l].reshape(2, n_pairs, 1, LANES)

        y_a = _hgrn_mixer(p, lower_bounds[l], hgrn_norm_g[l], ctx_len, col_a)
        y_b = _gla_mixer(p, p_gk, wg, bg, cos, sin, gla_norm_g[l], ctx_len, col_bq, col_bk, col_bv, col_bg)
        y_c = _natten_mixer(p_cg, natten_rpb[l], ctx_len, col_cq, col_ck, col_cv)

        merged = _merge(y_a.reshape(m_tok, MIX_W), y_b.reshape(m_tok, MIX_W), y_c.reshape(m_tok, MIX_W),
                        w_branch_bf, l, p_cg.reshape(m_tok, n_cg), col_gate, d)
        xs, h2 = _proj_ln(merged, w_out_bf, l, xs, mod_sel(l, np.array([2, 4, 3])),
                          jnp.stack([ln1_g[l], ln1_b[l]]), alpha, ctx_len, tt, True, "out_proj_ln")
        u = _matmul_sq_relu(h2, w_mlp1, l, "mlp_up")
        last = l == depth - 1
        if last:
            nxt = jnp.zeros((bsz, 2, 3, d), F32).at[:, :, 0].set(mod_sel(l, np.array([5]))[:, :, 0])
        else:
            nxt = jnp.concatenate([mod_sel(l, np.array([5])), mod_sel(l + 1, np.array([1, 0]))], axis=2)
        xs, h = _proj_ln(u, w_mlp2_bf, l, xs, nxt, jnp.stack([ln2_g[l], ln2_b[l]]),
                         alpha, ctx_len, tt, not last, "mlp_down_ln")

    return xs.reshape(bsz, tt, d)[:, ctx_len:, :]
```

```python
import functools

import jax
import jax.numpy as jnp
import numpy as np
from jax import lax
from jax.experimental import pallas as pl
from jax.experimental.pallas import tpu as pltpu

GRID_W = 64
N_BRANCH = 3
MIX_W = 1024
N_HEADS = 8
A_DK = 128
B_DK = 64
HEAD_DV = MIX_W // N_HEADS
GK_RANK = 16
GATE_LOGIT_NORMALIZER = 16.0
KR_MAX = 8
KC = 16
ROPE_THETA = 10000.0
RMS_EPS = 1e-6
LN_EPS = 1e-5
MASK_VALUE = -1e30
LOG2E = 1.4426950408889634

LANES = 128
SUBLANES = 8
BF16_ROWS_PER_TILE = 16
SCAN_CHUNK = 128
SCAN_UNROLL = 2
NATTEN_ROWS_PER_TRIP = 8
LN_ROW_BLOCK = 128
VMEM_LIMIT_BYTES = 56 * 1024 * 1024
RESIDENT_WEIGHT_BYTES = 16 * 1024 * 1024
MATMUL_ROW_TILES = (1536, 1152, 1024, 512, 256, 128)
MATMUL_COL_TILES = (1024, 512, 256, 128)

F32 = jnp.float32
BF16 = jnp.bfloat16


def _pick_tile(n, prefs):
    for t in prefs:
        if n % t == 0:
            return t
    return n


def _params(sem):
    return pltpu.CompilerParams(dimension_semantics=sem, vmem_limit_bytes=VMEM_LIMIT_BYTES)


def _dot(a, b):
    return jnp.dot(a, b, preferred_element_type=F32)


def _dot_nt(a, b):
    return lax.dot_general(a, b, (((1,), (1,)), ((), ())), preferred_element_type=F32)


def _ada_kernel(c_ref, w_ref, b_ref, o_ref):
    c = c_ref[...]
    s = (c * jax.nn.sigmoid(c)).astype(BF16)
    o_ref[...] = _dot(s, w_ref[...].astype(BF16)) + b_ref[...]


def _ada_all_layers(c_rows, w_ada, b_ada):
    depth, d, n = w_ada.shape
    tn = _pick_tile(n, MATMUL_COL_TILES)
    return pl.pallas_call(
        _ada_kernel,
        grid=(depth, n // tn),
        in_specs=[pl.BlockSpec((SUBLANES, d), lambda l, j: (0, 0)),
                  pl.BlockSpec((None, d, tn), lambda l, j: (l, 0, j)),
                  pl.BlockSpec((None, 1, tn), lambda l, j: (l, 0, j))],
        out_specs=pl.BlockSpec((None, SUBLANES, tn), lambda l, j: (l, 0, j)),
        out_shape=jax.ShapeDtypeStruct((depth, SUBLANES, n), F32),
        compiler_params=_params(("arbitrary", "arbitrary")),
        name="ada_mod",
    )(c_rows, w_ada, b_ada.reshape(depth, 1, n))


def _modulate_kernel(ctx_len, x_ref, m_ref, h_ref):
    x = x_ref[...]
    rows = lax.broadcasted_iota(jnp.int32, x.shape, 0) + pl.program_id(1) * x.shape[0]
    is_ctx = rows < ctx_len
    sc = jnp.where(is_ctx, m_ref[1, 0:1, :], m_ref[0, 0:1, :])
    sh = jnp.where(is_ctx, m_ref[1, 1:2, :], m_ref[0, 1:2, :])
    h_ref[...] = (x * (1.0 + sc) + sh).astype(BF16)


def _modulate(x, mod_sel, ctx_len):
    b, tt, d = x.shape
    tm = _pick_tile(tt, (768, 384, 256, 128))
    return pl.pallas_call(
        functools.partial(_modulate_kernel, ctx_len),
        grid=(b, tt // tm),
        in_specs=[pl.BlockSpec((None, tm, d), lambda i, j: (i, j, 0)),
                  pl.BlockSpec((None, 2, 2, d), lambda i, j: (i, 0, 0, 0))],
        out_specs=pl.BlockSpec((None, tm, d), lambda i, j: (i, j, 0)),
        out_shape=jax.ShapeDtypeStruct((b, tt, d), BF16),
        compiler_params=_params(("arbitrary", "arbitrary")),
        name="modulate",
    )(x, mod_sel)


def _mm_sq_relu_kernel(a_ref, w_ref, o_ref, wb_ref):
    @pl.when(pl.program_id(1) == 0)
    def _():
        wb_ref[...] = w_ref[...].astype(BF16)

    o_ref[...] = jnp.square(jnp.maximum(_dot(a_ref[...], wb_ref[...]), 0.0)).astype(o_ref.dtype)


def _matmul_sq_relu(a, w, layer, name):
    m, k = a.shape
    n = w.shape[2]
    tm = _pick_tile(m, MATMUL_ROW_TILES)
    tn = _pick_tile(n, MATMUL_COL_TILES)
    return pl.pallas_call(
        _mm_sq_relu_kernel,
        grid=(n // tn, m // tm),
        in_specs=[pl.BlockSpec((tm, k), lambda j, i: (i, 0)),
                  pl.BlockSpec((None, k, tn), lambda j, i: (layer, 0, j))],
        out_specs=pl.BlockSpec((tm, tn), lambda j, i: (i, j)),
        out_shape=jax.ShapeDtypeStruct((m, n), BF16),
        scratch_shapes=[pltpu.VMEM((k, tn), BF16)],
        compiler_params=_params(("arbitrary", "arbitrary")),
        name=name,
    )(a, w)


def _mm_t_kernel(a_ref, w_ref, o_ref, wb_ref):
    @pl.when(pl.program_id(1) == 0)
    def _():
        wb_ref[...] = w_ref[0].astype(BF16)

    o_ref[...] = _dot_nt(a_ref[...], wb_ref[...]).astype(o_ref.dtype)


def _matmul_t(a, wt, layer, row0, n, name):
    m, k = a.shape
    tm = _pick_tile(m, MATMUL_ROW_TILES)
    tn = _pick_tile(n, MATMUL_COL_TILES)
    assert row0 % BF16_ROWS_PER_TILE == 0 and wt.shape[1] >= row0 + n
    return pl.pallas_call(
        _mm_t_kernel,
        grid=(n // tn, m // tm),
        in_specs=[pl.BlockSpec((tm, k), lambda j, i: (i, 0)),
                  pl.BlockSpec((pl.Element(1), pl.Element(tn), pl.Element(k)),
                               lambda j, i: (layer, pl.multiple_of(row0 + j * tn, BF16_ROWS_PER_TILE), 0))],
        out_specs=pl.BlockSpec((tm, tn), lambda j, i: (i, j)),
        out_shape=jax.ShapeDtypeStruct((m, n), BF16),
        scratch_shapes=[pltpu.VMEM((tn, k), BF16)],
        compiler_params=_params(("arbitrary", "arbitrary")),
        name=name,
    )(a, wt)


def _level_operand(q, k, la, bc, m, rev):
    c, dk = q.shape
    if m >= SUBLANES:
        pieces = []
        for jb in range(c // m):
            rows = slice(jb * m, (jb + 1) * m)
            r = (jb // 2) * 2 * m + (m if rev else m - 1)
            cvec = bc[r:r + 1, :]
            if (jb % 2 == 1) != rev:
                pieces.append(q[rows, :] * jnp.exp2(bc[rows, :] - cvec))
            else:
                pieces.append(k[rows, :] * jnp.exp2(cvec - bc[rows, :]))
        return jnp.concatenate(pieces, axis=0)
    shape3 = (c // SUBLANES, SUBLANES, dk)
    sub = lax.broadcasted_iota(jnp.int32, (1, SUBLANES, dk), 1)
    upper = (sub & m) != 0
    q_role = jnp.logical_not(upper) if rev else upper
    q3, k3, bc3 = q.reshape(shape3), k.reshape(shape3), bc.reshape(shape3)
    if m == 1:
        e2 = jnp.where(q_role, la.reshape(shape3), 0.0)
    else:
        if m == 4:
            r = 4 if rev else 3
            c3 = bc3[:, r:r + 1, :]
        else:
            r0, r1 = (2, 6) if rev else (1, 5)
            c3 = jnp.where(sub < 4, bc3[:, r0:r0 + 1, :], bc3[:, r1:r1 + 1, :])
        e2 = (bc3 - c3) * jnp.where(q_role, 1.0, -1.0)
    return (jnp.where(q_role, q3, k3) * jnp.exp2(e2)).reshape(c, dk)


def _level_matrix(c, rev):
    t = lax.broadcasted_iota(jnp.int32, (c, c), 0)
    s = lax.broadcasted_iota(jnp.int32, (c, c), 1)
    x = t ^ s
    lvl = jnp.full((c, c), -1, jnp.int32)
    b = 1
    while b < c:
        lvl = lvl + (x >= b).astype(jnp.int32)
        b *= 2
    return jnp.where((t < s) if rev else (t > s), lvl, -1)


def _scan_both_directions(prep_chunk, q_s, kf_s, kb_s, laf_s, lab_s, v_ref, o_s, st_s, bc_s, vt_s, sc_s, xt_s,
                          n_ctx_chunks):
    c = SCAN_CHUNK
    n_chunks = q_s.shape[0] // c
    nh = st_s.shape[1]
    dv = v_ref.shape[1] // nh
    ti = lax.broadcasted_iota(jnp.int32, (c, c), 0)
    si = lax.broadcasted_iota(jnp.int32, (c, c), 1)
    lane = lax.broadcasted_iota(jnp.int32, (c, LANES), 1) // (LANES // nh)
    head_lanes = [(lane == h, (lane == h).astype(BF16)) for h in range(nh)]
    k_refs, la_refs = (kf_s, kb_s), (laf_s, lab_s)
    tris = ((si <= ti).astype(BF16), (si >= ti).astype(BF16))
    lvls = (_level_matrix(c, False), _level_matrix(c, True))

    def cumulate(i, carry):
        sl = pl.ds(pl.multiple_of(i * c, c), c)
        las = prep_chunk(sl)
        for h in range(nh):
            vt_s[i, h * dv:(h + 1) * dv, :] = v_ref[sl, h * dv:(h + 1) * dv].T
        for d in range(2):
            la_hi = las[d].astype(BF16)
            la_lo = (las[d] - la_hi.astype(F32)).astype(BF16)
            bb = _dot(tris[d], jnp.concatenate([la_hi, la_lo], axis=1))
            bc_s[d, sl, :] = (bb[:, :LANES] + bb[:, LANES:]) * LOG2E
            la_refs[d][sl, :] = las[d] * LOG2E
        return carry

    lax.fori_loop(0, n_chunks, cumulate, 0, unroll=3)

    st_s[...] = jnp.zeros_like(st_s)
    nu = SCAN_UNROLL
    n_groups = n_chunks // nu
    assert n_chunks % nu == 0 and n_groups >= 2
    streams = [(d, u) for u in range(nu) for d in range(2)]
    slot = {(su, h): (su[0] * nu + su[1]) * nh + h for su in streams for h in range(nh)}

    def group_rows(g):
        seq = [g * nu + u for u in range(nu)]
        cis = [seq, [jnp.where(j < n_ctx_chunks, n_ctx_chunks - 1 - j, n_chunks - 1 + n_ctx_chunks - j) for j in seq]]
        return cis, {(d, u): pl.ds(pl.multiple_of(cis[d][u] * c, c), c) for d, u in streams}

    def state_phase(g):
        cis, sl = group_rows(g)
        q = {su: q_s[sl[su], :] for su in streams}
        k = {(d, u): k_refs[d][sl[d, u], :] for d, u in streams}
        la = {(d, u): la_refs[d][sl[d, u], :] for d, u in streams}
        bc = {(d, u): bc_s[d, sl[d, u], :] for d, u in streams}
        st = [[st_s[d, h] for h in range(nh)] for d in range(2)]
        inter = {}
        for d, u in streams:
            su = (d, u)
            qg = (q[su] * jnp.exp2(bc[su])).astype(BF16)
            tot = bc[su][0:1, :] if d else bc[su][c - 1:c, :]
            kg = (k[su] * jnp.exp2(tot - bc[su])).astype(BF16)
            dec = jnp.exp2(tot)
            for h in range(nh):
                qg_h = qg if nh == 1 else qg * head_lanes[h][1]
                inter[su, h] = _dot_nt(qg_h, st[d][h].astype(BF16))
                st[d][h] = st[d][h] * dec + _dot(vt_s[cis[d][u], h * dv:(h + 1) * dv, :], kg)
        for d in range(2):
            for h in range(nh):
                st_s[d, h] = st[d][h]
        return sl, q, k, la, bc, inter

    def score_matmuls(g):
        _, sl = group_rows(g)
        return sl, {(su, h): _dot(sc_s[slot[su, h]], v_ref[sl[su], h * dv:(h + 1) * dv])
                    for su in streams for h in range(nh)}

    def level_phase(sl, q, k, la, bc, inter):
        scores = {key: jnp.zeros((c, c), F32) for key in slot}
        m, level = c // 2, (c // 2).bit_length() - 1
        while m >= 1:
            for d, u in streams:
                su = (d, u)
                x = _level_operand(q[su], k[su], la[su], bc[su], m, d == 1).astype(BF16)
                if nh == 1:
                    xt_s[slot[su, 0]] = x.T
                    s_all = _dot(x, xt_s[slot[su, 0]])
                else:
                    x_heads = jnp.concatenate([x * head_lanes[h][1] for h in range(nh)], axis=0)
                    s_all = _dot_nt(x, x_heads)
                for h in range(nh):
                    scores[su, h] = jnp.where(lvls[d] == level, s_all[:, h * c:(h + 1) * c], scores[su, h])
            m, level = m // 2, level - 1
        for d, u in streams:
            su = (d, u)
            qk = q[su] * k[su]
            for h in range(nh):
                sc_s[slot[su, h]] = scores[su, h].astype(BF16)
                v = v_ref[sl[su], h * dv:(h + 1) * dv].astype(F32)
                qk_h = qk if nh == 1 else jnp.where(head_lanes[h][0], qk, 0.0)
                o_s[d, sl[su], h * dv:(h + 1) * dv] = inter[su, h] + jnp.sum(qk_h, axis=1, keepdims=True) * v

    def add_scores_part(sl, outs):
        for (su, h), o in outs.items():
            cols = slice(h * dv, (h + 1) * dv)
            o_s[su[0], sl[su], cols] = o_s[su[0], sl[su], cols] + o

    level_phase(*state_phase(0))

    def body(g, carry):
        cur = state_phase(g)
        sl_prev, outs = score_matmuls(g - 1)
        level_phase(*cur)
        add_scores_part(sl_prev, outs)
        return carry

    lax.fori_loop(1, n_groups, body, 0)
    add_scores_part(*score_matmuls(n_groups - 1))


def _scan_scratch(tt, nh):
    n_chunks = tt // SCAN_CHUNK
    return [pltpu.VMEM((2, tt, nh * HEAD_DV), F32), pltpu.VMEM((2, nh, HEAD_DV, LANES), F32),
            pltpu.VMEM((2, tt, LANES), F32), pltpu.VMEM((n_chunks, nh * HEAD_DV, SCAN_CHUNK), BF16),
            pltpu.VMEM((2 * SCAN_UNROLL * nh, SCAN_CHUNK, SCAN_CHUNK), BF16),
            pltpu.VMEM((2 * SCAN_UNROLL * nh, LANES, SCAN_CHUNK), BF16)]


def _rms_gate(o, gain, gate):
    o = o * lax.rsqrt(jnp.mean(jnp.square(o), axis=-1, keepdims=True) + RMS_EPS) * gain
    return o * (gate * jax.nn.sigmoid(gate))


def _hgrn_kernel(n_ctx_chunks, q_ref, i_ref, g_ref, ff_ref, fb_ref, lb_ref, gain_ref, y_ref,
                 q_s, kf_s, kb_s, laf_s, lab_s, *scan_s):
    c = SCAN_CHUNK
    n_chunks = q_s.shape[0] // c
    o_s = scan_s[0]
    lb_f = lb_ref[0:1, :]
    lb_b = lb_ref[1:2, :]

    def prep_chunk(sl):
        q = q_ref[sl, :].astype(F32)
        q_s[sl, :] = q * jax.nn.sigmoid(q) * (A_DK ** -0.5)
        las = []
        for f_ref, lb, k_s in ((ff_ref, lb_f, kf_s), (fb_ref, lb_b, kb_s)):
            s = jax.nn.sigmoid(f_ref[sl, :].astype(F32))
            k_s[sl, :] = (1.0 - lb) * (1.0 - s)
            las.append(jnp.log(lb + (1.0 - lb) * s))
        return las

    _scan_both_directions(prep_chunk, q_s, kf_s, kb_s, laf_s, lab_s, i_ref, *scan_s, n_ctx_chunks)

    def fin(i, carry):
        sl = pl.ds(pl.multiple_of(i * c, c), c)
        o = o_s[0, sl, :] + o_s[1, sl, :]
        y_ref[sl, :] = _rms_gate(o, gain_ref[...], g_ref[sl, :].astype(F32)).astype(y_ref.dtype)
        return carry

    lax.fori_loop(0, n_chunks, fin, 0, unroll=2)


def _hgrn_mixer(p, lb, gain, ctx_len, col0):
    b, tt, _ = p.shape
    blk = lambda seg: pl.BlockSpec((None, tt, LANES), lambda i, h, s=seg: (i, 0, col0 + s * N_HEADS + h))
    seq = lambda: pltpu.VMEM((tt, LANES), F32)
    return pl.pallas_call(
        functools.partial(_hgrn_kernel, ctx_len // SCAN_CHUNK),
        grid=(b, N_HEADS),
        in_specs=[blk(0), blk(1), blk(2), blk(3), blk(4),
                  pl.BlockSpec((2, LANES), lambda i, h: (0, h)),
                  pl.BlockSpec((1, LANES), lambda i, h: (0, 0))],
        out_specs=pl.BlockSpec((None, tt, LANES), lambda i, h: (i, 0, h)),
        out_shape=jax.ShapeDtypeStruct((b, tt, MIX_W), BF16),
        scratch_shapes=[seq(), seq(), seq(), seq(), seq()] + _scan_scratch(tt, 1),
        compiler_params=_params(("arbitrary", "arbitrary")),
        name="hgrn2_mixer",
    )(p, p, p, p, p, lb, gain.reshape(1, HEAD_DV))


def _rope_rotate(x):
    n = x.shape[-1]
    lane = lax.broadcasted_iota(jnp.int32, x.shape, x.ndim - 1)
    first = (lane % 32) < 16
    return jnp.where(first, -pltpu.roll(x, n - 16, x.ndim - 1), pltpu.roll(x, 16, x.ndim - 1))


def _gla_kernel(n_ctx_chunks, q_ref, k_ref, v_ref, g_ref, gk_ref, wg_ref, bg_ref, cos_ref, sin_ref,
                gain_ref, y_ref, q_s, k_s, laf_s, lab_s, *scan_s):
    c = SCAN_CHUNK
    n_chunks = q_s.shape[0] // c
    nh = LANES // B_DK
    o_s = scan_s[0]

    def prep_chunk(sl):
        cos = cos_ref[sl, :]
        sin = sin_ref[sl, :]
        q = q_ref[sl, :].astype(F32)
        k = k_ref[sl, :].astype(F32)
        q_s[sl, :] = (q * cos + _rope_rotate(q) * sin) * (B_DK ** -0.5)
        k_s[sl, :] = k * cos + _rope_rotate(k) * sin
        gk = gk_ref[sl, :]
        return [jax.nn.log_sigmoid(_dot(gk, wg_ref[d]) + bg_ref[d]) / GATE_LOGIT_NORMALIZER for d in range(2)]

    _scan_both_directions(prep_chunk, q_s, k_s, k_s, laf_s, lab_s, v_ref, *scan_s, n_ctx_chunks)

    def fin(i, carry):
        sl = pl.ds(pl.multiple_of(i * c, c), c)
        for h in range(nh):
            cols = slice(h * HEAD_DV, (h + 1) * HEAD_DV)
            o = o_s[0, sl, cols] + o_s[1, sl, cols]
            y_ref[sl, cols] = _rms_gate(o, gain_ref[...], g_ref[sl, cols].astype(F32)).astype(y_ref.dtype)
        return carry

    lax.fori_loop(0, n_chunks, fin, 0, unroll=2)


def _gla_mixer(p, p_gk, wg, bg, cos, sin, gain, ctx_len, col_q, col_k, col_v, col_g):
    b, tt, _ = p.shape
    nh = LANES // B_DK
    wide = nh * HEAD_DV
    pair = lambda c0: pl.BlockSpec((None, tt, LANES), lambda i, h: (i, 0, c0 + h))
    pair_v = lambda c0: pl.BlockSpec((None, tt, wide), lambda i, h: (i, 0, c0 * LANES // wide + h))
    seq = lambda: pltpu.VMEM((tt, LANES), F32)
    assert (col_v * LANES) % wide == 0 and (col_g * LANES) % wide == 0
    return pl.pallas_call(
        functools.partial(_gla_kernel, ctx_len // SCAN_CHUNK),
        grid=(b, N_HEADS // nh),
        in_specs=[pair(col_q), pair(col_k), pair_v(col_v), pair_v(col_g),
                  pl.BlockSpec((None, tt, LANES), lambda i, h: (i, 0, 0)),
                  pl.BlockSpec((2, None, LANES, LANES), lambda i, h: (0, h, 0, 0)),
                  pl.BlockSpec((2, None, 1, LANES), lambda i, h: (0, h, 0, 0)),
                  pl.BlockSpec((tt, LANES), lambda i, h: (0, 0)),
                  pl.BlockSpec((tt, LANES), lambda i, h: (0, 0)),
                  pl.BlockSpec((1, LANES), lambda i, h: (0, 0))],
        out_specs=pl.BlockSpec((None, tt, wide), lambda i, h: (i, 0, h)),
        out_shape=jax.ShapeDtypeStruct((b, tt, MIX_W), BF16),
        scratch_shapes=[seq(), seq(), seq(), seq()] + _scan_scratch(tt, nh),
        compiler_params=_params(("arbitrary", "arbitrary")),
        name="gla_mixer",
    )(p, p, p, p, p_gk, wg, bg, cos, sin, gain.reshape(1, HEAD_DV))


def _natten_kernel(ctx_len, rows, kr, q_ref, k_ref, v_ref, rpb_ref, y_ref, bias_ref):
    scale = HEAD_DV ** -0.5
    kc = k_ref[0:ctx_len, :]
    vc = v_ref[0:ctx_len, :]

    qi = lax.broadcasted_iota(jnp.int32, (GRID_W, LANES), 0)
    kj = lax.broadcasted_iota(jnp.int32, (GRID_W, LANES), 1) % GRID_W
    col_start = jnp.clip(qi - KC // 2, 0, GRID_W - KC)
    col_ok = (kj >= col_start) & (kj < col_start + KC)
    for v in range(kr):
        for blk in range(kr * GRID_W // LANES):
            d = v + blk * (LANES // GRID_W)
            src = jnp.broadcast_to(rpb_ref[d:d + 1, :], (GRID_W, LANES))
            rot = pltpu.roll(src, LANES - (KC - 1), 1, stride=1, stride_axis=0)
            bias_ref[v, :, blk * LANES:(blk + 1) * LANES] = jnp.where(col_ok, rot, MASK_VALUE)

    s = _dot_nt(q_ref[0:ctx_len, :], kc) * scale
    p = jnp.exp(s - jnp.max(s, axis=-1, keepdims=True))
    o = _dot(p.astype(BF16), vc) / jnp.sum(p, axis=-1, keepdims=True)
    y_ref[0:ctx_len, :] = o.astype(y_ref.dtype)

    nr = NATTEN_ROWS_PER_TRIP
    assert rows % nr == 0

    def body(i, carry):
        q_sl, b_sl, s_b, s_c = [], [], [], []
        for u in range(nr):
            r = i * nr + u
            start = jnp.clip(r - kr // 2, 0, rows - kr)
            q_sl.append(pl.ds(pl.multiple_of(ctx_len + r * GRID_W, GRID_W), GRID_W))
            b_sl.append(pl.ds(pl.multiple_of(ctx_len + start * GRID_W, GRID_W), kr * GRID_W))
            q = q_ref[q_sl[u], :]
            s_b.append(_dot_nt(q, k_ref[b_sl[u], :]) * scale + bias_ref[start - r + (KR_MAX - 1)])
            s_c.append(_dot_nt(q, kc) * scale)
        for u in range(nr):
            mx = jnp.maximum(jnp.max(s_b[u], axis=-1, keepdims=True), jnp.max(s_c[u], axis=-1, keepdims=True))
            p_b = jnp.exp(s_b[u] - mx)
            p_c = jnp.exp(s_c[u] - mx)
            den = jnp.sum(p_b, axis=-1, keepdims=True) + jnp.sum(p_c, axis=-1, keepdims=True)
            o = (_dot(p_b.astype(BF16), v_ref[b_sl[u], :]) + _dot(p_c.astype(BF16), vc)) / den
            y_ref[q_sl[u], :] = o.astype(y_ref.dtype)
        return carry

    lax.fori_loop(0, rows // nr, body, 0)


def _natten_mixer(p, rpb, ctx_len, col_q, col_k, col_v):
    b, tt, _ = p.shape
    rows = (tt - ctx_len) // GRID_W
    kr = KR_MAX
    assert rows >= KR_MAX and rows % 2 == 0 and 2 * GRID_W == LANES and 2 * KC - 1 <= GRID_W
    padded = jnp.pad(rpb.astype(F32), ((0, 0), (0, 0), (0, GRID_W - (2 * KC - 1))))
    pairs = jnp.concatenate([padded[:, :-1], padded[:, 1:]], axis=-1)
    head = lambda c0: pl.BlockSpec((None, tt, LANES), lambda i, h: (i, 0, c0 + h))
    return pl.pallas_call(
        functools.partial(_natten_kernel, ctx_len, rows, kr),
        grid=(b, N_HEADS),
        in_specs=[head(col_q), head(col_k), head(col_v),
                  pl.BlockSpec((None, 2 * KR_MAX - 2, LANES), lambda i, h: (h, 0, 0))],
        out_specs=pl.BlockSpec((None, tt, LANES), lambda i, h: (i, 0, h)),
        out_shape=jax.ShapeDtypeStruct((b, tt, MIX_W), BF16),
        scratch_shapes=[pltpu.VMEM((kr, GRID_W, kr * GRID_W), F32)],
        compiler_params=_params(("arbitrary", "arbitrary")),
        name="natten_mixer",
    )(p, p, p, pairs)


def _merge_kernel(ya_ref, yb_ref, yc_ref, w_ref, ga_ref, gb_ref, gc_ref, o_ref):
    acc = None
    for n, (y_ref, g_ref) in enumerate(((ya_ref, ga_ref), (yb_ref, gb_ref), (yc_ref, gc_ref))):
        z = jax.nn.sigmoid(g_ref[...].astype(F32)) * _dot(y_ref[...], w_ref[n])
        acc = z if acc is None else acc + z
    o_ref[...] = acc.astype(o_ref.dtype)


def _merge(ya, yb, yc, w_branch, layer, p, col_gate, d_model):
    m = ya.shape[0]
    tm = _pick_tile(m, (512, 256, 128))
    tn = _pick_tile(d_model, MATMUL_COL_TILES)
    nj = d_model // tn
    assert (col_gate * LANES) % tn == 0
    yspec = pl.BlockSpec((tm, MIX_W), lambda j, i: (i, 0))
    gspec = lambda n: pl.BlockSpec((tm, tn), lambda j, i, n=n: (i, col_gate * LANES // tn + n * nj + j))
    return pl.pallas_call(
        _merge_kernel,
        grid=(nj, m // tm),
        in_specs=[yspec, yspec, yspec,
                  pl.BlockSpec((None, N_BRANCH, MIX_W, tn), lambda j, i: (layer, 0, 0, j)),
                  gspec(0), gspec(1), gspec(2)],
        out_specs=pl.BlockSpec((tm, tn), lambda j, i: (i, j)),
        out_shape=jax.ShapeDtypeStruct((m, d_model), BF16),
        compiler_params=_params(("arbitrary", "arbitrary")),
        name="branch_merge",
    )(ya, yb, yc, w_branch, p, p, p)


def _proj_ln_kernel(alpha, ctx_len, tiles_per_batch, emit_h, a_ref, w_ref, x_ref, mod_ref, ln_ref, *rest):
    if emit_h:
        xo_ref, ho_ref, acc_ref = rest
    else:
        xo_ref, acc_ref = rest
    kk = pl.program_id(1)

    @pl.when(kk == 0)
    def _():
        acc_ref[...] = jnp.zeros_like(acc_ref)

    acc_ref[...] += _dot(a_ref[...], w_ref[...])

    @pl.when(kk == pl.num_programs(1) - 1)
    def _():
        tm, d = acc_ref.shape
        rb = LN_ROW_BLOCK
        row0 = (pl.program_id(0) % tiles_per_batch) * tm

        def block(i, carry):
            sl = pl.ds(pl.multiple_of(i * rb, rb), rb)
            mod = mod_ref[jnp.where(row0 + i * rb < ctx_len, 1, 0)]
            sel = lambda j: mod[j:j + 1, :]
            y = alpha * x_ref[sl, :] + sel(0) * acc_ref[sl, :]
            mu = jnp.mean(y, axis=-1, keepdims=True)
            yc = y - mu
            var = jnp.mean(jnp.square(yc), axis=-1, keepdims=True)
            xn = yc * lax.rsqrt(var + LN_EPS) * ln_ref[0:1, :] + ln_ref[1:2, :]
            xo_ref[sl, :] = xn
            if emit_h:
                ho_ref[sl, :] = (xn * (1.0 + sel(1)) + sel(2)).astype(BF16)
            return carry

        lax.fori_loop(0, tm // rb, block, 0)


def _proj_ln(a, w, layer, x, mod_sel, ln_gb, alpha, ctx_len, tt, emit_h, name):
    m, k = a.shape
    d = w.shape[2]
    assert w.dtype == BF16
    if k * d * 2 <= RESIDENT_WEIGHT_BYTES:
        tm, tk = _pick_tile(tt, (384, 256, 128)), k
    else:
        tm, tk = _pick_tile(tt, (768, 384, 256, 128)), _pick_tile(k, MATMUL_COL_TILES)
    assert tm % LN_ROW_BLOCK == 0 and ctx_len <= tm and ctx_len % LN_ROW_BLOCK == 0
    tpb = tt // tm
    out_shape = [jax.ShapeDtypeStruct((m, d), F32)]
    out_specs = [pl.BlockSpec((tm, d), lambda i, j: (i, 0))]
    if emit_h:
        out_shape.append(jax.ShapeDtypeStruct((m, d), BF16))
        out_specs.append(pl.BlockSpec((tm, d), lambda i, j: (i, 0)))
    res = pl.pallas_call(
        functools.partial(_proj_ln_kernel, alpha, ctx_len, tpb, emit_h),
        grid=(m // tm, k // tk),
        in_specs=[pl.BlockSpec((tm, tk), lambda i, j: (i, j)),
                  pl.BlockSpec((None, tk, d), lambda i, j: (layer, j, 0)),
                  pl.BlockSpec((tm, d), lambda i, j: (i, 0)),
                  pl.BlockSpec((None, 2, 3, d), lambda i, j: (i // tpb, 0, 0, 0)),
                  pl.BlockSpec((2, d), lambda i, j: (0, 0))],
        out_specs=out_specs,
        out_shape=out_shape,
        scratch_shapes=[pltpu.VMEM((tm, d), F32)],
        compiler_params=_params(("arbitrary", "arbitrary")),
        name=name,
    )(a, w, x, mod_sel, ln_gb)
    return (res[0], res[1]) if emit_h else (res[0], None)


def _rope_tables(seq, ctx_len):
    half = B_DK // 2
    freqs = ROPE_THETA ** (-jnp.arange(0, half, 2, dtype=F32) / half)
    pos = jnp.arange(seq)
    ang_r = (pos // GRID_W).astype(F32)[:, None] * freqs
    ang_c = (pos % GRID_W).astype(F32)[:, None] * freqs
    ang = jnp.concatenate([ang_r, ang_r, ang_c, ang_c], axis=-1)
    ang = jnp.concatenate([jnp.zeros((ctx_len, B_DK), F32), ang], axis=0)
    ang = jnp.concatenate([ang, ang], axis=-1)
    return jnp.cos(ang), jnp.sin(ang)


def kernel(x, c, ctx, c_ctx, w_ada, b_ada, w_in, hgrn_lb_logits, hgrn_norm_g, gla_w_gk2, gla_b_gk2, gla_norm_g,
           natten_rpb, w_branch, w_out, ln1_g, ln1_b, ln2_g, ln2_b, w_mlp1, w_mlp2):
    bsz, seq, d = x.shape
    ctx_len = ctx.shape[1]
    depth = w_ada.shape[0]
    tt = ctx_len + seq
    m_tok = bsz * tt
    alpha = (2 * depth) ** 0.25
    assert ctx_len % SCAN_CHUNK == 0 and seq % SCAN_CHUNK == 0 and seq % GRID_W == 0 and bsz + 1 <= SUBLANES

    seg = N_HEADS * A_DK
    gk_lo = 5 * seg + 2 * N_HEADS * B_DK + 2 * MIX_W
    gk_hi = gk_lo + 2 * GK_RANK
    n_cg = 3 * MIX_W + N_BRANCH * d
    assert gk_lo % 1024 == 0 and w_in.shape[2] == gk_hi + n_cg
    col_a = 0
    col_bq = 5 * seg // LANES
    col_bk = col_bq + N_HEADS * B_DK // LANES
    col_bv = col_bk + N_HEADS * B_DK // LANES
    col_bg = col_bv + MIX_W // LANES
    col_cq = 0
    col_ck = col_cq + MIX_W // LANES
    col_cv = col_ck + MIX_W // LANES
    col_gate = col_cv + MIX_W // LANES

    c_rows = jnp.zeros((SUBLANES, d), F32).at[:bsz].set(c).at[bsz].set(c_ctx)
    mod = _ada_all_layers(c_rows, w_ada, b_ada).reshape(depth, SUBLANES, 6, d)

    def mod_sel(l, idx):
        lat = mod[l, :bsz][:, idx, :]
        cx = jnp.broadcast_to(mod[l, bsz][idx, :], lat.shape)
        return jnp.stack([lat, cx], axis=1)

    lb_p = jax.nn.softmax(hgrn_lb_logits.astype(F32), axis=0)
    lb_cum = jnp.cumsum(lb_p, axis=0)
    lower_bounds = jnp.concatenate([jnp.zeros_like(lb_cum[:1]), lb_cum[:-1]], axis=0)

    cos, sin = _rope_tables(seq, ctx_len)
    w_out_bf = w_out.astype(BF16)
    w_mlp2_bf = w_mlp2.astype(BF16)

    xs = jnp.concatenate([ctx, x], axis=1)
    h = _modulate(xs, mod_sel(0, np.array([1, 0])), ctx_len).reshape(m_tok, d)
    xs = xs.reshape(m_tok, d)

    w_in_t = jnp.swapaxes(w_in, 1, 2)
    w_branch_bf = w_branch.astype(BF16)

    for l in range(depth):
        p = _matmul_t(h, w_in_t, l, 0, gk_lo, "in_proj_ab").reshape(bsz, tt, gk_lo)
        p_gk = _matmul_t(h, w_in_t, l, gk_lo, LANES, "in_proj_gk").reshape(bsz, tt, LANES)
        p_cg = _matmul_t(h, w_in_t, l, gk_hi, n_cg, "in_proj_cg").reshape(bsz, tt, n_cg)

        n_pairs = N_HEADS * B_DK // LANES
        wg = jnp.zeros((2, n_pairs, LANES, LANES), F32)
        w2 = gla_w_gk2[l].reshape(2, GK_RANK, n_pairs, LANES).transpose(0, 2, 1, 3)
        wg = wg.at[0, :, :GK_RANK].set(w2[0]).at[1, :, GK_RANK:2 * GK_RANK].set(w2[1]).astype(BF16)
        bg = gla_b_gk2[l].reshape(2, n_pairs, 1, LANES)

        y_a = _hgrn_mixer(p, lower_bounds[l], hgrn_norm_g[l], ctx_len, col_a)
        y_b = _gla_mixer(p, p_gk, wg, bg, cos, sin, gla_norm_g[l], ctx_len, col_bq, col_bk, col_bv, col_bg)
        y_c = _natten_mixer(p_cg, natten_rpb[l], ctx_len, col_cq, col_ck, col_cv)

        merged = _merge(y_a.reshape(m_tok, MIX_W), y_b.reshape(m_tok, MIX_W), y_c.reshape(m_tok, MIX_W),
                        w_branch_bf, l, p_cg.reshape(m_tok, n_cg), col_gate, d)
        xs, h2 = _proj_ln(merged, w_out_bf, l, xs, mod_sel(l, np.array([2, 4, 3])),
                          jnp.stack([ln1_g[l], ln1_b[l]]), alpha, ctx_len, tt, True, "out_proj_ln")
        u = _matmul_sq_relu(h2, w_mlp1, l, "mlp_up")
        last = l == depth - 1
        if last:
            nxt = jnp.zeros((bsz, 2, 3, d), F32).at[:, :, 0].set(mod_sel(l, np.array([5]))[:, :, 0])
        else:
            nxt = jnp.concatenate([mod_sel(l, np.array([5])), mod_sel(l + 1, np.array([1, 0]))], axis=2)
        xs, h = _proj_ln(u, w_mlp2_bf, l, xs, nxt, jnp.stack([ln2_g[l], ln2_b[l]]),
                         alpha, ctx_len, tt, not last, "mlp_down_ln")

    return xs.reshape(bsz, tt, d)[:, ctx_len:, :]
```

```python
import functools

import jax
import jax.numpy as jnp
import numpy as np
from jax import lax
from jax.experimental import pallas as pl
from jax.experimental.pallas import tpu as pltpu

GRID_W = 64
N_BRANCH = 3
MIX_W = 1024
N_HEADS = 8
A_DK = 128
B_DK = 64
HEAD_DV = MIX_W // N_HEADS
GK_RANK = 16
GATE_LOGIT_NORMALIZER = 16.0
KR_MAX = 8
KC = 16
ROPE_THETA = 10000.0
RMS_EPS = 1e-6
LN_EPS = 1e-5
MASK_VALUE = -1e30
LOG2E = 1.4426950408889634

LANES = 128
SUBLANES = 8
BF16_ROWS_PER_TILE = 16
SCAN_CHUNK = 128
SCAN_UNROLL = 3
NATTEN_ROWS_PER_TRIP = 16
LN_ROW_BLOCK = 128
VMEM_LIMIT_BYTES = 56 * 1024 * 1024
RESIDENT_WEIGHT_BYTES = 16 * 1024 * 1024
MATMUL_ROW_TILES = (1536, 1152, 1024, 512, 256, 128)
MATMUL_COL_TILES = (1024, 512, 256, 128)

F32 = jnp.float32
BF16 = jnp.bfloat16


def _pick_tile(n, prefs):
    for t in prefs:
        if n % t == 0:
            return t
    return n


def _params(sem):
    return pltpu.CompilerParams(dimension_semantics=sem, vmem_limit_bytes=VMEM_LIMIT_BYTES)


def _dot(a, b):
    return jnp.dot(a, b, preferred_element_type=F32)


def _dot_nt(a, b):
    return lax.dot_general(a, b, (((1,), (1,)), ((), ())), preferred_element_type=F32)


def _ada_kernel(c_ref, w_ref, b_ref, o_ref):
    c = c_ref[...]
    s = (c * jax.nn.sigmoid(c)).astype(BF16)
    o_ref[...] = _dot(s, w_ref[...].astype(BF16)) + b_ref[...]


def _ada_all_layers(c_rows, w_ada, b_ada):
    depth, d, n = w_ada.shape
    tn = _pick_tile(n, MATMUL_COL_TILES)
    return pl.pallas_call(
        _ada_kernel,
        grid=(depth, n // tn),
        in_specs=[pl.BlockSpec((SUBLANES, d), lambda l, j: (0, 0)),
                  pl.BlockSpec((None, d, tn), lambda l, j: (l, 0, j)),
                  pl.BlockSpec((None, 1, tn), lambda l, j: (l, 0, j))],
        out_specs=pl.BlockSpec((None, SUBLANES, tn), lambda l, j: (l, 0, j)),
        out_shape=jax.ShapeDtypeStruct((depth, SUBLANES, n), F32),
        compiler_params=_params(("arbitrary", "arbitrary")),
        name="ada_mod",
    )(c_rows, w_ada, b_ada.reshape(depth, 1, n))


def _modulate_kernel(ctx_len, x_ref, m_ref, h_ref):
    x = x_ref[...]
    rows = lax.broadcasted_iota(jnp.int32, x.shape, 0) + pl.program_id(1) * x.shape[0]
    is_ctx = rows < ctx_len
    sc = jnp.where(is_ctx, m_ref[1, 0:1, :], m_ref[0, 0:1, :])
    sh = jnp.where(is_ctx, m_ref[1, 1:2, :], m_ref[0, 1:2, :])
    h_ref[...] = (x * (1.0 + sc) + sh).astype(BF16)


def _modulate(x, mod_sel, ctx_len):
    b, tt, d = x.shape
    tm = _pick_tile(tt, (768, 384, 256, 128))
    return pl.pallas_call(
        functools.partial(_modulate_kernel, ctx_len),
        grid=(b, tt // tm),
        in_specs=[pl.BlockSpec((None, tm, d), lambda i, j: (i, j, 0)),
                  pl.BlockSpec((None, 2, 2, d), lambda i, j: (i, 0, 0, 0))],
        out_specs=pl.BlockSpec((None, tm, d), lambda i, j: (i, j, 0)),
        out_shape=jax.ShapeDtypeStruct((b, tt, d), BF16),
        compiler_params=_params(("arbitrary", "arbitrary")),
        name="modulate",
    )(x, mod_sel)


def _mm_sq_relu_kernel(a_ref, w_ref, o_ref, wb_ref):
    @pl.when(pl.program_id(1) == 0)
    def _():
        wb_ref[...] = w_ref[...].astype(BF16)

    o_ref[...] = jnp.square(jnp.maximum(_dot(a_ref[...], wb_ref[...]), 0.0)).astype(o_ref.dtype)


def _matmul_sq_relu(a, w, layer, name):
    m, k = a.shape
    n = w.shape[2]
    tm = _pick_tile(m, MATMUL_ROW_TILES)
    tn = _pick_tile(n, MATMUL_COL_TILES)
    return pl.pallas_call(
        _mm_sq_relu_kernel,
        grid=(n // tn, m // tm),
        in_specs=[pl.BlockSpec((tm, k), lambda j, i: (i, 0)),
                  pl.BlockSpec((None, k, tn), lambda j, i: (layer, 0, j))],
        out_specs=pl.BlockSpec((tm, tn), lambda j, i: (i, j)),
        out_shape=jax.ShapeDtypeStruct((m, n), BF16),
        scratch_shapes=[pltpu.VMEM((k, tn), BF16)],
        compiler_params=_params(("arbitrary", "arbitrary")),
        name=name,
    )(a, w)


def _mm_t_kernel(a_ref, w_ref, o_ref, wb_ref):
    @pl.when(pl.program_id(1) == 0)
    def _():
        wb_ref[...] = w_ref[0].astype(BF16)

    o_ref[...] = _dot_nt(a_ref[...], wb_ref[...]).astype(o_ref.dtype)


def _matmul_t(a, wt, layer, row0, n, name):
    m, k = a.shape
    tm = _pick_tile(m, MATMUL_ROW_TILES)
    tn = _pick_tile(n, MATMUL_COL_TILES)
    assert row0 % BF16_ROWS_PER_TILE == 0 and wt.shape[1] >= row0 + n
    return pl.pallas_call(
        _mm_t_kernel,
        grid=(n // tn, m // tm),
        in_specs=[pl.BlockSpec((tm, k), lambda j, i: (i, 0)),
                  pl.BlockSpec((pl.Element(1), pl.Element(tn), pl.Element(k)),
                               lambda j, i: (layer, pl.multiple_of(row0 + j * tn, BF16_ROWS_PER_TILE), 0))],
        out_specs=pl.BlockSpec((tm, tn), lambda j, i: (i, j)),
        out_shape=jax.ShapeDtypeStruct((m, n), BF16),
        scratch_shapes=[pltpu.VMEM((tn, k), BF16)],
        compiler_params=_params(("arbitrary", "arbitrary")),
        name=name,
    )(a, wt)


def _level_operand(q, k, la, bc, m, rev):
    c, dk = q.shape
    if m >= SUBLANES:
        pieces = []
        for jb in range(c // m):
            rows = slice(jb * m, (jb + 1) * m)
            r = (jb // 2) * 2 * m + (m if rev else m - 1)
            cvec = bc[r:r + 1, :]
            if (jb % 2 == 1) != rev:
                pieces.append(q[rows, :] * jnp.exp2(bc[rows, :] - cvec))
            else:
                pieces.append(k[rows, :] * jnp.exp2(cvec - bc[rows, :]))
        return jnp.concatenate(pieces, axis=0)
    shape3 = (c // SUBLANES, SUBLANES, dk)
    sub = lax.broadcasted_iota(jnp.int32, (1, SUBLANES, dk), 1)
    upper = (sub & m) != 0
    q_role = jnp.logical_not(upper) if rev else upper
    q3, k3, bc3 = q.reshape(shape3), k.reshape(shape3), bc.reshape(shape3)
    if m == 1:
        e2 = jnp.where(q_role, la.reshape(shape3), 0.0)
    else:
        if m == 4:
            r = 4 if rev else 3
            c3 = bc3[:, r:r + 1, :]
        else:
            r0, r1 = (2, 6) if rev else (1, 5)
            c3 = jnp.where(sub < 4, bc3[:, r0:r0 + 1, :], bc3[:, r1:r1 + 1, :])
        e2 = (bc3 - c3) * jnp.where(q_role, 1.0, -1.0)
    return (jnp.where(q_role, q3, k3) * jnp.exp2(e2)).reshape(c, dk)


def _level_matrix(c, rev):
    t = lax.broadcasted_iota(jnp.int32, (c, c), 0)
    s = lax.broadcasted_iota(jnp.int32, (c, c), 1)
    x = t ^ s
    lvl = jnp.full((c, c), -1, jnp.int32)
    b = 1
    while b < c:
        lvl = lvl + (x >= b).astype(jnp.int32)
        b *= 2
    return jnp.where((t < s) if rev else (t > s), lvl, -1)


def _scan_both_directions(prep_chunk, q_s, kf_s, kb_s, laf_s, lab_s, v_ref, o_s, st_s, bc_s, vt_s, sc_s, xt_s,
                          n_ctx_chunks):
    c = SCAN_CHUNK
    n_chunks = q_s.shape[0] // c
    nh = st_s.shape[1]
    dv = v_ref.shape[1] // nh
    ti = lax.broadcasted_iota(jnp.int32, (c, c), 0)
    si = lax.broadcasted_iota(jnp.int32, (c, c), 1)
    lane = lax.broadcasted_iota(jnp.int32, (c, LANES), 1) // (LANES // nh)
    head_lanes = [(lane == h, (lane == h).astype(BF16)) for h in range(nh)]
    k_refs, la_refs = (kf_s, kb_s), (laf_s, lab_s)
    tris = ((si <= ti).astype(BF16), (si >= ti).astype(BF16))
    lvls = (_level_matrix(c, False), _level_matrix(c, True))

    def cumulate(i, carry):
        sl = pl.ds(pl.multiple_of(i * c, c), c)
        las = prep_chunk(sl)
        for h in range(nh):
            vt_s[i, h * dv:(h + 1) * dv, :] = v_ref[sl, h * dv:(h + 1) * dv].T
        for d in range(2):
            la_hi = las[d].astype(BF16)
            la_lo = (las[d] - la_hi.astype(F32)).astype(BF16)
            bb = _dot(tris[d], jnp.concatenate([la_hi, la_lo], axis=1))
            bc_s[d, sl, :] = (bb[:, :LANES] + bb[:, LANES:]) * LOG2E
            la_refs[d][sl, :] = las[d] * LOG2E
        return carry

    lax.fori_loop(0, n_chunks, cumulate, 0, unroll=3)

    st_s[...] = jnp.zeros_like(st_s)
    nu = SCAN_UNROLL
    n_groups = n_chunks // nu
    assert n_chunks % nu == 0 and n_groups >= 2
    streams = [(d, u) for u in range(nu) for d in range(2)]
    slot = {(su, h): (su[0] * nu + su[1]) * nh + h for su in streams for h in range(nh)}

    def group_rows(g):
        seq = [g * nu + u for u in range(nu)]
        cis = [seq, [jnp.where(j < n_ctx_chunks, n_ctx_chunks - 1 - j, n_chunks - 1 + n_ctx_chunks - j) for j in seq]]
        return cis, {(d, u): pl.ds(pl.multiple_of(cis[d][u] * c, c), c) for d, u in streams}

    def state_phase(g):
        cis, sl = group_rows(g)
        q = {su: q_s[sl[su], :] for su in streams}
        k = {(d, u): k_refs[d][sl[d, u], :] for d, u in streams}
        la = {(d, u): la_refs[d][sl[d, u], :] for d, u in streams}
        bc = {(d, u): bc_s[d, sl[d, u], :] for d, u in streams}
        st = [[st_s[d, h] for h in range(nh)] for d in range(2)]
        inter = {}
        for d, u in streams:
            su = (d, u)
            qg = (q[su] * jnp.exp2(bc[su])).astype(BF16)
            tot = bc[su][0:1, :] if d else bc[su][c - 1:c, :]
            kg = (k[su] * jnp.exp2(tot - bc[su])).astype(BF16)
            dec = jnp.exp2(tot)
            for h in range(nh):
                qg_h = qg if nh == 1 else qg * head_lanes[h][1]
                inter[su, h] = _dot_nt(qg_h, st[d][h].astype(BF16))
                st[d][h] = st[d][h] * dec + _dot(vt_s[cis[d][u], h * dv:(h + 1) * dv, :], kg)
        for d in range(2):
            for h in range(nh):
                st_s[d, h] = st[d][h]
        return sl, q, k, la, bc, inter

    def score_matmuls(g):
        _, sl = group_rows(g)
        return sl, {(su, h): _dot(sc_s[slot[su, h]], v_ref[sl[su], h * dv:(h + 1) * dv])
                    for su in streams for h in range(nh)}

    def level_phase(sl, q, k, la, bc, inter):
        scores = {key: jnp.zeros((c, c), F32) for key in slot}
        m, level = c // 2, (c // 2).bit_length() - 1
        while m >= 1:
            for d, u in streams:
                su = (d, u)
                x = _level_operand(q[su], k[su], la[su], bc[su], m, d == 1).astype(BF16)
                if nh == 1:
                    xt_s[slot[su, 0]] = x.T
                    s_all = _dot(x, xt_s[slot[su, 0]])
                else:
                    x_heads = jnp.concatenate([x * head_lanes[h][1] for h in range(nh)], axis=0)
                    s_all = _dot_nt(x, x_heads)
                for h in range(nh):
                    scores[su, h] = jnp.where(lvls[d] == level, s_all[:, h * c:(h + 1) * c], scores[su, h])
            m, level = m // 2, level - 1
        for d, u in streams:
            su = (d, u)
            qk = q[su] * k[su]
            for h in range(nh):
                sc_s[slot[su, h]] = scores[su, h].astype(BF16)
                v = v_ref[sl[su], h * dv:(h + 1) * dv].astype(F32)
                qk_h = qk if nh == 1 else jnp.where(head_lanes[h][0], qk, 0.0)
                o_s[d, sl[su], h * dv:(h + 1) * dv] = inter[su, h] + jnp.sum(qk_h, axis=1, keepdims=True) * v

    def add_scores_part(sl, outs):
        for (su, h), o in outs.items():
            cols = slice(h * dv, (h + 1) * dv)
            o_s[su[0], sl[su], cols] = o_s[su[0], sl[su], cols] + o

    level_phase(*state_phase(0))

    def body(g, carry):
        cur = state_phase(g)
        sl_prev, outs = score_matmuls(g - 1)
        level_phase(*cur)
        add_scores_part(sl_prev, outs)
        return carry

    lax.fori_loop(1, n_groups, body, 0)
    add_scores_part(*score_matmuls(n_groups - 1))


def _scan_scratch(tt, nh):
    n_chunks = tt // SCAN_CHUNK
    return [pltpu.VMEM((2, tt, nh * HEAD_DV), F32), pltpu.VMEM((2, nh, HEAD_DV, LANES), F32),
            pltpu.VMEM((2, tt, LANES), F32), pltpu.VMEM((n_chunks, nh * HEAD_DV, SCAN_CHUNK), BF16),
            pltpu.VMEM((2 * SCAN_UNROLL * nh, SCAN_CHUNK, SCAN_CHUNK), BF16),
            pltpu.VMEM((2 * SCAN_UNROLL * nh, LANES, SCAN_CHUNK), BF16)]


def _rms_gate(o, gain, gate):
    o = o * lax.rsqrt(jnp.mean(jnp.square(o), axis=-1, keepdims=True) + RMS_EPS) * gain
    return o * (gate * jax.nn.sigmoid(gate))


def _hgrn_kernel(n_ctx_chunks, q_ref, i_ref, g_ref, ff_ref, fb_ref, lb_ref, gain_ref, y_ref,
                 q_s, kf_s, kb_s, laf_s, lab_s, *scan_s):
    c = SCAN_CHUNK
    n_chunks = q_s.shape[0] // c
    o_s = scan_s[0]
    lb_f = lb_ref[0:1, :]
    lb_b = lb_ref[1:2, :]

    def prep_chunk(sl):
        q = q_ref[sl, :].astype(F32)
        q_s[sl, :] = q * jax.nn.sigmoid(q) * (A_DK ** -0.5)
        las = []
        for f_ref, lb, k_s in ((ff_ref, lb_f, kf_s), (fb_ref, lb_b, kb_s)):
            s = jax.nn.sigmoid(f_ref[sl, :].astype(F32))
            k_s[sl, :] = (1.0 - lb) * (1.0 - s)
            las.append(jnp.log(lb + (1.0 - lb) * s))
        return las

    _scan_both_directions(prep_chunk, q_s, kf_s, kb_s, laf_s, lab_s, i_ref, *scan_s, n_ctx_chunks)

    def fin(i, carry):
        sl = pl.ds(pl.multiple_of(i * c, c), c)
        o = o_s[0, sl, :] + o_s[1, sl, :]
        y_ref[sl, :] = _rms_gate(o, gain_ref[...], g_ref[sl, :].astype(F32)).astype(y_ref.dtype)
        return carry

    lax.fori_loop(0, n_chunks, fin, 0, unroll=2)


def _hgrn_mixer(p, lb, gain, ctx_len, col0):
    b, tt, _ = p.shape
    blk = lambda seg: pl.BlockSpec((None, tt, LANES), lambda i, h, s=seg: (i, 0, col0 + s * N_HEADS + h))
    seq = lambda: pltpu.VMEM((tt, LANES), F32)
    return pl.pallas_call(
        functools.partial(_hgrn_kernel, ctx_len // SCAN_CHUNK),
        grid=(b, N_HEADS),
        in_specs=[blk(0), blk(1), blk(2), blk(3), blk(4),
                  pl.BlockSpec((2, LANES), lambda i, h: (0, h)),
                  pl.BlockSpec((1, LANES), lambda i, h: (0, 0))],
        out_specs=pl.BlockSpec((None, tt, LANES), lambda i, h: (i, 0, h)),
        out_shape=jax.ShapeDtypeStruct((b, tt, MIX_W), BF16),
        scratch_shapes=[seq(), seq(), seq(), seq(), seq()] + _scan_scratch(tt, 1),
        compiler_params=_params(("arbitrary", "arbitrary")),
        name="hgrn2_mixer",
    )(p, p, p, p, p, lb, gain.reshape(1, HEAD_DV))


def _rope_rotate(x):
    n = x.shape[-1]
    lane = lax.broadcasted_iota(jnp.int32, x.shape, x.ndim - 1)
    first = (lane % 32) < 16
    return jnp.where(first, -pltpu.roll(x, n - 16, x.ndim - 1), pltpu.roll(x, 16, x.ndim - 1))


def _gla_kernel(n_ctx_chunks, q_ref, k_ref, v_ref, g_ref, gk_ref, wg_ref, bg_ref, cos_ref, sin_ref,
                gain_ref, y_ref, q_s, k_s, laf_s, lab_s, *scan_s):
    c = SCAN_CHUNK
    n_chunks = q_s.shape[0] // c
    nh = LANES // B_DK
    o_s = scan_s[0]

    def prep_chunk(sl):
        cos = cos_ref[sl, :]
        sin = sin_ref[sl, :]
        q = q_ref[sl, :].astype(F32)
        k = k_ref[sl, :].astype(F32)
        q_s[sl, :] = (q * cos + _rope_rotate(q) * sin) * (B_DK ** -0.5)
        k_s[sl, :] = k * cos + _rope_rotate(k) * sin
        gk = gk_ref[sl, :]
        return [jax.nn.log_sigmoid(_dot(gk, wg_ref[d]) + bg_ref[d]) / GATE_LOGIT_NORMALIZER for d in range(2)]

    _scan_both_directions(prep_chunk, q_s, k_s, k_s, laf_s, lab_s, v_ref, *scan_s, n_ctx_chunks)

    def fin(i, carry):
        sl = pl.ds(pl.multiple_of(i * c, c), c)
        for h in range(nh):
            cols = slice(h * HEAD_DV, (h + 1) * HEAD_DV)
            o = o_s[0, sl, cols] + o_s[1, sl, cols]
            y_ref[sl, cols] = _rms_gate(o, gain_ref[...], g_ref[sl, cols].astype(F32)).astype(y_ref.dtype)
        return carry

    lax.fori_loop(0, n_chunks, fin, 0, unroll=2)


def _gla_mixer(p, p_gk, wg, bg, cos, sin, gain, ctx_len, col_q, col_k, col_v, col_g):
    b, tt, _ = p.shape
    nh = LANES // B_DK
    wide = nh * HEAD_DV
    pair = lambda c0: pl.BlockSpec((None, tt, LANES), lambda i, h: (i, 0, c0 + h))
    pair_v = lambda c0: pl.BlockSpec((None, tt, wide), lambda i, h: (i, 0, c0 * LANES // wide + h))
    seq = lambda: pltpu.VMEM((tt, LANES), F32)
    assert (col_v * LANES) % wide == 0 and (col_g * LANES) % wide == 0
    return pl.pallas_call(
        functools.partial(_gla_kernel, ctx_len // SCAN_CHUNK),
        grid=(b, N_HEADS // nh),
        in_specs=[pair(col_q), pair(col_k), pair_v(col_v), pair_v(col_g),
                  pl.BlockSpec((None, tt, LANES), lambda i, h: (i, 0, 0)),
                  pl.BlockSpec((2, None, LANES, LANES), lambda i, h: (0, h, 0, 0)),
                  pl.BlockSpec((2, None, 1, LANES), lambda i, h: (0, h, 0, 0)),
                  pl.BlockSpec((tt, LANES), lambda i, h: (0, 0)),
                  pl.BlockSpec((tt, LANES), lambda i, h: (0, 0)),
                  pl.BlockSpec((1, LANES), lambda i, h: (0, 0))],
        out_specs=pl.BlockSpec((None, tt, wide), lambda i, h: (i, 0, h)),
        out_shape=jax.ShapeDtypeStruct((b, tt, MIX_W), BF16),
        scratch_shapes=[seq(), seq(), seq(), seq()] + _scan_scratch(tt, nh),
        compiler_params=_params(("arbitrary", "arbitrary")),
        name="gla_mixer",
    )(p, p, p, p, p_gk, wg, bg, cos, sin, gain.reshape(1, HEAD_DV))


def _natten_kernel(ctx_len, rows, kr, q_ref, k_ref, v_ref, rpb_ref, y_ref, bias_ref):
    scale = HEAD_DV ** -0.5
    kc = k_ref[0:ctx_len, :]
    vc = v_ref[0:ctx_len, :]

    qi = lax.broadcasted_iota(jnp.int32, (GRID_W, LANES), 0)
    kj = lax.broadcasted_iota(jnp.int32, (GRID_W, LANES), 1) % GRID_W
    col_start = jnp.clip(qi - KC // 2, 0, GRID_W - KC)
    col_ok = (kj >= col_start) & (kj < col_start + KC)
    for v in range(kr):
        for blk in range(kr * GRID_W // LANES):
            d = v + blk * (LANES // GRID_W)
            src = jnp.broadcast_to(rpb_ref[d:d + 1, :], (GRID_W, LANES))
            rot = pltpu.roll(src, LANES - (KC - 1), 1, stride=1, stride_axis=0)
            bias_ref[v, :, blk * LANES:(blk + 1) * LANES] = jnp.where(col_ok, rot, MASK_VALUE)

    s = _dot_nt(q_ref[0:ctx_len, :], kc) * scale
    p = jnp.exp(s - jnp.max(s, axis=-1, keepdims=True))
    o = _dot(p.astype(BF16), vc) / jnp.sum(p, axis=-1, keepdims=True)
    y_ref[0:ctx_len, :] = o.astype(y_ref.dtype)

    nr = min(NATTEN_ROWS_PER_TRIP, rows)
    assert rows % nr == 0

    def body(i, carry):
        q_sl, b_sl, s_b, s_c = [], [], [], []
        for u in range(nr):
            r = i * nr + u
            start = jnp.clip(r - kr // 2, 0, rows - kr)
            q_sl.append(pl.ds(pl.multiple_of(ctx_len + r * GRID_W, GRID_W), GRID_W))
            b_sl.append(pl.ds(pl.multiple_of(ctx_len + start * GRID_W, GRID_W), kr * GRID_W))
            q = q_ref[q_sl[u], :]
            s_b.append(_dot_nt(q, k_ref[b_sl[u], :]) * scale + bias_ref[start - r + (KR_MAX - 1)])
            s_c.append(_dot_nt(q, kc) * scale)
        for u in range(nr):
            mx = jnp.maximum(jnp.max(s_b[u], axis=-1, keepdims=True), jnp.max(s_c[u], axis=-1, keepdims=True))
            p_b = jnp.exp(s_b[u] - mx)
            p_c = jnp.exp(s_c[u] - mx)
            den = jnp.sum(p_b, axis=-1, keepdims=True) + jnp.sum(p_c, axis=-1, keepdims=True)
            o = (_dot(p_b.astype(BF16), v_ref[b_sl[u], :]) + _dot(p_c.astype(BF16), vc)) / den
            y_ref[q_sl[u], :] = o.astype(y_ref.dtype)
        return carry

    lax.fori_loop(0, rows // nr, body, 0)


def _natten_mixer(p, rpb, ctx_len, col_q, col_k, col_v):
    b, tt, _ = p.shape
    rows = (tt - ctx_len) // GRID_W
    kr = KR_MAX
    assert rows >= KR_MAX and rows % 2 == 0 and 2 * GRID_W == LANES and 2 * KC - 1 <= GRID_W
    padded = jnp.pad(rpb.astype(F32), ((0, 0), (0, 0), (0, GRID_W - (2 * KC - 1))))
    pairs = jnp.concatenate([padded[:, :-1], padded[:, 1:]], axis=-1)
    head = lambda c0: pl.BlockSpec((None, tt, LANES), lambda i, h: (i, 0, c0 + h))
    return pl.pallas_call(
        functools.partial(_natten_kernel, ctx_len, rows, kr),
        grid=(b, N_HEADS),
        in_specs=[head(col_q), head(col_k), head(col_v),
                  pl.BlockSpec((None, 2 * KR_MAX - 2, LANES), lambda i, h: (h, 0, 0))],
        out_specs=pl.BlockSpec((None, tt, LANES), lambda i, h: (i, 0, h)),
        out_shape=jax.ShapeDtypeStruct((b, tt, MIX_W), BF16),
        scratch_shapes=[pltpu.VMEM((kr, GRID_W, kr * GRID_W), F32)],
        compiler_params=_params(("arbitrary", "arbitrary")),
        name="natten_mixer",
    )(p, p, p, pairs)


def _merge_kernel(ya_ref, yb_ref, yc_ref, w_ref, ga_ref, gb_ref, gc_ref, o_ref):
    acc = None
    for n, (y_ref, g_ref) in enumerate(((ya_ref, ga_ref), (yb_ref, gb_ref), (yc_ref, gc_ref))):
        z = jax.nn.sigmoid(g_ref[...].astype(F32)) * _dot(y_ref[...], w_ref[n])
        acc = z if acc is None else acc + z
    o_ref[...] = acc.astype(o_ref.dtype)


def _merge(ya, yb, yc, w_branch, layer, p, col_gate, d_model):
    m = ya.shape[0]
    tm = _pick_tile(m, (512, 256, 128))
    tn = _pick_tile(d_model, MATMUL_COL_TILES)
    nj = d_model // tn
    assert (col_gate * LANES) % tn == 0
    yspec = pl.BlockSpec((tm, MIX_W), lambda j, i: (i, 0))
    gspec = lambda n: pl.BlockSpec((tm, tn), lambda j, i, n=n: (i, col_gate * LANES // tn + n * nj + j))
    return pl.pallas_call(
        _merge_kernel,
        grid=(nj, m // tm),
        in_specs=[yspec, yspec, yspec,
                  pl.BlockSpec((None, N_BRANCH, MIX_W, tn), lambda j, i: (layer, 0, 0, j)),
                  gspec(0), gspec(1), gspec(2)],
        out_specs=pl.BlockSpec((tm, tn), lambda j, i: (i, j)),
        out_shape=jax.ShapeDtypeStruct((m, d_model), BF16),
        compiler_params=_params(("arbitrary", "arbitrary")),
        name="branch_merge",
    )(ya, yb, yc, w_branch, p, p, p)


def _proj_ln_kernel(alpha, ctx_len, tiles_per_batch, emit_h, a_ref, w_ref, x_ref, mod_ref, ln_ref, *rest):
    if emit_h:
        xo_ref, ho_ref, acc_ref = rest
    else:
        xo_ref, acc_ref = rest
    kk = pl.program_id(1)

    @pl.when(kk == 0)
    def _():
        acc_ref[...] = jnp.zeros_like(acc_ref)

    acc_ref[...] += _dot(a_ref[...], w_ref[...])

    @pl.when(kk == pl.num_programs(1) - 1)
    def _():
        tm, d = acc_ref.shape
        rb = LN_ROW_BLOCK
        row0 = (pl.program_id(0) % tiles_per_batch) * tm

        def block(i, carry):
            sl = pl.ds(pl.multiple_of(i * rb, rb), rb)
            mod = mod_ref[jnp.where(row0 + i * rb < ctx_len, 1, 0)]
            sel = lambda j: mod[j:j + 1, :]
            y = alpha * x_ref[sl, :] + sel(0) * acc_ref[sl, :]
            mu = jnp.mean(y, axis=-1, keepdims=True)
            yc = y - mu
            var = jnp.mean(jnp.square(yc), axis=-1, keepdims=True)
            xn = yc * lax.rsqrt(var + LN_EPS) * ln_ref[0:1, :] + ln_ref[1:2, :]
            xo_ref[sl, :] = xn
            if emit_h:
                ho_ref[sl, :] = (xn * (1.0 + sel(1)) + sel(2)).astype(BF16)
            return carry

        lax.fori_loop(0, tm // rb, block, 0)


def _proj_ln(a, w, layer, x, mod_sel, ln_gb, alpha, ctx_len, tt, emit_h, name):
    m, k = a.shape
    d = w.shape[2]
    assert w.dtype == BF16
    if k * d * 2 <= RESIDENT_WEIGHT_BYTES:
        tm, tk = _pick_tile(tt, (384, 256, 128)), k
    else:
        tm, tk = _pick_tile(tt, (768, 384, 256, 128)), _pick_tile(k, MATMUL_COL_TILES)
    assert tm % LN_ROW_BLOCK == 0 and ctx_len <= tm and ctx_len % LN_ROW_BLOCK == 0
    tpb = tt // tm
    out_shape = [jax.ShapeDtypeStruct((m, d), F32)]
    out_specs = [pl.BlockSpec((tm, d), lambda i, j: (i, 0))]
    if emit_h:
        out_shape.append(jax.ShapeDtypeStruct((m, d), BF16))
        out_specs.append(pl.BlockSpec((tm, d), lambda i, j: (i, 0)))
    res = pl.pallas_call(
        functools.partial(_proj_ln_kernel, alpha, ctx_len, tpb, emit_h),
        grid=(m // tm, k // tk),
        in_specs=[pl.BlockSpec((tm, tk), lambda i, j: (i, j)),
                  pl.BlockSpec((None, tk, d), lambda i, j: (layer, j, 0)),
                  pl.BlockSpec((tm, d), lambda i, j: (i, 0)),
                  pl.BlockSpec((None, 2, 3, d), lambda i, j: (i // tpb, 0, 0, 0)),
                  pl.BlockSpec((2, d), lambda i, j: (0, 0))],
        out_specs=out_specs,
        out_shape=out_shape,
        scratch_shapes=[pltpu.VMEM((tm, d), F32)],
        compiler_params=_params(("arbitrary", "arbitrary")),
        name=name,
    )(a, w, x, mod_sel, ln_gb)
    return (res[0], res[1]) if emit_h else (res[0], None)


def _rope_tables(seq, ctx_len):
    half = B_DK // 2
    freqs = ROPE_THETA ** (-jnp.arange(0, half, 2, dtype=F32) / half)
    pos = jnp.arange(seq)
    ang_r = (pos // GRID_W).astype(F32)[:, None] * freqs
    ang_c = (pos % GRID_W).astype(F32)[:, None] * freqs
    ang = jnp.concatenate([ang_r, ang_r, ang_c, ang_c], axis=-1)
    ang = jnp.concatenate([jnp.zeros((ctx_len, B_DK), F32), ang], axis=0)
    ang = jnp.concatenate([ang, ang], axis=-1)
    return jnp.cos(ang), jnp.sin(ang)


def kernel(x, c, ctx, c_ctx, w_ada, b_ada, w_in, hgrn_lb_logits, hgrn_norm_g, gla_w_gk2, gla_b_gk2, gla_norm_g,
           natten_rpb, w_branch, w_out, ln1_g, ln1_b, ln2_g, ln2_b, w_mlp1, w_mlp2):
    bsz, seq, d = x.shape
    ctx_len = ctx.shape[1]
    depth = w_ada.shape[0]
    tt = ctx_len + seq
    m_tok = bsz * tt
    alpha = (2 * depth) ** 0.25
    assert ctx_len % SCAN_CHUNK == 0 and seq % SCAN_CHUNK == 0 and seq % GRID_W == 0 and bsz + 1 <= SUBLANES

    seg = N_HEADS * A_DK
    gk_lo = 5 * seg + 2 * N_HEADS * B_DK + 2 * MIX_W
    gk_hi = gk_lo + 2 * GK_RANK
    n_cg = 3 * MIX_W + N_BRANCH * d
    assert gk_lo % 1024 == 0 and w_in.shape[2] == gk_hi + n_cg
    col_a = 0
    col_bq = 5 * seg // LANES
    col_bk = col_bq + N_HEADS * B_DK // LANES
    col_bv = col_bk + N_HEADS * B_DK // LANES
    col_bg = col_bv + MIX_W // LANES
    col_cq = 0
    col_ck = col_cq + MIX_W // LANES
    col_cv = col_ck + MIX_W // LANES
    col_gate = col_cv + MIX_W // LANES

    c_rows = jnp.zeros((SUBLANES, d), F32).at[:bsz].set(c).at[bsz].set(c_ctx)
    mod = _ada_all_layers(c_rows, w_ada, b_ada).reshape(depth, SUBLANES, 6, d)

    def mod_sel(l, idx):
        lat = mod[l, :bsz][:, idx, :]
        cx = jnp.broadcast_to(mod[l, bsz][idx, :], lat.shape)
        return jnp.stack([lat, cx], axis=1)

    lb_p = jax.nn.softmax(hgrn_lb_logits.astype(F32), axis=0)
    lb_cum = jnp.cumsum(lb_p, axis=0)
    lower_bounds = jnp.concatenate([jnp.zeros_like(lb_cum[:1]), lb_cum[:-1]], axis=0)

    cos, sin = _rope_tables(seq, ctx_len)
    w_out_bf = w_out.astype(BF16)
    w_mlp2_bf = w_mlp2.astype(BF16)

    xs = jnp.concatenate([ctx, x], axis=1)
    h = _modulate(xs, mod_sel(0, np.array([1, 0])), ctx_len).reshape(m_tok, d)
    xs = xs.reshape(m_tok, d)

    w_in_t = jnp.swapaxes(w_in, 1, 2)
    w_branch_bf = w_branch.astype(BF16)

    for l in range(depth):
        p = _matmul_t(h, w_in_t, l, 0, gk_lo, "in_proj_ab").reshape(bsz, tt, gk_lo)
        p_gk = _matmul_t(h, w_in_t, l, gk_lo, LANES, "in_proj_gk").reshape(bsz, tt, LANES)
        p_cg = _matmul_t(h, w_in_t, l, gk_hi, n_cg, "in_proj_cg").reshape(bsz, tt, n_cg)

        n_pairs = N_HEADS * B_DK // LANES
        wg = jnp.zeros((2, n_pairs, LANES, LANES), F32)
        w2 = gla_w_gk2[l].reshape(2, GK_RANK, n_pairs, LANES).transpose(0, 2, 1, 3)
        wg = wg.at[0, :, :GK_RANK].set(w2[0]).at[1, :, GK_RANK:2 * GK_RANK].set(w2[1]).astype(BF16)
        bg = gla_b_gk2[l].reshape(2, n_pairs, 1, LANES)

        y_a = _hgrn_mixer(p, lower_bounds[l], hgrn_norm_g[l], ctx_len, col_a)
        y_b = _gla_mixer(p, p_gk, wg, bg, cos, sin, gla_norm_g[l], ctx_len, col_bq, col_bk, col_bv, col_bg)
        y_c = _natten_mixer(p_cg, natten_rpb[l], ctx_len, col_cq, col_ck, col_cv)

        merged = _merge(y_a.reshape(m_tok, MIX_W), y_b.reshape(m_tok, MIX_W), y_c.reshape(m_tok, MIX_W),
                        w_branch_bf, l, p_cg.reshape(m_tok, n_cg), col_gate, d)
        xs, h2 = _proj_ln(merged, w_out_bf, l, xs, mod_sel(l, np.array([2, 4, 3])),
                          jnp.stack([ln1_g[l], ln1_b[l]]), alpha, ctx_len, tt, True, "out_proj_ln")
        u = _matmul_sq_relu(h2, w_mlp1, l, "mlp_up")
        last = l == depth - 1
        if last:
            nxt = jnp.zeros((bsz, 2, 3, d), F32).at[:, :, 0].set(mod_sel(l, np.array([5]))[:, :, 0])
        else:
            nxt = jnp.concatenate([mod_sel(l, np.array([5])), mod_sel(l + 1, np.array([1, 0]))], axis=2)
        xs, h = _proj_ln(u, w_mlp2_bf, l, xs, nxt, jnp.stack([ln2_g[l], ln2_b[l]]),
                         alpha, ctx_len, tt, not last, "mlp_down_ln")

    return xs.reshape(bsz, tt, d)[:, ctx_len:, :]
```
